```python
import math
import jax, jax.numpy as jnp
from jax import lax
import numpy as np

D_MODEL = 1024
BATCH = 4
SEQ = 4096
DEPTH = 4
DEC_BATCH = 32
DEC_SEQ = 4
PAST_LEN = 8192
PAGE_SIZE = 128

N_MIXERS = 3
N_CONV_LAYERS = (DEPTH + N_MIXERS - 1) // N_MIXERS
N_LRU_LAYERS = (DEPTH + N_MIXERS - 2) // N_MIXERS
N_ATTN_LAYERS = (DEPTH + N_MIXERS - 3) // N_MIXERS

D_CONV = D_MODEL
CONV_A_WIDTH = 3
D_LRU = D_MODEL
LRU_CONV_WIDTH = 4
LRU_BLOCKS = 4
LRU_BLOCK = D_LRU // LRU_BLOCKS
LRU_C = 8.0
N_HEADS = 8
HEAD_DIM = D_MODEL // (2 * N_HEADS)
V_HEAD_DIM = 2 * HEAD_DIM
QK_WIDTH = 2 * N_HEADS * HEAD_DIM
V_WIDTH = N_HEADS * V_HEAD_DIM
ROT_DIM = HEAD_DIM // 4
ROPE_THETA = 500000.0
Q_BLOCK = 128
LN_EPS = 1e-5
SUBLN_EPS = 1e-5
DEEPNORM_ALPHA = (2 * DEPTH) ** 0.25
DEEPNORM_BETA = (8 * DEPTH) ** -0.25

kernel_name = "hybrid_conv_lru_diffattn_step"


def layer_norm(x, g, b):
    xf = x.astype(jnp.float32)
    mu = jnp.mean(xf, axis=-1, keepdims=True)
    var = jnp.mean(jnp.square(xf - mu), axis=-1, keepdims=True)
    return ((xf - mu) * lax.rsqrt(var + LN_EPS) * g.astype(jnp.float32) + b.astype(jnp.float32)).astype(x.dtype)


def adaln(c, w, b):
    m = (jax.nn.silu(c) @ w + b)[:, None, :]
    shift, scale, gate = jnp.split(m, 3, axis=-1)
    return shift, scale, gate


def causal_dwconv(u, buf, w):
    width = w.shape[0]
    t = u.shape[1]
    full = jnp.concatenate([buf.astype(u.dtype), u], axis=1)
    y = full[:, 0:t] * w[0]
    for k in range(1, width):
        y = y + full[:, k:k + t] * w[k]
    return y, full[:, -(width - 1):]


def short_conv_mixer(u, buf, w_in, conv_w, w_out):
    h, bg, cg, z = jnp.split(u @ w_in, 4, axis=-1)
    y, new_buf = causal_dwconv(cg * h, buf, conv_w)
    return (jax.nn.silu(z) * bg * y) @ w_out, new_buf


def _linear_combine(left, right):
    a1, b1 = left
    a2, b2 = right
    return a1 * a2, a2 * b1 + b2


def rglru_mixer(u, h0, buf, w_in, conv_w, conv_b, w_ga, b_ga, w_gx, b_gx, lru_param, w_out):
    bsz, t, _ = u.shape
    xb, z = jnp.split(u @ w_in, 2, axis=-1)
    xc, new_buf = causal_dwconv(xb, buf, conv_w)
    xc = xc + conv_b
    xh = xc.reshape(bsz, t, LRU_BLOCKS, LRU_BLOCK)
    r = jax.nn.sigmoid(jnp.einsum('btnc,ncd->btnd', xh, w_ga).reshape(bsz, t, D_LRU) + b_ga)
    gi = jax.nn.sigmoid(jnp.einsum('btnc,ncd->btnd', xh, w_gx).reshape(bsz, t, D_LRU) + b_gx)
    log_a = LRU_C * r.astype(jnp.float32) * jax.nn.log_sigmoid(lru_param.astype(jnp.float32))
    a = jnp.exp(log_a)
    mult = jnp.sqrt(-jnp.expm1(2.0 * log_a))
    b = mult * (gi * xc).astype(jnp.float32)
    b = b.at[:, 0].add(a[:, 0] * h0.astype(jnp.float32))
    _, h = lax.associative_scan(_linear_combine, (a, b), axis=1)
    y = h.astype(u.dtype) * jax.nn.silu(z)
    return y @ w_out, h[:, -1], new_buf


def rope_partial(x, pos):
    half = ROT_DIM // 2
    inv_freq = jnp.exp(jnp.arange(half, dtype=jnp.float32) * (-2.0 * math.log(ROPE_THETA) / ROT_DIM))
    ang = pos.astype(jnp.float32)[:, None] * inv_freq[None, :]
    cos = jnp.cos(ang)[None, :, None, :]
    sin = jnp.sin(ang)[None, :, None, :]
    x1 = x[..., :half].astype(jnp.float32)
    x2 = x[..., half:ROT_DIM].astype(jnp.float32)
    rot = jnp.concatenate([x1 * cos - x2 * sin, x2 * cos + x1 * sin], axis=-1).astype(x.dtype)
    return jnp.concatenate([rot, x[..., ROT_DIM:]], axis=-1)


def diff_lambda(lq1, lk1, lq2, lk2, lam_init):
    f = jnp.float32
    return (jnp.exp(jnp.sum(lq1.astype(f) * lk1.astype(f))) - jnp.exp(jnp.sum(lq2.astype(f) * lk2.astype(f))) + lam_init)


def diff_attn_project(u, w_in, pos):
    bsz, t, _ = u.shape
    q, k, v, z = jnp.split(u @ w_in, [QK_WIDTH, 2 * QK_WIDTH, 2 * QK_WIDTH + V_WIDTH], axis=-1)
    q = rope_partial(q.reshape(bsz, t, 2 * N_HEADS, HEAD_DIM), pos) * (HEAD_DIM ** -0.5)
    k = rope_partial(k.reshape(bsz, t, 2 * N_HEADS, HEAD_DIM), pos)
    v = v.reshape(bsz, t, N_HEADS, V_HEAD_DIM)
    return q, k, v, z


def diff_combine(p, lam):
    p = p.reshape(p.shape[0], N_HEADS, 2, p.shape[2], p.shape[3])
    return p[:, :, 0] - lam * p[:, :, 1]


def diff_attn_prompt(q, k, v, lam):
    bsz, s = q.shape[:2]
    n_blk = s // Q_BLOCK
    qb = jnp.moveaxis(q.reshape(bsz, n_blk, Q_BLOCK, 2 * N_HEADS, HEAD_DIM), 1, 0)
    kpos = jnp.arange(s)

    def one_block(args):
        qi, blk = args
        sc = jnp.einsum('bqhd,bkhd->bhqk', qi, k).astype(jnp.float32)
        qpos = blk * Q_BLOCK + jnp.arange(Q_BLOCK)
        sc = jnp.where(kpos[None, :] <= qpos[:, None], sc, -jnp.inf)
        w = diff_combine(jax.nn.softmax(sc, axis=-1), lam).astype(v.dtype)
        return jnp.einsum('bhqk,bkhd->bqhd', w, v)

    out = lax.map(one_block, (qb, jnp.arange(n_blk)))
    return jnp.moveaxis(out, 0, 1).reshape(bsz, s, N_HEADS, V_HEAD_DIM)


def diff_attn_sample(q, k_new, v_new, cache_k, cache_v, page_table, lam):
    bsz, t = q.shape[:2]
    k_past = cache_k[page_table].reshape(bsz, -1, 2 * N_HEADS, HEAD_DIM)
    v_past = cache_v[page_table].reshape(bsz, -1, N_HEADS, V_HEAD_DIM)
    n_past = k_past.shape[1]
    s_past = jnp.einsum('bqhd,bkhd->bhqk', q, k_past).astype(jnp.float32)
    s_new = jnp.einsum('bqhd,bkhd->bhqk', q, k_new).astype(jnp.float32)
    causal = jnp.tril(jnp.ones((t, t), dtype=bool))
    s_new = jnp.where(causal, s_new, -jnp.inf)
    p = jax.nn.softmax(jnp.concatenate([s_past, s_new], axis=-1), axis=-1)
    w = diff_combine(p, lam).astype(v_new.dtype)
    return (jnp.einsum('bhqk,bkhd->bqhd', w[..., :n_past], v_past)
            + jnp.einsum('bhqk,bkhd->bqhd', w[..., n_past:], v_new))


def diff_attn_output(o, z, subln_g, lam_init, w_out):
    bsz, t = o.shape[:2]
    of = o.astype(jnp.float32)
    of = of * lax.rsqrt(jnp.mean(jnp.square(of), axis=-1, keepdims=True) + SUBLN_EPS)
    of = of * subln_g.astype(jnp.float32) * (1.0 - lam_init)
    o = of.astype(z.dtype).reshape(bsz, t, V_WIDTH)
    return (o * jax.nn.silu(z)) @ w_out


def setup_inputs(seed: int = 0) -> dict:
    key = jax.random.key(seed)
    ks = iter(jax.random.split(key, 40))
    nrm = lambda shape, s=1.0: jax.random.normal(next(ks), shape, jnp.float32) * s
    n_pages = PAST_LEN // PAGE_SIZE
    n_phys = (5 * DEC_BATCH * n_pages + 3) // 4
    d = D_MODEL
    page_table = jax.random.permutation(next(ks), n_phys)[:DEC_BATCH * n_pages].reshape(DEC_BATCH, n_pages).astype(jnp.int32)
    u = jax.random.uniform(next(ks), (N_LRU_LAYERS, D_LRU), jnp.float32, minval=0.9, maxval=0.999)
    s_lru = u ** (1.0 / LRU_C)
    return {
        'x_prompt': nrm((BATCH, SEQ, d)),
        'x_sample': nrm((DEC_BATCH, DEC_SEQ, d)),
        'state_conv_a': nrm((N_CONV_LAYERS, DEC_BATCH, CONV_A_WIDTH - 1, D_CONV)),
        'state_lru_h': nrm((N_LRU_LAYERS, DEC_BATCH, D_LRU), 0.5),
        'state_lru_conv': nrm((N_LRU_LAYERS, DEC_BATCH, LRU_CONV_WIDTH - 1, D_LRU)),
        'cache_k': nrm((N_ATTN_LAYERS, n_phys, PAGE_SIZE, 2 * N_HEADS, HEAD_DIM)),
        'cache_v': nrm((N_ATTN_LAYERS, n_phys, PAGE_SIZE, N_HEADS, V_HEAD_DIM)),
        'page_table': page_table,
        'c_prompt': nrm((BATCH, d)),
        'c_sample': nrm((DEC_BATCH, d)),
        'w_ada': nrm((DEPTH, d, 3 * d), d ** -0.5),
        'b_ada': nrm((DEPTH, 3 * d), 0.02),
        'ln_g': 1.0 + nrm((DEPTH, d), 0.05),
        'ln_b': nrm((DEPTH, d), 0.02),
        'a_w_in': nrm((N_CONV_LAYERS, d, 4 * D_CONV), d ** -0.5),
        'a_conv_w': nrm((N_CONV_LAYERS, CONV_A_WIDTH, D_CONV), CONV_A_WIDTH ** -0.5),
        'a_w_out': nrm((N_CONV_LAYERS, D_CONV, d), DEEPNORM_BETA * D_CONV ** -0.5),
        'r_w_in': nrm((N_LRU_LAYERS, d, 2 * D_LRU), d ** -0.5),
        'r_conv_w': nrm((N_LRU_LAYERS, LRU_CONV_WIDTH, D_LRU), LRU_CONV_WIDTH ** -0.5),
        'r_conv_b': nrm((N_LRU_LAYERS, D_LRU), 0.02),
        'r_w_ga': nrm((N_LRU_LAYERS, LRU_BLOCKS, LRU_BLOCK, LRU_BLOCK), LRU_BLOCK ** -0.5),
        'r_b_ga': nrm((N_LRU_LAYERS, D_LRU), 0.1),
        'r_w_gx': nrm((N_LRU_LAYERS, LRU_BLOCKS, LRU_BLOCK, LRU_BLOCK), LRU_BLOCK ** -0.5),
        'r_b_gx': nrm((N_LRU_LAYERS, D_LRU), 0.1),
        'r_lru_param': jnp.log(s_lru) - jnp.log1p(-s_lru),
        'r_w_out': nrm((N_LRU_LAYERS, D_LRU, d), DEEPNORM_BETA * D_LRU ** -0.5),
        'd_w_in': nrm((N_ATTN_LAYERS, d, 2 * QK_WIDTH + 2 * V_WIDTH), d ** -0.5),
        'd_lq1': nrm((N_ATTN_LAYERS, HEAD_DIM), 0.1),
        'd_lk1': nrm((N_ATTN_LAYERS, HEAD_DIM), 0.1),
        'd_lq2': nrm((N_ATTN_LAYERS, HEAD_DIM), 0.1),
        'd_lk2': nrm((N_ATTN_LAYERS, HEAD_DIM), 0.1),
        'd_subln_g': 1.0 + nrm((N_ATTN_LAYERS, V_HEAD_DIM), 0.05),
        'd_w_out': nrm((N_ATTN_LAYERS, V_WIDTH, d), DEEPNORM_BETA * V_WIDTH ** -0.5),
    }


def reference(x_prompt, x_sample, state_conv_a, state_lru_h, state_lru_conv, cache_k, cache_v, page_table,
              c_prompt, c_sample, w_ada, b_ada, ln_g, ln_b, a_w_in, a_conv_w, a_w_out,
              r_w_in, r_conv_w, r_conv_b, r_w_ga, r_b_ga, r_w_gx, r_b_gx, r_lru_param, r_w_out,
              d_w_in, d_lq1, d_lk1, d_lq2, d_lk2, d_subln_g, d_w_out):
    xp, xs = x_prompt, x_sample
    bp, tp = xp.shape[:2]
    bs, ts = xs.shape[:2]
    pos_p = jnp.arange(tp)
    pos_s = PAST_LEN + jnp.arange(ts)
    conv_p, conv_s, lruh_p, lruh_s, lruc_p, lruc_s = [], [], [], [], [], []
    k_p, v_p, k_s, v_s = [], [], [], []
    for i in range(DEPTH):
        kind, j = i % N_MIXERS, i // N_MIXERS
        shp, scp, gtp = adaln(c_prompt, w_ada[i], b_ada[i])
        shs, scs, gts = adaln(c_sample, w_ada[i], b_ada[i])
        up = xp * (1.0 + scp) + shp
        us = xs * (1.0 + scs) + shs
        if kind == 0:
            zero_buf = jnp.zeros((bp, CONV_A_WIDTH - 1, D_CONV), xp.dtype)
            op, nbp = short_conv_mixer(up, zero_buf, a_w_in[j], a_conv_w[j], a_w_out[j])
            osm, nbs = short_conv_mixer(us, state_conv_a[j], a_w_in[j], a_conv_w[j], a_w_out[j])
            conv_p.append(nbp)
            conv_s.append(nbs)
        elif kind == 1:
            h0 = jnp.zeros((bp, D_LRU), jnp.float32)
            zero_buf = jnp.zeros((bp, LRU_CONV_WIDTH - 1, D_LRU), xp.dtype)
            op, hp, nbp = rglru_mixer(up, h0, zero_buf, r_w_in[j], r_conv_w[j], r_conv_b[j], r_w_ga[j], r_b_ga[j],
                                      r_w_gx[j], r_b_gx[j], r_lru_param[j], r_w_out[j])
            osm, hs, nbs = rglru_mixer(us, state_lru_h[j], state_lru_conv[j], r_w_in[j], r_conv_w[j], r_conv_b[j],
                                       r_w_ga[j], r_b_ga[j], r_w_gx[j], r_b_gx[j], r_lru_param[j], r_w_out[j])
            lruh_p.append(hp)
            lruh_s.append(hs)
            lruc_p.append(nbp)
            lruc_s.append(nbs)
        else:
            lam_init = 0.8 - 0.6 * math.exp(-0.3 * i)
            lam = diff_lambda(d_lq1[j], d_lk1[j], d_lq2[j], d_lk2[j], lam_init)
            qp, kp, vp, zp = diff_attn_project(up, d_w_in[j], pos_p)
            op = diff_attn_output(diff_attn_prompt(qp, kp, vp, lam), zp, d_subln_g[j], lam_init, d_w_out[j])
            qs, ksn, vsn, zs = diff_attn_project(us, d_w_in[j], pos_s)
            o_heads = diff_attn_sample(qs, ksn, vsn, cache_k[j], cache_v[j], page_table, lam)
            osm = diff_attn_output(o_heads, zs, d_subln_g[j], lam_init, d_w_out[j])
            k_p.append(kp)
            v_p.append(vp)
            k_s.append(ksn)
            v_s.append(vsn)
        xp = layer_norm(DEEPNORM_ALPHA * xp + gtp * op, ln_g[i], ln_b[i])
        xs = layer_norm(DEEPNORM_ALPHA * xs + gts * osm, ln_g[i], ln_b[i])
    return (xp, xs,
            jnp.stack(conv_p), jnp.stack(conv_s),
            jnp.stack(lruh_p), jnp.stack(lruh_s),
            jnp.stack(lruc_p), jnp.stack(lruc_s),
            jnp.stack(k_p), jnp.stack(v_p), jnp.stack(k_s), jnp.stack(v_s))
```

```python
import functools
import math

import jax
import jax.numpy as jnp
from jax import lax
from jax.experimental import pallas as pl
from jax.experimental.pallas import tpu as pltpu

F32 = jnp.float32
BF16 = jnp.bfloat16

LN_EPS = 1e-5
SUBLN_EPS = 1e-5
LRU_C = 8.0
ROPE_THETA = 500000.0
N_MIXERS = 3

SUBLANES = 8
LANES = 128
VMEM_LIMIT_BYTES = 56 * 1024 * 1024


def _cparams(semantics):
    return pltpu.CompilerParams(dimension_semantics=semantics, vmem_limit_bytes=VMEM_LIMIT_BYTES)


def _dot(a, b):
    return jnp.dot(a, b, preferred_element_type=F32)


def _dot_nt(a, b):
    return lax.dot_general(a, b, (((1,), (1,)), ((), ())), preferred_element_type=F32)


def _silu(z):
    return z * jax.nn.sigmoid(z)


def _layer_norm(y, g, b):
    mu = jnp.mean(y, axis=-1, keepdims=True)
    yc = y - mu
    var = jnp.mean(yc * yc, axis=-1, keepdims=True)
    return yc * lax.rsqrt(var + LN_EPS) * g + b


def _round_up(n, m):
    return (n + m - 1) // m * m


def _ada_body(c_ref, w_ref, b_ref, o_ref):
    c = c_ref[...]
    a = _silu(c).astype(BF16)
    o_ref[...] = _dot(a, w_ref[...].astype(BF16)) + b_ref[...]


def _ada_call(c_all, w_ada, b_ada):
    depth, d, d3 = w_ada.shape
    rows = c_all.shape[0]
    nt = d3 // d
    return pl.pallas_call(
        _ada_body,
        grid=(depth, nt),
        in_specs=[
            pl.BlockSpec((rows, d), lambda i, n: (0, 0)),
            pl.BlockSpec((None, d, d), lambda i, n: (i, 0, n)),
            pl.BlockSpec((None, 1, d), lambda i, n: (i, 0, n)),
        ],
        out_specs=pl.BlockSpec((None, rows, d), lambda i, n: (i, 0, n)),
        out_shape=jax.ShapeDtypeStruct((depth, rows, d3), F32),
        compiler_params=_cparams(("arbitrary", "arbitrary")),
        name="adaln",
    )(c_all, w_ada, b_ada.reshape(depth, 1, d3))


def _load_history(t, buf, st0_ref, pad, ks, tm):
    @pl.when(t == 0)
    def _():
        buf[pad - ks:pad, :] = st0_ref[...]

    @pl.when(t > 0)
    def _():
        buf[pad - ks:pad, :] = buf[pad + tm - ks:pad + tm, :]


def _conv_taps(buf, cw_ref, cols, width, s, pad, tm):
    y = None
    for k in range(width):
        r0 = pad - (width - 1 - k) * s
        term = cw_ref[k:k + 1, cols] * buf[r0:r0 + tm, cols]
        y = term if y is None else y + term
    return y


def _conv_body(x_ref, sh_ref, sc_ref, gt_ref, st0_ref, win_ref, cw_ref, wout_ref, g_ref, b_ref,
               y_ref, st_ref, abuf, acc, *, s, tm, cw, width, pad, alpha):
    d = x_ref.shape[-1]
    ks = (width - 1) * s
    t = pl.program_id(1)
    _load_history(t, abuf, st0_ref, pad, ks, tm)
    x = x_ref[...]
    u = (x * (1.0 + sc_ref[...]) + sh_ref[...]).astype(BF16)
    for c in range(d // cw):
        cols = slice(c * cw, (c + 1) * cw)
        h = _dot(u, win_ref[:, c * cw:(c + 1) * cw])
        cg = _dot(u, win_ref[:, 2 * d + c * cw:2 * d + (c + 1) * cw])
        abuf[pad:pad + tm, cols] = cg * h
        y = _conv_taps(abuf, cw_ref, cols, width, s, pad, tm)
        bg = _dot(u, win_ref[:, d + c * cw:d + (c + 1) * cw])
        z = _dot(u, win_ref[:, 3 * d + c * cw:3 * d + (c + 1) * cw])
        gated = (_silu(z) * bg * y).astype(BF16)
        part = _dot(gated, wout_ref[cols, :])
        if c == 0:
            acc[...] = part
        else:
            acc[...] += part
    st_ref[...] = abuf[pad + tm - ks:pad + tm, :]
    res = alpha * x + gt_ref[...] * acc[...]
    y_ref[...] = _layer_norm(res, g_ref[...], b_ref[...])


def _conv_layer(x, sh, sc, gt, st0, w_in, conv_w, w_out, ln_g, ln_b, *, s, tm, alpha):
    bsz, t_rows, d = x.shape
    width = conv_w.shape[0]
    ks = (width - 1) * s
    pad = _round_up(ks, SUBLANES)
    r = sh.shape[1]
    cw = 256
    body = functools.partial(_conv_body, s=s, tm=tm, cw=cw, width=width, pad=pad, alpha=alpha)
    row_spec = pl.BlockSpec((None, tm, d), lambda b, t: (b, t, 0))
    mod_spec = pl.BlockSpec((None, r, d), lambda b, t: (b, 0, 0))
    st_spec = pl.BlockSpec((None, ks, d), lambda b, t: (b, 0, 0))
    const = lambda shape: pl.BlockSpec(shape, lambda b, t: (0,) * len(shape))
    return pl.pallas_call(
        body,
        grid=(bsz, t_rows // tm),
        in_specs=[row_spec, mod_spec, mod_spec, mod_spec, st_spec,
                  const(w_in.shape), const(conv_w.shape), const(w_out.shape),
                  const((1, d)), const((1, d))],
        out_specs=[row_spec, st_spec],
        out_shape=[jax.ShapeDtypeStruct((bsz, t_rows, d), F32),
                   jax.ShapeDtypeStruct((bsz, ks, d), F32)],
        scratch_shapes=[pltpu.VMEM((pad + tm, d), F32), pltpu.VMEM((tm, d), F32)],
        compiler_params=_cparams(("arbitrary", "arbitrary")),
        name="conv_layer",
    )(x, sh, sc, gt, st0, w_in, conv_w, w_out, ln_g.reshape(1, d), ln_b.reshape(1, d))


def _lru_body(x_ref, sh_ref, sc_ref, gt_ref, st0_ref, h0_ref, win_ref, cw_ref, cb_ref,
              wga_ref, bga_ref, wgx_ref, bgx_ref, prm_ref, wout_ref, g_ref, b_ref,
              y_ref, st_ref, hl_ref, xbuf, a_s, b_s, h_s, hc, *, s, tm, width, pad, alpha):
    d = x_ref.shape[-1]
    nblk, blk, _ = wga_ref.shape
    ks = (width - 1) * s
    t = pl.program_id(1)
    _load_history(t, xbuf, st0_ref, pad, ks, tm)

    @pl.when(t == 0)
    def _():
        hc[...] = h0_ref[...]

    x = x_ref[...]
    u = (x * (1.0 + sc_ref[...]) + sh_ref[...]).astype(BF16)
    xbuf[pad:pad + tm, :] = _dot(u, win_ref[:, 0:d])
    st_ref[...] = xbuf[pad + tm - ks:pad + tm, :]
    for n in range(nblk):
        cols = slice(n * blk, (n + 1) * blk)
        xc = _conv_taps(xbuf, cw_ref, cols, width, s, pad, tm) + cb_ref[:, cols]
        xcb = xc.astype(BF16)
        r = jax.nn.sigmoid(_dot(xcb, wga_ref[n]) + bga_ref[:, cols])
        gi = jax.nn.sigmoid(_dot(xcb, wgx_ref[n]) + bgx_ref[:, cols])
        log_a = LRU_C * r * jax.nn.log_sigmoid(prm_ref[:, cols])
        a = jnp.exp(log_a)
        a_s[:, cols] = a
        b_s[:, cols] = jnp.sqrt(-jnp.tanh(log_a) * (a * a + 1.0)) * (gi * xc)

    steps = tm // s

    def step(i, h):
        r0 = pl.multiple_of(i * s, s)
        h = a_s[pl.ds(r0, s), :] * h + b_s[pl.ds(r0, s), :]
        h_s[pl.ds(r0, s), :] = h
        return h

    h_last = lax.fori_loop(0, steps, step, hc[...], unroll=min(steps, 8))
    hc[...] = h_last
    hl_ref[...] = h_last

    z = _dot(u, win_ref[:, d:2 * d])
    yy = (h_s[...] * _silu(z)).astype(BF16)
    out = _dot(yy, wout_ref[...])
    res = alpha * x + gt_ref[...] * out
    y_ref[...] = _layer_norm(res, g_ref[...], b_ref[...])


def _lru_layer(x, sh, sc, gt, st0, h0, w_in, conv_w, conv_b, w_ga, b_ga, w_gx, b_gx, prm, w_out,
               ln_g, ln_b, *, s, tm, alpha):
    bsz, t_rows, d = x.shape
    width = conv_w.shape[0]
    ks = (width - 1) * s
    pad = _round_up(ks, SUBLANES)
    r = sh.shape[1]
    body = functools.partial(_lru_body, s=s, tm=tm, width=width, pad=pad, alpha=alpha)
    row_spec = pl.BlockSpec((None, tm, d), lambda b, t: (b, t, 0))
    mod_spec = pl.BlockSpec((None, r, d), lambda b, t: (b, 0, 0))
    st_spec = pl.BlockSpec((None, ks, d), lambda b, t: (b, 0, 0))
    h_spec = pl.BlockSpec((None, s, d), lambda b, t: (b, 0, 0))
    const = lambda shape: pl.BlockSpec(shape, lambda b, t: (0,) * len(shape))
    vec = const((1, d))
    return pl.pallas_call(
        body,
        grid=(bsz, t_rows // tm),
        in_specs=[row_spec, mod_spec, mod_spec, mod_spec, st_spec, h_spec,
                  const(w_in.shape), const(conv_w.shape), vec,
                  const(w_ga.shape), vec, const(w_gx.shape), vec, vec,
                  const(w_out.shape), vec, vec],
        out_specs=[row_spec, st_spec, h_spec],
        out_shape=[jax.ShapeDtypeStruct((bsz, t_rows, d), F32),
                   jax.ShapeDtypeStruct((bsz, ks, d), F32),
                   jax.ShapeDtypeStruct((bsz, s, d), F32)],
        scratch_shapes=[pltpu.VMEM((pad + tm, d), F32), pltpu.VMEM((tm, d), F32),
                        pltpu.VMEM((tm, d), F32), pltpu.VMEM((tm, d), F32),
                        pltpu.VMEM((s, d), F32)],
        compiler_params=_cparams(("arbitrary", "arbitrary")),
        name="lru_layer",
    )(x, sh, sc, gt, st0, h0, w_in, conv_w, conv_b.reshape(1, d), w_ga, b_ga.reshape(1, d),
      w_gx, b_gx.reshape(1, d), prm.reshape(1, d), w_out, ln_g.reshape(1, d), ln_b.reshape(1, d))


def _rope_block(xb, cos, sin_lo, sin_hi, half):
    return (xb * cos + pltpu.roll(xb, LANES - half, axis=1) * sin_lo
            + pltpu.roll(xb, half, axis=1) * sin_hi)


def _qkv_body(x_ref, sh_ref, sc_ref, cos_ref, slo_ref, shi_ref, win_ref,
              k_ref, v_ref, z_ref, qb_ref, kb_ref, vb_ref, *, half, qscale):
    d = x_ref.shape[-1]
    x = x_ref[...]
    u = (x * (1.0 + sc_ref[...]) + sh_ref[...]).astype(BF16)
    cos, slo, shi = cos_ref[...], slo_ref[...], shi_ref[...]
    q = _dot(u, win_ref[:, 0:d])
    for j in range(d // LANES):
        cols = slice(j * LANES, (j + 1) * LANES)
        qb_ref[:, cols] = (_rope_block(q[:, cols], cos, slo, shi, half) * qscale).astype(BF16)
    k = _dot(u, win_ref[:, d:2 * d])
    for j in range(d // LANES):
        cols = slice(j * LANES, (j + 1) * LANES)
        kr = _rope_block(k[:, cols], cos, slo, shi, half)
        k_ref[:, cols] = kr
        kb_ref[:, cols] = kr.astype(BF16)
    v = _dot(u, win_ref[:, 2 * d:3 * d])
    v_ref[...] = v
    vb_ref[...] = v.astype(BF16)
    z_ref[...] = _dot(u, win_ref[:, 3 * d:4 * d])


def _qkv_call(x, sh, sc, cos_t, slo_t, shi_t, w_in, *, tm, half, qscale):
    bsz, t_rows, d = x.shape
    r = sh.shape[1]
    body = functools.partial(_qkv_body, half=half, qscale=qscale)
    row_spec = pl.BlockSpec((None, tm, d), lambda b, t: (b, t, 0))
    mod_spec = pl.BlockSpec((None, r, d), lambda b, t: (b, 0, 0))
    tab_spec = pl.BlockSpec((tm, LANES), lambda b, t: (t, 0))
    f32_out = jax.ShapeDtypeStruct((bsz, t_rows, d), F32)
    bf_out = jax.ShapeDtypeStruct((bsz, t_rows, d), BF16)
    return pl.pallas_call(
        body,
        grid=(bsz, t_rows // tm),
        in_specs=[row_spec, mod_spec, mod_spec, tab_spec, tab_spec, tab_spec,
                  pl.BlockSpec(w_in.shape, lambda b, t: (0, 0))],
        out_specs=[row_spec] * 6,
        out_shape=[f32_out, f32_out, f32_out, bf_out, bf_out, bf_out],
        compiler_params=_cparams(("arbitrary", "arbitrary")),
        name="attn_qkv",
    )(x, sh, sc, cos_t, slo_t, shi_t, w_in)


def _rope_tables(pos, head_dim, rot_dim):
    half = rot_dim // 2
    inv_freq = jnp.exp(jnp.arange(half, dtype=F32) * (-2.0 * math.log(ROPE_THETA) / rot_dim))
    ang = pos.astype(F32)[:, None] * inv_freq[None, :]
    cos, sin = jnp.cos(ang), jnp.sin(ang)
    n = pos.shape[0]
    ones = jnp.ones((n, head_dim - rot_dim), F32)
    zeros = jnp.zeros((n, head_dim - rot_dim), F32)
    zh = jnp.zeros((n, half), F32)
    cos_h = jnp.concatenate([cos, cos, ones], axis=1)
    slo_h = jnp.concatenate([-sin, zh, zeros], axis=1)
    shi_h = jnp.concatenate([zh, sin, zeros], axis=1)
    rep = LANES // head_dim
    return (jnp.tile(cos_h, (1, rep)), jnp.tile(slo_h, (1, rep)), jnp.tile(shi_h, (1, rep)))


def _diff_lambda(lq1_ref, lk1_ref, lq2_ref, lk2_ref, lam_init):
    s1 = jnp.sum(lq1_ref[...] * lk1_ref[...], axis=-1, keepdims=True)
    s2 = jnp.sum(lq2_ref[...] * lk2_ref[...], axis=-1, keepdims=True)
    return jnp.exp(s1) - jnp.exp(s2) + lam_init


def _flash_body(q_ref, k_ref, v_ref, lq1_ref, lk1_ref, lq2_ref, lk2_ref, o_ref,
                m_s, l_s, acc_s, *, tq, head_dim, lam_init):
    qi = pl.program_id(2)
    q = q_ref[...]
    lane = lax.broadcasted_iota(jnp.int32, q.shape, 1)
    zero = jnp.zeros_like(q)
    qq = jnp.concatenate([jnp.where(lane < head_dim, q, zero),
                          jnp.where(lane >= head_dim, q, zero)], axis=0)
    m_s[...] = jnp.full(m_s.shape, -jnp.inf, F32)
    l_s[...] = jnp.zeros(l_s.shape, F32)
    acc_s[...] = jnp.zeros(acc_s.shape, F32)

    def kv_block(j, masked):
        r0 = pl.multiple_of(j * tq, tq)
        k = k_ref[pl.ds(r0, tq), :]
        v = v_ref[pl.ds(r0, tq), :]
        sc = _dot_nt(qq, k)
        if masked:
            row = lax.broadcasted_iota(jnp.int32, sc.shape, 0)
            row = jnp.where(row >= tq, row - tq, row)
            col = lax.broadcasted_iota(jnp.int32, sc.shape, 1)
            sc = jnp.where(col <= row, sc, -jnp.inf)
        m_prev = m_s[...]
        m_new = jnp.maximum(m_prev, jnp.max(sc, axis=1, keepdims=True))
        alpha = jnp.exp(m_prev - m_new)
        p = jnp.exp(sc - m_new)
        l_s[...] = alpha * l_s[...] + jnp.sum(p, axis=1, keepdims=True)
        acc_s[...] = alpha * acc_s[...] + _dot(p.astype(BF16), v)
        m_s[...] = m_new

    def loop_body(j, carry):
        kv_block(j, False)
        return carry

    lax.fori_loop(0, qi, loop_body, 0)
    kv_block(qi, True)

    o = acc_s[...] / l_s[...]
    lam = _diff_lambda(lq1_ref, lk1_ref, lq2_ref, lk2_ref, lam_init)
    o_ref[...] = o[0:tq, :] - lam * o[tq:2 * tq, :]


def _flash_call(qb, kb, vb, lq1, lk1, lq2, lk2, *, tq, head_dim, lam_init):
    bsz, t_rows, d = qb.shape
    n_heads = d // LANES
    body = functools.partial(_flash_body, tq=tq, head_dim=head_dim, lam_init=lam_init)
    q_spec = pl.BlockSpec((None, tq, LANES), lambda b, h, i: (b, i, h))
    kv_spec = pl.BlockSpec((None, t_rows, LANES), lambda b, h, i: (b, 0, h))
    l_spec = pl.BlockSpec((1, head_dim), lambda b, h, i: (0, 0))
    return pl.pallas_call(
        body,
        grid=(bsz, n_heads, t_rows // tq),
        in_specs=[q_spec, kv_spec, kv_spec, l_spec, l_spec, l_spec, l_spec],
        out_specs=q_spec,
        out_shape=jax.ShapeDtypeStruct((bsz, t_rows, d), F32),
        scratch_shapes=[pltpu.VMEM((2 * tq, 1), F32), pltpu.VMEM((2 * tq, 1), F32),
                        pltpu.VMEM((2 * tq, LANES), F32)],
        compiler_params=_cparams(("arbitrary", "arbitrary", "arbitrary")),
        name="attn_flash",
    )(qb, kb, vb, lq1, lk1, lq2, lk2)


def _decode_body(pt_ref, qh_ref, kc_ref, vc_ref, kn_ref, vn_ref, lq1_ref, lk1_ref, lq2_ref,
                 lk2_ref, o_ref, m_s, l_s, acc_s, kpad, vpad, *, n_pages, t_new, lam_init):
    b = pl.program_id(0)
    p = pl.program_id(1)
    n_sub = qh_ref.shape[0]
    n_heads = n_sub // 2
    rows = qh_ref.shape[1]
    grp = 2 * t_new

    @pl.when((b == 0) & (p == 0))
    def _():
        kpad[...] = jnp.zeros(kpad.shape, F32)
        vpad[...] = jnp.zeros(vpad.shape, F32)

    @pl.when(p == 0)
    def _():
        m_s[...] = jnp.full(m_s.shape, -jnp.inf, F32)
        l_s[...] = jnp.zeros(l_s.shape, F32)
        acc_s[...] = jnp.zeros(acc_s.shape, F32)

    def page(k_ref, v_ref, masked):
        sc = None
        for h in range(n_sub):
            kh = k_ref[:, h, :].astype(BF16)
            part = _dot_nt(qh_ref[h], kh)
            sc = part if sc is None else sc + part
        if masked:
            row = lax.broadcasted_iota(jnp.int32, sc.shape, 0)
            col = lax.broadcasted_iota(jnp.int32, sc.shape, 1)
            sc = jnp.where(col <= row % t_new, sc, -jnp.inf)
        m_prev = m_s[...]
        m_new = jnp.maximum(m_prev, jnp.max(sc, axis=1, keepdims=True))
        alpha = jnp.exp(m_prev - m_new)
        pe = jnp.exp(sc - m_new)
        l_s[...] = alpha * l_s[...] + jnp.sum(pe, axis=1, keepdims=True)
        m_s[...] = m_new
        pb = pe.astype(BF16)
        for hv in range(n_heads):
            rs = slice(hv * grp, (hv + 1) * grp)
            vh = v_ref[:, hv, :].astype(BF16)
            pv = _dot(pb, vh)
            acc_s[rs, :] = alpha[rs, :] * acc_s[rs, :] + pv[rs, :]

    page(kc_ref, vc_ref, False)

    @pl.when(p == n_pages - 1)
    def _():
        kpad[0:t_new] = kn_ref[...]
        vpad[0:t_new] = vn_ref[...]
        page(kpad, vpad, True)
        o = acc_s[...] / l_s[...]
        lam = _diff_lambda(lq1_ref, lk1_ref, lq2_ref, lk2_ref, lam_init)
        vd = o.shape[1]
        for hv in range(n_heads):
            o1 = o[hv * grp:hv * grp + t_new, :]
            o2 = o[hv * grp + t_new:(hv + 1) * grp, :]
            o_ref[:, hv * vd:(hv + 1) * vd] = o1 - lam * o2


def _decode_call(page_table, qh, cache_k, cache_v, layer, k_new, v_new, lq1, lk1, lq2, lk2, *,
                 lam_init):
    bsz, n_pages = page_table.shape
    _, _, page, n_sub, head_dim = cache_k.shape
    _, _, _, n_heads, vd = cache_v.shape
    t_new = k_new.shape[1]
    rows = n_sub * t_new
    body = functools.partial(_decode_body, n_pages=n_pages, t_new=t_new, lam_init=lam_init)
    l_spec = pl.BlockSpec((1, head_dim), lambda b, p, pt: (0, 0))
    grid_spec = pltpu.PrefetchScalarGridSpec(
        num_scalar_prefetch=1,
        grid=(bsz, n_pages),
        in_specs=[
            pl.BlockSpec((None, n_sub, rows, head_dim), lambda b, p, pt: (b, 0, 0, 0)),
            pl.BlockSpec((None, None, page, n_sub, head_dim),
                         lambda b, p, pt: (layer, pt[b * n_pages + p], 0, 0, 0)),
            pl.BlockSpec((None, None, page, n_heads, vd),
                         lambda b, p, pt: (layer, pt[b * n_pages + p], 0, 0, 0)),
            pl.BlockSpec((None, t_new, n_sub, head_dim), lambda b, p, pt: (b, 0, 0, 0)),
            pl.BlockSpec((None, t_new, n_heads, vd), lambda b, p, pt: (b, 0, 0, 0)),
            l_spec, l_spec, l_spec, l_spec,
        ],
        out_specs=pl.BlockSpec((None, t_new, n_heads * vd), lambda b, p, pt: (b, 0, 0)),
        scratch_shapes=[pltpu.VMEM((rows, 1), F32), pltpu.VMEM((rows, 1), F32),
                        pltpu.VMEM((rows, vd), F32),
                        pltpu.VMEM((page, n_sub, head_dim), F32),
                        pltpu.VMEM((page, n_heads, vd), F32)],
    )
    return pl.pallas_call(
        body,
        grid_spec=grid_spec,
        out_shape=jax.ShapeDtypeStruct((bsz, t_new, n_heads * vd), F32),
        compiler_params=_cparams(("arbitrary", "arbitrary")),
        name="attn_decode",
    )(page_table.reshape(-1), qh, cache_k, cache_v, k_new, v_new, lq1, lk1, lq2, lk2)


def _attn_out_body(x_ref, gt_ref, o_ref, z_ref, sg_ref, wout_ref, g_ref, b_ref, y_ref, gbuf, *,
                   alpha, out_scale):
    d = x_ref.shape[-1]
    vd = sg_ref.shape[-1]
    for hv in range(d // vd):
        cols = slice(hv * vd, (hv + 1) * vd)
        of = o_ref[:, cols]
        of = of * lax.rsqrt(jnp.mean(of * of, axis=-1, keepdims=True) + SUBLN_EPS)
        of = of * sg_ref[...] * out_scale
        gbuf[:, cols] = (of * _silu(z_ref[:, cols])).astype(BF16)
    out = _dot(gbuf[...], wout_ref[...])
    res = alpha * x_ref[...] + gt_ref[...] * out
    y_ref[...] = _layer_norm(res, g_ref[...], b_ref[...])


def _attn_out_call(x, gt, o, z, subln_g, w_out, ln_g, ln_b, *, tm, alpha, out_scale):
    bsz, t_rows, d = x.shape
    r = gt.shape[1]
    vd = subln_g.shape[-1]
    body = functools.partial(_attn_out_body, alpha=alpha, out_scale=out_scale)
    row_spec = pl.BlockSpec((None, tm, d), lambda b, t: (b, t, 0))
    mod_spec = pl.BlockSpec((None, r, d), lambda b, t: (b, 0, 0))
    const = lambda shape: pl.BlockSpec(shape, lambda b, t: (0,) * len(shape))
    return pl.pallas_call(
        body,
        grid=(bsz, t_rows // tm),
        in_specs=[row_spec, mod_spec, row_spec, row_spec, const((1, vd)), const(w_out.shape),
                  const((1, d)), const((1, d))],
        out_specs=row_spec,
        out_shape=jax.ShapeDtypeStruct((bsz, t_rows, d), F32),
        scratch_shapes=[pltpu.VMEM((tm, d), BF16)],
        compiler_params=_cparams(("arbitrary", "arbitrary")),
        name="attn_out",
    )(x, gt, o, z, subln_g.reshape(1, vd), w_out, ln_g.reshape(1, d), ln_b.reshape(1, d))


def kernel(x_prompt, x_sample, state_conv_a, state_lru_h, state_lru_conv, cache_k, cache_v, page_table, c_prompt, c_sample, w_ada, b_ada, ln_g, ln_b, a_w_in, a_conv_w, a_w_out, r_w_in, r_conv_w, r_conv_b, r_w_ga, r_b_ga, r_w_gx, r_b_gx, r_lru_param, r_w_out, d_w_in, d_lq1, d_lk1, d_lq2, d_lk2, d_subln_g, d_w_out):
    bp, tp, d = x_prompt.shape
    bs, ts, _ = x_sample.shape
    depth = w_ada.shape[0]
    n_pages = page_table.shape[1]
    page = cache_k.shape[2]
    n_sub, head_dim = cache_k.shape[3], cache_k.shape[4]
    n_heads, vd = cache_v.shape[3], cache_v.shape[4]
    past_len = n_pages * page
    rot_dim = head_dim // 4
    alpha = (2 * depth) ** 0.25
    rows_s = bs * ts

    n_c = _round_up(bp + bs, SUBLANES)
    c_all = jnp.concatenate([c_prompt, c_sample, jnp.zeros((n_c - bp - bs, d), F32)], axis=0)
    ada = _ada_call(c_all, w_ada, b_ada)

    def mods(i):
        m = ada[i]
        mp = [m[:bp, k * d:(k + 1) * d].reshape(bp, 1, d) for k in range(3)]
        ms = [jnp.tile(m[bp:bp + bs, k * d:(k + 1) * d], (ts, 1)).reshape(1, rows_s, d)
              for k in range(3)]
        return mp, ms

    def to_time_major(a):
        return jnp.swapaxes(a, 0, 1).reshape((1, a.shape[1] * bs) + a.shape[2:])

    def from_time_major(a, n):
        return jnp.swapaxes(a.reshape(n, bs, a.shape[-1]), 0, 1)

    xp = x_prompt
    xs = to_time_major(x_sample)
    bf = lambda w: w.astype(BF16)

    conv_p, conv_s, lruh_p, lruh_s, lruc_p, lruc_s = [], [], [], [], [], []
    k_p, v_p, k_s, v_s = [], [], [], []
    for i in range(depth):
        kind, j = i % N_MIXERS, i // N_MIXERS
        (shp, scp, gtp), (shs, scs, gts) = mods(i)
        if kind == 0:
            w_in, w_out = bf(a_w_in[j]), bf(a_w_out[j])
            width = a_conv_w.shape[1]
            args = (w_in, a_conv_w[j], w_out, ln_g[i], ln_b[i])
            xp, nbp = _conv_layer(xp, shp, scp, gtp, jnp.zeros((bp, width - 1, d), F32), *args,
                                  s=1, tm=512, alpha=alpha)
            xs, nbs = _conv_layer(xs, shs, scs, gts, to_time_major(state_conv_a[j]), *args,
                                  s=bs, tm=rows_s, alpha=alpha)
            conv_p.append(nbp)
            conv_s.append(from_time_major(nbs, width - 1))
        elif kind == 1:
            width = r_conv_w.shape[1]
            args = (bf(r_w_in[j]), r_conv_w[j], r_conv_b[j], bf(r_w_ga[j]), r_b_ga[j],
                    bf(r_w_gx[j]), r_b_gx[j], r_lru_param[j], bf(r_w_out[j]), ln_g[i], ln_b[i])
            xp, nbp, hp = _lru_layer(xp, shp, scp, gtp, jnp.zeros((bp, width - 1, d), F32),
                                     jnp.zeros((bp, 1, d), F32), *args, s=1, tm=256, alpha=alpha)
            xs, nbs, hs = _lru_layer(xs, shs, scs, gts, to_time_major(state_lru_conv[j]),
                                     state_lru_h[j].reshape(1, bs, d), *args,
                                     s=bs, tm=rows_s, alpha=alpha)
            lruh_p.append(hp.reshape(bp, d))
            lruh_s.append(hs.reshape(bs, d))
            lruc_p.append(nbp)
            lruc_s.append(from_time_major(nbs, width - 1))
        else:
            lam_init = 0.8 - 0.6 * math.exp(-0.3 * i)
            w_in, w_out = bf(d_w_in[j]), bf(d_w_out[j])
            half = rot_dim // 2
            qscale = head_dim ** -0.5
            lvec = [v[j].reshape(1, head_dim) for v in (d_lq1, d_lk1, d_lq2, d_lk2)]
            tabs_p = _rope_tables(jnp.arange(tp), head_dim, rot_dim)
            kp, vp, zp, qb, kb, vb = _qkv_call(xp, shp, scp, *tabs_p, w_in, tm=256, half=half,
                                               qscale=qscale)
            op = _flash_call(qb, kb, vb, *lvec, tq=256, head_dim=head_dim, lam_init=lam_init)
            xp = _attn_out_call(xp, gtp, op, zp, d_subln_g[j], w_out, ln_g[i], ln_b[i], tm=512,
                                alpha=alpha, out_scale=1.0 - lam_init)
            k_p.append(kp.reshape(bp, tp, n_sub, head_dim))
            v_p.append(vp.reshape(bp, tp, n_heads, vd))
            pos_s = past_len + jnp.repeat(jnp.arange(ts), bs)
            tabs_s = _rope_tables(pos_s, head_dim, rot_dim)
            ks_, vs_, zs, qsb, _, _ = _qkv_call(xs, shs, scs, *tabs_s, w_in, tm=rows_s, half=half,
                                                qscale=qscale)
            ksn = from_time_major(ks_, ts).reshape(bs, ts, n_sub, head_dim)
            vsn = from_time_major(vs_, ts).reshape(bs, ts, n_heads, vd)
            q4 = from_time_major(qsb, ts).reshape(bs, ts, n_sub, head_dim)
            q4 = jnp.transpose(q4, (0, 2, 1, 3))
            eye = jnp.eye(n_sub, dtype=BF16)
            qh = (eye[None, :, :, None, None] * q4[:, :, None, :, :]).reshape(
                bs, n_sub, n_sub * ts, head_dim)
            os_ = _decode_call(page_table, qh, cache_k, cache_v, j, ksn, vsn, *lvec,
                               lam_init=lam_init)
            xs = _attn_out_call(xs, gts, to_time_major(os_), zs, d_subln_g[j], w_out, ln_g[i],
                                ln_b[i], tm=rows_s, alpha=alpha, out_scale=1.0 - lam_init)
            k_s.append(ksn)
            v_s.append(vsn)

    return (xp, from_time_major(xs, ts),
            jnp.stack(conv_p), jnp.stack(conv_s),
            jnp.stack(lruh_p), jnp.stack(lruh_s),
            jnp.stack(lruc_p), jnp.stack(lruc_s),
            jnp.stack(k_p), jnp.stack(v_p), jnp.stack(k_s), jnp.stack(v_s))
```

```python
import functools
import math

import jax
import jax.numpy as jnp
from jax import lax
from jax.experimental import pallas as pl
from jax.experimental.pallas import tpu as pltpu

F32 = jnp.float32
BF16 = jnp.bfloat16

LN_EPS = 1e-5
SUBLN_EPS = 1e-5
LRU_C = 8.0
ROPE_THETA = 500000.0
N_MIXERS = 3

SUBLANES = 8
LANES = 128
VMEM_LIMIT_BYTES = 56 * 1024 * 1024


def _cparams(semantics):
    return pltpu.CompilerParams(dimension_semantics=semantics, vmem_limit_bytes=VMEM_LIMIT_BYTES)


def _dot(a, b):
    return jnp.dot(a, b, preferred_element_type=F32)


def _dot_nt(a, b):
    return lax.dot_general(a, b, (((1,), (1,)), ((), ())), preferred_element_type=F32)


def _silu(z):
    return z * jax.nn.sigmoid(z)


def _layer_norm(y, g, b):
    mu = jnp.mean(y, axis=-1, keepdims=True)
    yc = y - mu
    var = jnp.mean(yc * yc, axis=-1, keepdims=True)
    return yc * lax.rsqrt(var + LN_EPS) * g + b


def _round_up(n, m):
    return (n + m - 1) // m * m


def _ada_body(c_ref, w_ref, b_ref, o_ref):
    c = c_ref[...]
    a = _silu(c).astype(BF16)
    o_ref[...] = _dot(a, w_ref[...].astype(BF16)) + b_ref[...]


def _ada_call(c_all, w_ada, b_ada):
    depth, d, d3 = w_ada.shape
    rows = c_all.shape[0]
    nt = d3 // d
    return pl.pallas_call(
        _ada_body,
        grid=(depth, nt),
        in_specs=[
            pl.BlockSpec((rows, d), lambda i, n: (0, 0)),
            pl.BlockSpec((None, d, d), lambda i, n: (i, 0, n)),
            pl.BlockSpec((None, 1, d), lambda i, n: (i, 0, n)),
        ],
        out_specs=pl.BlockSpec((None, rows, d), lambda i, n: (i, 0, n)),
        out_shape=jax.ShapeDtypeStruct((depth, rows, d3), F32),
        compiler_params=_cparams(("arbitrary", "arbitrary")),
        name="adaln",
    )(c_all, w_ada, b_ada.reshape(depth, 1, d3))


def _load_history(t, buf, st0_ref, pad, ks, tm):
    @pl.when(t == 0)
    def _():
        buf[pad - ks:pad, :] = st0_ref[...]

    @pl.when(t > 0)
    def _():
        buf[pad - ks:pad, :] = buf[pad + tm - ks:pad + tm, :]


def _conv_taps(buf, cw_ref, cols, width, s, pad, tm):
    y = None
    for k in range(width):
        r0 = pad - (width - 1 - k) * s
        term = cw_ref[k:k + 1, cols] * buf[r0:r0 + tm, cols]
        y = term if y is None else y + term
    return y


def _conv_body(x_ref, sh_ref, sc_ref, gt_ref, st0_ref, win_ref, cw_ref, wout_ref, g_ref, b_ref,
               y_ref, st_ref, abuf, acc, *, s, tm, cw, width, pad, alpha):
    d = x_ref.shape[-1]
    ks = (width - 1) * s
    t = pl.program_id(1)
    _load_history(t, abuf, st0_ref, pad, ks, tm)
    x = x_ref[...]
    u = (x * (1.0 + sc_ref[...]) + sh_ref[...]).astype(BF16)
    for c in range(d // cw):
        cols = slice(c * cw, (c + 1) * cw)
        h = _dot(u, win_ref[:, c * cw:(c + 1) * cw])
        cg = _dot(u, win_ref[:, 2 * d + c * cw:2 * d + (c + 1) * cw])
        abuf[pad:pad + tm, cols] = cg * h
        y = _conv_taps(abuf, cw_ref, cols, width, s, pad, tm)
        bg = _dot(u, win_ref[:, d + c * cw:d + (c + 1) * cw])
        z = _dot(u, win_ref[:, 3 * d + c * cw:3 * d + (c + 1) * cw])
        gated = (_silu(z) * bg * y).astype(BF16)
        part = _dot(gated, wout_ref[cols, :])
        if c == 0:
            acc[...] = part
        else:
            acc[...] += part
    st_ref[...] = abuf[pad + tm - ks:pad + tm, :]
    res = alpha * x + gt_ref[...] * acc[...]
    y_ref[...] = _layer_norm(res, g_ref[...], b_ref[...])


def _conv_layer(x, sh, sc, gt, st0, w_in, conv_w, w_out, ln_g, ln_b, *, s, tm, alpha):
    bsz, t_rows, d = x.shape
    width = conv_w.shape[0]
    ks = (width - 1) * s
    pad = _round_up(ks, SUBLANES)
    r = sh.shape[1]
    cw = 256
    body = functools.partial(_conv_body, s=s, tm=tm, cw=cw, width=width, pad=pad, alpha=alpha)
    row_spec = pl.BlockSpec((None, tm, d), lambda b, t: (b, t, 0))
    mod_spec = pl.BlockSpec((None, r, d), lambda b, t: (b, 0, 0))
    st_spec = pl.BlockSpec((None, ks, d), lambda b, t: (b, 0, 0))
    const = lambda shape: pl.BlockSpec(shape, lambda b, t: (0,) * len(shape))
    return pl.pallas_call(
        body,
        grid=(bsz, t_rows // tm),
        in_specs=[row_spec, mod_spec, mod_spec, mod_spec, st_spec,
                  const(w_in.shape), const(conv_w.shape), const(w_out.shape),
                  const((1, d)), const((1, d))],
        out_specs=[row_spec, st_spec],
        out_shape=[jax.ShapeDtypeStruct((bsz, t_rows, d), F32),
                   jax.ShapeDtypeStruct((bsz, ks, d), F32)],
        scratch_shapes=[pltpu.VMEM((pad + tm, d), F32), pltpu.VMEM((tm, d), F32)],
        compiler_params=_cparams(("arbitrary", "arbitrary")),
        name="conv_layer",
    )(x, sh, sc, gt, st0, w_in, conv_w, w_out, ln_g.reshape(1, d), ln_b.reshape(1, d))


def _lru_body(x_ref, sh_ref, sc_ref, gt_ref, st0_ref, h0_ref, win_ref, cw_ref, cb_ref,
              wga_ref, bga_ref, wgx_ref, bgx_ref, prm_ref, wout_ref, g_ref, b_ref,
              y_ref, st_ref, hl_ref, xbuf, a_s, b_s, h_s, hc, *, s, tm, width, pad, alpha):
    d = x_ref.shape[-1]
    nblk, blk, _ = wga_ref.shape
    ks = (width - 1) * s
    t = pl.program_id(1)
    _load_history(t, xbuf, st0_ref, pad, ks, tm)

    @pl.when(t == 0)
    def _():
        hc[...] = h0_ref[...]

    x = x_ref[...]
    u = (x * (1.0 + sc_ref[...]) + sh_ref[...]).astype(BF16)
    xbuf[pad:pad + tm, :] = _dot(u, win_ref[:, 0:d])
    st_ref[...] = xbuf[pad + tm - ks:pad + tm, :]
    for n in range(nblk):
        cols = slice(n * blk, (n + 1) * blk)
        xc = _conv_taps(xbuf, cw_ref, cols, width, s, pad, tm) + cb_ref[:, cols]
        xcb = xc.astype(BF16)
        r = jax.nn.sigmoid(_dot(xcb, wga_ref[n]) + bga_ref[:, cols])
        gi = jax.nn.sigmoid(_dot(xcb, wgx_ref[n]) + bgx_ref[:, cols])
        log_a = LRU_C * r * jax.nn.log_sigmoid(prm_ref[:, cols])
        a = jnp.exp(log_a)
        a_s[:, cols] = a
        b_s[:, cols] = jnp.sqrt(-jnp.tanh(log_a) * (a * a + 1.0)) * (gi * xc)

    steps = tm // s

    def step(i, h):
        r0 = pl.multiple_of(i * s, s)
        h = a_s[pl.ds(r0, s), :] * h + b_s[pl.ds(r0, s), :]
        h_s[pl.ds(r0, s), :] = h
        return h

    h_last = lax.fori_loop(0, steps, step, hc[...], unroll=min(steps, 8))
    hc[...] = h_last
    hl_ref[...] = h_last

    z = _dot(u, win_ref[:, d:2 * d])
    yy = (h_s[...] * _silu(z)).astype(BF16)
    out = _dot(yy, wout_ref[...])
    res = alpha * x + gt_ref[...] * out
    y_ref[...] = _layer_norm(res, g_ref[...], b_ref[...])


def _lru_layer(x, sh, sc, gt, st0, h0, w_in, conv_w, conv_b, w_ga, b_ga, w_gx, b_gx, prm, w_out,
               ln_g, ln_b, *, s, tm, alpha):
    bsz, t_rows, d = x.shape
    width = conv_w.shape[0]
    ks = (width - 1) * s
    pad = _round_up(ks, SUBLANES)
    r = sh.shape[1]
    body = functools.partial(_lru_body, s=s, tm=tm, width=width, pad=pad, alpha=alpha)
    row_spec = pl.BlockSpec((None, tm, d), lambda b, t: (b, t, 0))
    mod_spec = pl.BlockSpec((None, r, d), lambda b, t: (b, 0, 0))
    st_spec = pl.BlockSpec((None, ks, d), lambda b, t: (b, 0, 0))
    h_spec = pl.BlockSpec((None, s, d), lambda b, t: (b, 0, 0))
    const = lambda shape: pl.BlockSpec(shape, lambda b, t: (0,) * len(shape))
    vec = const((1, d))
    return pl.pallas_call(
        body,
        grid=(bsz, t_rows // tm),
        in_specs=[row_spec, mod_spec, mod_spec, mod_spec, st_spec, h_spec,
                  const(w_in.shape), const(conv_w.shape), vec,
                  const(w_ga.shape), vec, const(w_gx.shape), vec, vec,
                  const(w_out.shape), vec, vec],
        out_specs=[row_spec, st_spec, h_spec],
        out_shape=[jax.ShapeDtypeStruct((bsz, t_rows, d), F32),
                   jax.ShapeDtypeStruct((bsz, ks, d), F32),
                   jax.ShapeDtypeStruct((bsz, s, d), F32)],
        scratch_shapes=[pltpu.VMEM((pad + tm, d), F32), pltpu.VMEM((tm, d), F32),
                        pltpu.VMEM((tm, d), F32), pltpu.VMEM((tm, d), F32),
                        pltpu.VMEM((s, d), F32)],
        compiler_params=_cparams(("arbitrary", "arbitrary")),
        name="lru_layer",
    )(x, sh, sc, gt, st0, h0, w_in, conv_w, conv_b.reshape(1, d), w_ga, b_ga.reshape(1, d),
      w_gx, b_gx.reshape(1, d), prm.reshape(1, d), w_out, ln_g.reshape(1, d), ln_b.reshape(1, d))


def _rope_block(xb, cos, sin_lo, sin_hi, half):
    return (xb * cos + pltpu.roll(xb, LANES - half, axis=1) * sin_lo
            + pltpu.roll(xb, half, axis=1) * sin_hi)


def _qkv_body(x_ref, sh_ref, sc_ref, cos_ref, slo_ref, shi_ref, win_ref,
              k_ref, v_ref, z_ref, qb_ref, kb_ref, vb_ref, *, half, qscale):
    d = x_ref.shape[-1]
    x = x_ref[...]
    u = (x * (1.0 + sc_ref[...]) + sh_ref[...]).astype(BF16)
    cos, slo, shi = cos_ref[...], slo_ref[...], shi_ref[...]
    q = _dot(u, win_ref[:, 0:d])
    for j in range(d // LANES):
        cols = slice(j * LANES, (j + 1) * LANES)
        qb_ref[:, cols] = (_rope_block(q[:, cols], cos, slo, shi, half) * qscale).astype(BF16)
    k = _dot(u, win_ref[:, d:2 * d])
    for j in range(d // LANES):
        cols = slice(j * LANES, (j + 1) * LANES)
        kr = _rope_block(k[:, cols], cos, slo, shi, half)
        k_ref[:, cols] = kr
        kb_ref[:, cols] = kr.astype(BF16)
    v = _dot(u, win_ref[:, 2 * d:3 * d])
    v_ref[...] = v
    vb_ref[...] = v.astype(BF16)
    z_ref[...] = _dot(u, win_ref[:, 3 * d:4 * d])


def _qkv_call(x, sh, sc, cos_t, slo_t, shi_t, w_in, *, tm, half, qscale):
    bsz, t_rows, d = x.shape
    r = sh.shape[1]
    body = functools.partial(_qkv_body, half=half, qscale=qscale)
    row_spec = pl.BlockSpec((None, tm, d), lambda b, t: (b, t, 0))
    mod_spec = pl.BlockSpec((None, r, d), lambda b, t: (b, 0, 0))
    tab_spec = pl.BlockSpec((tm, LANES), lambda b, t: (t, 0))
    f32_out = jax.ShapeDtypeStruct((bsz, t_rows, d), F32)
    bf_out = jax.ShapeDtypeStruct((bsz, t_rows, d), BF16)
    return pl.pallas_call(
        body,
        grid=(bsz, t_rows // tm),
        in_specs=[row_spec, mod_spec, mod_spec, tab_spec, tab_spec, tab_spec,
                  pl.BlockSpec(w_in.shape, lambda b, t: (0, 0))],
        out_specs=[row_spec] * 6,
        out_shape=[f32_out, f32_out, f32_out, bf_out, bf_out, bf_out],
        compiler_params=_cparams(("arbitrary", "arbitrary")),
        name="attn_qkv",
    )(x, sh, sc, cos_t, slo_t, shi_t, w_in)


def _rope_tables(pos, head_dim, rot_dim):
    half = rot_dim // 2
    inv_freq = jnp.exp(jnp.arange(half, dtype=F32) * (-2.0 * math.log(ROPE_THETA) / rot_dim))
    ang = pos.astype(F32)[:, None] * inv_freq[None, :]
    cos, sin = jnp.cos(ang), jnp.sin(ang)
    n = pos.shape[0]
    ones = jnp.ones((n, head_dim - rot_dim), F32)
    zeros = jnp.zeros((n, head_dim - rot_dim), F32)
    zh = jnp.zeros((n, half), F32)
    cos_h = jnp.concatenate([cos, cos, ones], axis=1)
    slo_h = jnp.concatenate([-sin, zh, zeros], axis=1)
    shi_h = jnp.concatenate([zh, sin, zeros], axis=1)
    rep = LANES // head_dim
    return (jnp.tile(cos_h, (1, rep)), jnp.tile(slo_h, (1, rep)), jnp.tile(shi_h, (1, rep)))


def _diff_lambda(lq1_ref, lk1_ref, lq2_ref, lk2_ref, lam_init):
    s1 = jnp.sum(lq1_ref[...] * lk1_ref[...], axis=-1, keepdims=True)
    s2 = jnp.sum(lq2_ref[...] * lk2_ref[...], axis=-1, keepdims=True)
    return jnp.exp(s1) - jnp.exp(s2) + lam_init


def _flash_body(q_ref, k_ref, v_ref, lq1_ref, lk1_ref, lq2_ref, lk2_ref, o_ref,
                qq_s, m_s, acc_s, *, tq, head_dim, nh, lam_init):
    qi = pl.program_id(2)
    lane = lax.broadcasted_iota(jnp.int32, (tq, LANES), 1)
    for h in range(nh):
        q = q_ref[:, h * LANES:(h + 1) * LANES]
        zero = jnp.zeros_like(q)
        qq_s[h, 0:tq, :] = jnp.where(lane < head_dim, q, zero)
        qq_s[h, tq:2 * tq, :] = jnp.where(lane >= head_dim, q, zero)
    m_s[...] = jnp.full(m_s.shape, -jnp.inf, F32)
    acc_s[...] = jnp.zeros(acc_s.shape, F32)
    ones = jnp.ones((tq, LANES), BF16)

    def kv_block(j, masked):
        r0 = pl.multiple_of(j * tq, tq)
        for h in range(nh):
            cols = slice(h * LANES, (h + 1) * LANES)
            k = k_ref[pl.ds(r0, tq), cols]
            v1 = jnp.concatenate([v_ref[pl.ds(r0, tq), cols], ones], axis=1)
            sc = _dot_nt(qq_s[h], k)
            if masked:
                row = lax.broadcasted_iota(jnp.int32, sc.shape, 0)
                row = jnp.where(row >= tq, row - tq, row)
                col = lax.broadcasted_iota(jnp.int32, sc.shape, 1)
                sc = jnp.where(col <= row, sc, -jnp.inf)
            m_prev = m_s[h]
            m_new = jnp.maximum(m_prev, jnp.max(sc, axis=1, keepdims=True))
            alpha = jnp.exp(m_prev - m_new)
            p = jnp.exp(sc - jnp.concatenate([m_new] * (tq // LANES), axis=1)).astype(BF16)
            acc_s[h] = jnp.concatenate([alpha, alpha], axis=1) * acc_s[h] + _dot(p, v1)
            m_s[h] = m_new

    def loop_body(j, carry):
        kv_block(j, False)
        return carry

    lax.fori_loop(0, qi, loop_body, 0)
    kv_block(qi, True)

    lam = _diff_lambda(lq1_ref, lk1_ref, lq2_ref, lk2_ref, lam_init)
    for h in range(nh):
        acc = acc_s[h]
        o = acc[:, 0:LANES] / acc[:, LANES:2 * LANES]
        o_ref[:, h * LANES:(h + 1) * LANES] = o[0:tq, :] - lam * o[tq:2 * tq, :]


def _flash_call(qb, kb, vb, lq1, lk1, lq2, lk2, *, tq, head_dim, nh, lam_init):
    bsz, t_rows, d = qb.shape
    n_heads = d // LANES
    body = functools.partial(_flash_body, tq=tq, head_dim=head_dim, nh=nh, lam_init=lam_init)
    q_spec = pl.BlockSpec((None, tq, nh * LANES), lambda b, h, i: (b, i, h))
    kv_spec = pl.BlockSpec((None, t_rows, nh * LANES), lambda b, h, i: (b, 0, h))
    l_spec = pl.BlockSpec((1, head_dim), lambda b, h, i: (0, 0))
    return pl.pallas_call(
        body,
        grid=(bsz, n_heads // nh, t_rows // tq),
        in_specs=[q_spec, kv_spec, kv_spec, l_spec, l_spec, l_spec, l_spec],
        out_specs=q_spec,
        out_shape=jax.ShapeDtypeStruct((bsz, t_rows, d), F32),
        scratch_shapes=[pltpu.VMEM((nh, 2 * tq, LANES), BF16),
                        pltpu.VMEM((nh, 2 * tq, LANES), F32),
                        pltpu.VMEM((nh, 2 * tq, 2 * LANES), F32)],
        compiler_params=_cparams(("arbitrary", "arbitrary", "arbitrary")),
        name="attn_flash",
    )(qb, kb, vb, lq1, lk1, lq2, lk2)


def _decode_body(pt_ref, qx_ref, *refs, n_steps, gp, t_new, n_heads, lam_init):
    k_refs, v_refs = refs[:gp], refs[gp:2 * gp]
    (kn_ref, vn_ref, lq1_ref, lk1_ref, lq2_ref, lk2_ref, o_ref,
     m_s, l_s, acc_s, kpad, vpad) = refs[2 * gp:]
    b = pl.program_id(0)
    p = pl.program_id(1)
    page = k_refs[0].shape[1]
    grp = 2 * t_new

    @pl.when((b == 0) & (p == 0))
    def _():
        kpad[...] = jnp.zeros(kpad.shape, F32)
        vpad[...] = jnp.zeros(vpad.shape, F32)

    @pl.when(p == 0)
    def _():
        m_s[...] = jnp.full(m_s.shape, -jnp.inf, F32)
        l_s[...] = jnp.zeros(l_s.shape, F32)
        acc_s[...] = jnp.zeros(acc_s.shape, F32)

    def update(sc, vs):
        m_prev = m_s[...]
        m_new = jnp.maximum(m_prev, jnp.max(sc, axis=1, keepdims=True))
        alpha = jnp.exp(m_prev - m_new)
        pe = jnp.exp(sc - m_new)
        l_s[...] = alpha * l_s[...] + jnp.sum(pe, axis=1, keepdims=True)
        m_s[...] = m_new
        pb = pe.astype(BF16)
        for hv in range(n_heads):
            rs = slice(hv * grp, (hv + 1) * grp)
            vh = jnp.concatenate(
                [v[pl.ds(hv, page, stride=n_heads), :].astype(BF16) for v in vs], axis=0)
            acc_s[rs, :] = alpha[rs, :] * acc_s[rs, :] + _dot(pb[rs, :], vh)

    kt = jnp.concatenate([k[...].astype(BF16) for k in k_refs], axis=1)
    update(_dot(qx_ref[...], kt), v_refs)

    @pl.when(p == n_steps - 1)
    def _():
        kpad[0:t_new, :] = kn_ref[...]
        vpad[0:t_new * n_heads, :] = vn_ref[...]
        sc = _dot_nt(qx_ref[...], kpad[...].astype(BF16))
        row = lax.broadcasted_iota(jnp.int32, sc.shape, 0)
        col = lax.broadcasted_iota(jnp.int32, sc.shape, 1)
        update(jnp.where(col <= row % t_new, sc, -jnp.inf), [vpad])
        o = acc_s[...] / l_s[...]
        lam = _diff_lambda(lq1_ref, lk1_ref, lq2_ref, lk2_ref, lam_init)
        vd = o.shape[1]
        for hv in range(n_heads):
            o1 = o[hv * grp:hv * grp + t_new, :]
            o2 = o[hv * grp + t_new:(hv + 1) * grp, :]
            o_ref[:, hv * vd:(hv + 1) * vd] = o1 - lam * o2


def _decode_call(page_table, qx, cache_kt, cache_vr, layer, k_new, v_new, lq1, lk1, lq2, lk2, *,
                 gp, n_heads, lam_init):
    bsz, n_pages = page_table.shape
    _, _, kd, page = cache_kt.shape
    vd = cache_vr.shape[-1]
    rows = qx.shape[1]
    t_new = k_new.shape[1]
    head_dim = lq1.shape[-1]
    n_steps = n_pages // gp
    body = functools.partial(_decode_body, n_steps=n_steps, gp=gp, t_new=t_new, n_heads=n_heads,
                             lam_init=lam_init)
    l_spec = pl.BlockSpec((1, head_dim), lambda b, p, pt: (0, 0))

    def page_spec(shape, g):
        return pl.BlockSpec((None, None) + shape,
                            lambda b, p, pt: (layer, pt[b * n_pages + p * gp + g], 0, 0))

    grid_spec = pltpu.PrefetchScalarGridSpec(
        num_scalar_prefetch=1,
        grid=(bsz, n_steps),
        in_specs=([pl.BlockSpec((None, rows, kd), lambda b, p, pt: (b, 0, 0))]
                  + [page_spec((kd, page), g) for g in range(gp)]
                  + [page_spec((page * n_heads, vd), g) for g in range(gp)]
                  + [pl.BlockSpec((None, t_new, kd), lambda b, p, pt: (b, 0, 0)),
                     pl.BlockSpec((None, t_new * n_heads, vd), lambda b, p, pt: (b, 0, 0)),
                     l_spec, l_spec, l_spec, l_spec]),
        out_specs=pl.BlockSpec((None, t_new, n_heads * vd), lambda b, p, pt: (b, 0, 0)),
        scratch_shapes=[pltpu.VMEM((rows, 1), F32), pltpu.VMEM((rows, 1), F32),
                        pltpu.VMEM((rows, vd), F32),
                        pltpu.VMEM((page, kd), F32),
                        pltpu.VMEM((page * n_heads, vd), F32)],
    )
    return pl.pallas_call(
        body,
        grid_spec=grid_spec,
        out_shape=jax.ShapeDtypeStruct((bsz, t_new, n_heads * vd), F32),
        compiler_params=_cparams(("arbitrary", "arbitrary")),
        name="attn_decode",
    )(page_table.reshape(-1), qx, *([cache_kt] * gp), *([cache_vr] * gp), k_new, v_new,
      lq1, lk1, lq2, lk2)


def _attn_out_body(x_ref, gt_ref, o_ref, z_ref, sg_ref, wout_ref, g_ref, b_ref, y_ref, gbuf, *,
                   alpha, out_scale):
    d = x_ref.shape[-1]
    vd = sg_ref.shape[-1]
    for hv in range(d // vd):
        cols = slice(hv * vd, (hv + 1) * vd)
        of = o_ref[:, cols]
        of = of * lax.rsqrt(jnp.mean(of * of, axis=-1, keepdims=True) + SUBLN_EPS)
        of = of * sg_ref[...] * out_scale
        gbuf[:, cols] = (of * _silu(z_ref[:, cols])).astype(BF16)
    out = _dot(gbuf[...], wout_ref[...])
    res = alpha * x_ref[...] + gt_ref[...] * out
    y_ref[...] = _layer_norm(res, g_ref[...], b_ref[...])


def _attn_out_call(x, gt, o, z, subln_g, w_out, ln_g, ln_b, *, tm, alpha, out_scale):
    bsz, t_rows, d = x.shape
    r = gt.shape[1]
    vd = subln_g.shape[-1]
    body = functools.partial(_attn_out_body, alpha=alpha, out_scale=out_scale)
    row_spec = pl.BlockSpec((None, tm, d), lambda b, t: (b, t, 0))
    mod_spec = pl.BlockSpec((None, r, d), lambda b, t: (b, 0, 0))
    const = lambda shape: pl.BlockSpec(shape, lambda b, t: (0,) * len(shape))
    return pl.pallas_call(
        body,
        grid=(bsz, t_rows // tm),
        in_specs=[row_spec, mod_spec, row_spec, row_spec, const((1, vd)), const(w_out.shape),
                  const((1, d)), const((1, d))],
        out_specs=row_spec,
        out_shape=jax.ShapeDtypeStruct((bsz, t_rows, d), F32),
        scratch_shapes=[pltpu.VMEM((tm, d), BF16)],
        compiler_params=_cparams(("arbitrary", "arbitrary")),
        name="attn_out",
    )(x, gt, o, z, subln_g.reshape(1, vd), w_out, ln_g.reshape(1, d), ln_b.reshape(1, d))


def kernel(x_prompt, x_sample, state_conv_a, state_lru_h, state_lru_conv, cache_k, cache_v, page_table, c_prompt, c_sample, w_ada, b_ada, ln_g, ln_b, a_w_in, a_conv_w, a_w_out, r_w_in, r_conv_w, r_conv_b, r_w_ga, r_b_ga, r_w_gx, r_b_gx, r_lru_param, r_w_out, d_w_in, d_lq1, d_lk1, d_lq2, d_lk2, d_subln_g, d_w_out):
    bp, tp, d = x_prompt.shape
    bs, ts, _ = x_sample.shape
    depth = w_ada.shape[0]
    n_pages = page_table.shape[1]
    page = cache_k.shape[2]
    n_sub, head_dim = cache_k.shape[3], cache_k.shape[4]
    n_heads, vd = cache_v.shape[3], cache_v.shape[4]
    past_len = n_pages * page
    rot_dim = head_dim // 4
    alpha = (2 * depth) ** 0.25
    rows_s = bs * ts

    n_c = _round_up(bp + bs, SUBLANES)
    c_all = jnp.concatenate([c_prompt, c_sample, jnp.zeros((n_c - bp - bs, d), F32)], axis=0)
    ada = _ada_call(c_all, w_ada, b_ada)

    def mods(i):
        m = ada[i]
        mp = [m[:bp, k * d:(k + 1) * d].reshape(bp, 1, d) for k in range(3)]
        ms = [jnp.tile(m[bp:bp + bs, k * d:(k + 1) * d], (ts, 1)).reshape(1, rows_s, d)
              for k in range(3)]
        return mp, ms

    def to_time_major(a):
        return jnp.swapaxes(a, 0, 1).reshape((1, a.shape[1] * bs) + a.shape[2:])

    def from_time_major(a, n):
        return jnp.swapaxes(a.reshape(n, bs, a.shape[-1]), 0, 1)

    xp = x_prompt
    xs = to_time_major(x_sample)
    bf = lambda w: w.astype(BF16)

    conv_p, conv_s, lruh_p, lruh_s, lruc_p, lruc_s = [], [], [], [], [], []
    k_p, v_p, k_s, v_s = [], [], [], []
    for i in range(depth):
        kind, j = i % N_MIXERS, i // N_MIXERS
        (shp, scp, gtp), (shs, scs, gts) = mods(i)
        if kind == 0:
            w_in, w_out = bf(a_w_in[j]), bf(a_w_out[j])
            width = a_conv_w.shape[1]
            args = (w_in, a_conv_w[j], w_out, ln_g[i], ln_b[i])
            xp, nbp = _conv_layer(xp, shp, scp, gtp, jnp.zeros((bp, width - 1, d), F32), *args,
                                  s=1, tm=512, alpha=alpha)
            xs, nbs = _conv_layer(xs, shs, scs, gts, to_time_major(state_conv_a[j]), *args,
                                  s=bs, tm=rows_s, alpha=alpha)
            conv_p.append(nbp)
            conv_s.append(from_time_major(nbs, width - 1))
        elif kind == 1:
            width = r_conv_w.shape[1]
            args = (bf(r_w_in[j]), r_conv_w[j], r_conv_b[j], bf(r_w_ga[j]), r_b_ga[j],
                    bf(r_w_gx[j]), r_b_gx[j], r_lru_param[j], bf(r_w_out[j]), ln_g[i], ln_b[i])
            xp, nbp, hp = _lru_layer(xp, shp, scp, gtp, jnp.zeros((bp, width - 1, d), F32),
                                     jnp.zeros((bp, 1, d), F32), *args, s=1, tm=256, alpha=alpha)
            xs, nbs, hs = _lru_layer(xs, shs, scs, gts, to_time_major(state_lru_conv[j]),
                                     state_lru_h[j].reshape(1, bs, d), *args,
                                     s=bs, tm=rows_s, alpha=alpha)
            lruh_p.append(hp.reshape(bp, d))
            lruh_s.append(hs.reshape(bs, d))
            lruc_p.append(nbp)
            lruc_s.append(from_time_major(nbs, width - 1))
        else:
            lam_init = 0.8 - 0.6 * math.exp(-0.3 * i)
            w_in, w_out = bf(d_w_in[j]), bf(d_w_out[j])
            half = rot_dim // 2
            qscale = head_dim ** -0.5
            lvec = [v[j].reshape(1, head_dim) for v in (d_lq1, d_lk1, d_lq2, d_lk2)]
            tabs_p = _rope_tables(jnp.arange(tp), head_dim, rot_dim)
            kp, vp, zp, qb, kb, vb = _qkv_call(xp, shp, scp, *tabs_p, w_in, tm=256, half=half,
                                               qscale=qscale)
            op = _flash_call(qb, kb, vb, *lvec, tq=256, head_dim=head_dim, nh=2,
                             lam_init=lam_init)
            xp = _attn_out_call(xp, gtp, op, zp, d_subln_g[j], w_out, ln_g[i], ln_b[i], tm=512,
                                alpha=alpha, out_scale=1.0 - lam_init)
            k_p.append(kp.reshape(bp, tp, n_sub, head_dim))
            v_p.append(vp.reshape(bp, tp, n_heads, vd))
            pos_s = past_len + jnp.repeat(jnp.arange(ts), bs)
            tabs_s = _rope_tables(pos_s, head_dim, rot_dim)
            ks_, vs_, zs, qsb, _, _ = _qkv_call(xs, shs, scs, *tabs_s, w_in, tm=rows_s, half=half,
                                                qscale=qscale)
            ksn = from_time_major(ks_, ts)
            vsn = from_time_major(vs_, ts)
            q4 = from_time_major(qsb, ts).reshape(bs, 1, ts, n_sub, head_dim)
            eye = jnp.eye(n_sub, dtype=BF16).reshape(1, n_sub, 1, n_sub, 1)
            qx = (eye * q4).reshape(bs, n_sub * ts, n_sub * head_dim)
            n_layers, n_phys = cache_k.shape[0], cache_k.shape[1]
            cache_kt = jnp.transpose(cache_k, (0, 1, 3, 4, 2)).reshape(
                n_layers, n_phys, n_sub * head_dim, page)
            cache_vr = cache_v.reshape(n_layers, n_phys, page * n_heads, vd)
            os_ = _decode_call(page_table, qx, cache_kt, cache_vr, j, ksn,
                               vsn.reshape(bs, ts * n_heads, vd), *lvec,
                               gp=8, n_heads=n_heads, lam_init=lam_init)
            xs = _attn_out_call(xs, gts, to_time_major(os_), zs, d_subln_g[j], w_out, ln_g[i],
                                ln_b[i], tm=rows_s, alpha=alpha, out_scale=1.0 - lam_init)
            k_s.append(ksn.reshape(bs, ts, n_sub, head_dim))
            v_s.append(vsn.reshape(bs, ts, n_heads, vd))

    return (xp, from_time_major(xs, ts),
            jnp.stack(conv_p), jnp.stack(conv_s),
            jnp.stack(lruh_p), jnp.stack(lruh_s),
            jnp.stack(lruc_p), jnp.stack(lruc_s),
            jnp.stack(k_p), jnp.stack(v_p), jnp.stack(k_s), jnp.stack(v_s))
```

```python
import functools
import math

import jax
import jax.numpy as jnp
from jax import lax
from jax.experimental import pallas as pl
from jax.experimental.pallas import tpu as pltpu

F32 = jnp.float32
BF16 = jnp.bfloat16

LN_EPS = 1e-5
SUBLN_EPS = 1e-5
LRU_C = 8.0
ROPE_THETA = 500000.0
N_MIXERS = 3

SUBLANES = 8
LANES = 128
VMEM_LIMIT_BYTES = 56 * 1024 * 1024


def _cparams(semantics):
    return pltpu.CompilerParams(dimension_semantics=semantics, vmem_limit_bytes=VMEM_LIMIT_BYTES)


def _dot(a, b):
    return jnp.dot(a, b, preferred_element_type=F32)


def _dot_nt(a, b):
    return lax.dot_general(a, b, (((1,), (1,)), ((), ())), preferred_element_type=F32)


def _silu(z):
    return z * jax.nn.sigmoid(z)


def _layer_norm(y, g, b):
    mu = jnp.mean(y, axis=-1, keepdims=True)
    yc = y - mu
    var = jnp.mean(yc * yc, axis=-1, keepdims=True)
    return yc * lax.rsqrt(var + LN_EPS) * g + b


def _round_up(n, m):
    return (n + m - 1) // m * m


def _ada_body(c_ref, w_ref, b_ref, o_ref):
    c = c_ref[...]
    a = _silu(c).astype(BF16)
    o_ref[...] = _dot(a, w_ref[...].astype(BF16)) + b_ref[...]


def _ada_call(c_all, w_ada, b_ada):
    depth, d, d3 = w_ada.shape
    rows = c_all.shape[0]
    nt = d3 // d
    return pl.pallas_call(
        _ada_body,
        grid=(depth, nt),
        in_specs=[
            pl.BlockSpec((rows, d), lambda i, n: (0, 0)),
            pl.BlockSpec((None, d, d), lambda i, n: (i, 0, n)),
            pl.BlockSpec((None, 1, d), lambda i, n: (i, 0, n)),
        ],
        out_specs=pl.BlockSpec((None, rows, d), lambda i, n: (i, 0, n)),
        out_shape=jax.ShapeDtypeStruct((depth, rows, d3), F32),
        compiler_params=_cparams(("arbitrary", "arbitrary")),
        name="adaln",
    )(c_all, w_ada, b_ada.reshape(depth, 1, d3))


def _load_history(t, buf, st0_ref, pad, ks, tm):
    @pl.when(t == 0)
    def _():
        buf[pad - ks:pad, :] = st0_ref[...]

    @pl.when(t > 0)
    def _():
        buf[pad - ks:pad, :] = buf[pad + tm - ks:pad + tm, :]


def _conv_taps(buf, cw_ref, cols, width, s, pad, tm):
    y = None
    for k in range(width):
        r0 = pad - (width - 1 - k) * s
        term = cw_ref[k:k + 1, cols] * buf[r0:r0 + tm, cols]
        y = term if y is None else y + term
    return y


def _conv_body(x_ref, sh_ref, sc_ref, gt_ref, st0_ref, win_ref, cw_ref, wout_ref, g_ref, b_ref,
               y_ref, st_ref, abuf, acc, *, s, tm, cw, width, pad, alpha):
    d = x_ref.shape[-1]
    ks = (width - 1) * s
    t = pl.program_id(1)
    _load_history(t, abuf, st0_ref, pad, ks, tm)
    x = x_ref[...]
    u = (x * (1.0 + sc_ref[...]) + sh_ref[...]).astype(BF16)
    for c in range(d // cw):
        cols = slice(c * cw, (c + 1) * cw)
        h = _dot(u, win_ref[:, c * cw:(c + 1) * cw])
        cg = _dot(u, win_ref[:, 2 * d + c * cw:2 * d + (c + 1) * cw])
        abuf[pad:pad + tm, cols] = cg * h
        y = _conv_taps(abuf, cw_ref, cols, width, s, pad, tm)
        bg = _dot(u, win_ref[:, d + c * cw:d + (c + 1) * cw])
        z = _dot(u, win_ref[:, 3 * d + c * cw:3 * d + (c + 1) * cw])
        gated = (_silu(z) * bg * y).astype(BF16)
        part = _dot(gated, wout_ref[cols, :])
        if c == 0:
            acc[...] = part
        else:
            acc[...] += part
    st_ref[...] = abuf[pad + tm - ks:pad + tm, :]
    res = alpha * x + gt_ref[...] * acc[...]
    y_ref[...] = _layer_norm(res, g_ref[...], b_ref[...])


def _conv_layer(x, sh, sc, gt, st0, w_in, conv_w, w_out, ln_g, ln_b, *, s, tm, alpha):
    bsz, t_rows, d = x.shape
    width = conv_w.shape[0]
    ks = (width - 1) * s
    pad = _round_up(ks, SUBLANES)
    r = sh.shape[1]
    cw = 256
    body = functools.partial(_conv_body, s=s, tm=tm, cw=cw, width=width, pad=pad, alpha=alpha)
    row_spec = pl.BlockSpec((None, tm, d), lambda b, t: (b, t, 0))
    mod_spec = pl.BlockSpec((None, r, d), lambda b, t: (b, 0, 0))
    st_spec = pl.BlockSpec((None, ks, d), lambda b, t: (b, 0, 0))
    const = lambda shape: pl.BlockSpec(shape, lambda b, t: (0,) * len(shape))
    return pl.pallas_call(
        body,
        grid=(bsz, t_rows // tm),
        in_specs=[row_spec, mod_spec, mod_spec, mod_spec, st_spec,
                  const(w_in.shape), const(conv_w.shape), const(w_out.shape),
                  const((1, d)), const((1, d))],
        out_specs=[row_spec, st_spec],
        out_shape=[jax.ShapeDtypeStruct((bsz, t_rows, d), F32),
                   jax.ShapeDtypeStruct((bsz, ks, d), F32)],
        scratch_shapes=[pltpu.VMEM((pad + tm, d), F32), pltpu.VMEM((tm, d), F32)],
        compiler_params=_cparams(("arbitrary", "arbitrary")),
        name="conv_layer",
    )(x, sh, sc, gt, st0, w_in, conv_w, w_out, ln_g.reshape(1, d), ln_b.reshape(1, d))


def _lru_body(x_ref, sh_ref, sc_ref, gt_ref, st0_ref, h0_ref, win_ref, cw_ref, cb_ref,
              wga_ref, bga_ref, wgx_ref, bgx_ref, prm_ref, wout_ref, g_ref, b_ref,
              y_ref, st_ref, hl_ref, xbuf, a_s, b_s, h_s, hc, *, s, tm, width, pad, alpha):
    d = x_ref.shape[-1]
    nblk, blk, _ = wga_ref.shape
    ks = (width - 1) * s
    t = pl.program_id(1)
    _load_history(t, xbuf, st0_ref, pad, ks, tm)

    @pl.when(t == 0)
    def _():
        hc[...] = h0_ref[...]

    x = x_ref[...]
    u = (x * (1.0 + sc_ref[...]) + sh_ref[...]).astype(BF16)
    xbuf[pad:pad + tm, :] = _dot(u, win_ref[:, 0:d])
    st_ref[...] = xbuf[pad + tm - ks:pad + tm, :]
    for n in range(nblk):
        cols = slice(n * blk, (n + 1) * blk)
        xc = _conv_taps(xbuf, cw_ref, cols, width, s, pad, tm) + cb_ref[:, cols]
        xcb = xc.astype(BF16)
        r = jax.nn.sigmoid(_dot(xcb, wga_ref[n]) + bga_ref[:, cols])
        gi = jax.nn.sigmoid(_dot(xcb, wgx_ref[n]) + bgx_ref[:, cols])
        log_a = LRU_C * r * jax.nn.log_sigmoid(prm_ref[:, cols])
        a = jnp.exp(log_a)
        a_s[:, cols] = a
        b_s[:, cols] = jnp.sqrt(-jnp.tanh(log_a) * (a * a + 1.0)) * (gi * xc)

    steps = tm // s

    def step(i, h):
        r0 = pl.multiple_of(i * s, s)
        h = a_s[pl.ds(r0, s), :] * h + b_s[pl.ds(r0, s), :]
        h_s[pl.ds(r0, s), :] = h
        return h

    h_last = lax.fori_loop(0, steps, step, hc[...], unroll=True)
    hc[...] = h_last
    hl_ref[...] = h_last

    z = _dot(u, win_ref[:, d:2 * d])
    yy = (h_s[...] * _silu(z)).astype(BF16)
    out = _dot(yy, wout_ref[...])
    res = alpha * x + gt_ref[...] * out
    y_ref[...] = _layer_norm(res, g_ref[...], b_ref[...])


def _lru_layer(x, sh, sc, gt, st0, h0, w_in, conv_w, conv_b, w_ga, b_ga, w_gx, b_gx, prm, w_out,
               ln_g, ln_b, *, s, tm, alpha):
    bsz, t_rows, d = x.shape
    width = conv_w.shape[0]
    ks = (width - 1) * s
    pad = _round_up(ks, SUBLANES)
    r = sh.shape[1]
    body = functools.partial(_lru_body, s=s, tm=tm, width=width, pad=pad, alpha=alpha)
    row_spec = pl.BlockSpec((None, tm, d), lambda b, t: (b, t, 0))
    mod_spec = pl.BlockSpec((None, r, d), lambda b, t: (b, 0, 0))
    st_spec = pl.BlockSpec((None, ks, d), lambda b, t: (b, 0, 0))
    h_spec = pl.BlockSpec((None, s, d), lambda b, t: (b, 0, 0))
    const = lambda shape: pl.BlockSpec(shape, lambda b, t: (0,) * len(shape))
    vec = const((1, d))
    return pl.pallas_call(
        body,
        grid=(bsz, t_rows // tm),
        in_specs=[row_spec, mod_spec, mod_spec, mod_spec, st_spec, h_spec,
                  const(w_in.shape), const(conv_w.shape), vec,
                  const(w_ga.shape), vec, const(w_gx.shape), vec, vec,
                  const(w_out.shape), vec, vec],
        out_specs=[row_spec, st_spec, h_spec],
        out_shape=[jax.ShapeDtypeStruct((bsz, t_rows, d), F32),
                   jax.ShapeDtypeStruct((bsz, ks, d), F32),
                   jax.ShapeDtypeStruct((bsz, s, d), F32)],
        scratch_shapes=[pltpu.VMEM((pad + tm, d), F32), pltpu.VMEM((tm, d), F32),
                        pltpu.VMEM((tm, d), F32), pltpu.VMEM((tm, d), F32),
                        pltpu.VMEM((s, d), F32)],
        compiler_params=_cparams(("arbitrary", "arbitrary")),
        name="lru_layer",
    )(x, sh, sc, gt, st0, h0, w_in, conv_w, conv_b.reshape(1, d), w_ga, b_ga.reshape(1, d),
      w_gx, b_gx.reshape(1, d), prm.reshape(1, d), w_out, ln_g.reshape(1, d), ln_b.reshape(1, d))


def _rope_block(xb, cos, sin_lo, sin_hi, half):
    return (xb * cos + pltpu.roll(xb, LANES - half, axis=1) * sin_lo
            + pltpu.roll(xb, half, axis=1) * sin_hi)


def _qkv_body(x_ref, sh_ref, sc_ref, cos_ref, slo_ref, shi_ref, win_ref,
              k_ref, v_ref, z_ref, qb_ref, *attn_refs, half, qscale):
    d = x_ref.shape[-1]
    x = x_ref[...]
    u = (x * (1.0 + sc_ref[...]) + sh_ref[...]).astype(BF16)
    cos, slo, shi = cos_ref[...], slo_ref[...], shi_ref[...]
    q = _dot(u, win_ref[:, 0:d])
    for j in range(d // LANES):
        cols = slice(j * LANES, (j + 1) * LANES)
        qb_ref[:, cols] = (_rope_block(q[:, cols], cos, slo, shi, half) * qscale).astype(BF16)
    k = _dot(u, win_ref[:, d:2 * d])
    for j in range(d // LANES):
        cols = slice(j * LANES, (j + 1) * LANES)
        kr = _rope_block(k[:, cols], cos, slo, shi, half)
        k_ref[:, cols] = kr
        if attn_refs:
            attn_refs[0][:, cols] = kr.astype(BF16)
    v = _dot(u, win_ref[:, 2 * d:3 * d])
    v_ref[...] = v
    if attn_refs:
        vt_ref = attn_refs[1]
        vd = vt_ref.shape[1]
        for hv in range(vt_ref.shape[0]):
            vt_ref[hv] = v[:, hv * vd:(hv + 1) * vd].T.astype(BF16)
    z_ref[...] = _dot(u, win_ref[:, 3 * d:4 * d])


def _qkv_call(x, sh, sc, cos_t, slo_t, shi_t, w_in, *, tm, half, qscale, attn_vd=None):
    bsz, t_rows, d = x.shape
    r = sh.shape[1]
    body = functools.partial(_qkv_body, half=half, qscale=qscale)
    row_spec = pl.BlockSpec((None, tm, d), lambda b, t: (b, t, 0))
    mod_spec = pl.BlockSpec((None, r, d), lambda b, t: (b, 0, 0))
    tab_spec = pl.BlockSpec((tm, LANES), lambda b, t: (t, 0))
    f32_out = jax.ShapeDtypeStruct((bsz, t_rows, d), F32)
    bf_out = jax.ShapeDtypeStruct((bsz, t_rows, d), BF16)
    out_specs = [row_spec] * 4
    out_shape = [f32_out, f32_out, f32_out, bf_out]
    if attn_vd is not None:
        n_heads = d // attn_vd
        out_specs += [row_spec, pl.BlockSpec((None, n_heads, None, attn_vd, tm),
                                             lambda b, t: (b, 0, t, 0, 0))]
        out_shape += [bf_out,
                      jax.ShapeDtypeStruct((bsz, n_heads, t_rows // tm, attn_vd, tm), BF16)]
    return pl.pallas_call(
        body,
        grid=(bsz, t_rows // tm),
        in_specs=[row_spec, mod_spec, mod_spec, tab_spec, tab_spec, tab_spec,
                  pl.BlockSpec(w_in.shape, lambda b, t: (0, 0))],
        out_specs=out_specs,
        out_shape=out_shape,
        compiler_params=_cparams(("arbitrary", "arbitrary")),
        name="attn_qkv",
    )(x, sh, sc, cos_t, slo_t, shi_t, w_in)


def _rope_tables(pos, head_dim, rot_dim):
    half = rot_dim // 2
    inv_freq = jnp.exp(jnp.arange(half, dtype=F32) * (-2.0 * math.log(ROPE_THETA) / rot_dim))
    ang = pos.astype(F32)[:, None] * inv_freq[None, :]
    cos, sin = jnp.cos(ang), jnp.sin(ang)
    n = pos.shape[0]
    ones = jnp.ones((n, head_dim - rot_dim), F32)
    zeros = jnp.zeros((n, head_dim - rot_dim), F32)
    zh = jnp.zeros((n, half), F32)
    cos_h = jnp.concatenate([cos, cos, ones], axis=1)
    slo_h = jnp.concatenate([-sin, zh, zeros], axis=1)
    shi_h = jnp.concatenate([zh, sin, zeros], axis=1)
    rep = LANES // head_dim
    return (jnp.tile(cos_h, (1, rep)), jnp.tile(slo_h, (1, rep)), jnp.tile(shi_h, (1, rep)))


def _diff_lambda(lq1_ref, lk1_ref, lq2_ref, lk2_ref, lam_init):
    s1 = jnp.sum(lq1_ref[...] * lk1_ref[...], axis=-1, keepdims=True)
    s2 = jnp.sum(lq2_ref[...] * lk2_ref[...], axis=-1, keepdims=True)
    return jnp.exp(s1) - jnp.exp(s2) + lam_init


def _flash_body(q_ref, k_ref, vt_ref, lq1_ref, lk1_ref, lq2_ref, lk2_ref, o_ref,
                qq_s, sa_s, sb_s, m_s, acc_s, *, tq, head_dim, nh, lam_init):
    qi = pl.program_id(2)
    vd = vt_ref.shape[2]
    lane = lax.broadcasted_iota(jnp.int32, (tq, LANES), 1)
    for h in range(nh):
        q = q_ref[:, h * LANES:(h + 1) * LANES]
        zero = jnp.zeros_like(q)
        qq_s[h, 0:tq, :] = jnp.where(lane < head_dim, q, zero)
        qq_s[h, tq:2 * tq, :] = jnp.where(lane >= head_dim, q, zero)
    m_s[...] = jnp.full(m_s.shape, -jnp.inf, F32)
    acc_s[...] = jnp.zeros(acc_s.shape, F32)
    ones = jnp.ones((2 * SUBLANES, tq), BF16)

    def scores(j, s_ref):
        r0 = pl.multiple_of(j * tq, tq)
        for h in range(nh):
            s_ref[h] = _dot_nt(k_ref[pl.ds(r0, tq), h * LANES:(h + 1) * LANES], qq_s[h])

    def consume(j, s_ref, masked):
        for h in range(nh):
            st = s_ref[h]
            if masked:
                key = lax.broadcasted_iota(jnp.int32, st.shape, 0)
                row = lax.broadcasted_iota(jnp.int32, st.shape, 1)
                row = jnp.where(row >= tq, row - tq, row)
                st = jnp.where(key <= row, st, -jnp.inf)
            m_prev = m_s[h]
            m_new = jnp.maximum(m_prev, jnp.max(st, axis=0, keepdims=True))
            alpha = jnp.exp(m_prev - m_new)
            pt = jnp.exp(st - m_new).astype(BF16)
            v1 = jnp.concatenate([vt_ref[h, j], ones], axis=0)
            acc_s[h] = alpha * acc_s[h] + _dot(v1, pt)
            m_s[h] = m_new

    scores(0, sa_s)

    def pair(i, carry):
        j = 2 * i
        scores(j + 1, sb_s)
        consume(j, sa_s, False)
        scores(j + 2, sa_s)
        consume(j + 1, sb_s, False)
        return carry

    lax.fori_loop(0, lax.shift_right_logical(qi, 1), pair, 0)

    @pl.when((qi & 1) == 0)
    def _():
        consume(qi, sa_s, True)

    @pl.when((qi & 1) == 1)
    def _():
        scores(qi, sb_s)
        consume(qi - 1, sa_s, False)
        consume(qi, sb_s, True)

    lam = _diff_lambda(lq1_ref, lk1_ref, lq2_ref, lk2_ref, lam_init)
    for h in range(nh):
        acc = acc_s[h]
        ot = acc[0:vd, :] / acc[vd:vd + 1, :]
        o_ref[:, h * LANES:(h + 1) * LANES] = (ot[:, 0:tq] - lam * ot[:, tq:2 * tq]).T


def _flash_call(qb, kb, vt, lq1, lk1, lq2, lk2, *, tq, head_dim, nh, lam_init):
    bsz, t_rows, d = qb.shape
    _, n_heads, n_blk, vd, tk = vt.shape
    assert tk == tq and n_blk * tk == t_rows and vd == LANES
    body = functools.partial(_flash_body, tq=tq, head_dim=head_dim, nh=nh, lam_init=lam_init)
    q_spec = pl.BlockSpec((None, tq, nh * LANES), lambda b, h, i: (b, i, h))
    k_spec = pl.BlockSpec((None, t_rows, nh * LANES), lambda b, h, i: (b, 0, h))
    vt_spec = pl.BlockSpec((None, nh, n_blk, vd, tk), lambda b, h, i: (b, h, 0, 0, 0))
    l_spec = pl.BlockSpec((1, head_dim), lambda b, h, i: (0, 0))
    return pl.pallas_call(
        body,
        grid=(bsz, n_heads // nh, t_rows // tq),
        in_specs=[q_spec, k_spec, vt_spec, l_spec, l_spec, l_spec, l_spec],
        out_specs=q_spec,
        out_shape=jax.ShapeDtypeStruct((bsz, t_rows, d), F32),
        scratch_shapes=[pltpu.VMEM((nh, 2 * tq, LANES), BF16),
                        pltpu.VMEM((nh, tq, 2 * tq), F32),
                        pltpu.VMEM((nh, tq, 2 * tq), F32),
                        pltpu.VMEM((nh, 1, 2 * tq), F32),
                        pltpu.VMEM((nh, vd + 2 * SUBLANES, 2 * tq), F32)],
        compiler_params=_cparams(("arbitrary", "arbitrary", "arbitrary")),
        name="attn_flash",
    )(qb, kb, vt, lq1, lk1, lq2, lk2)


def _decode_body(pt_ref, qx_ref, *refs, n_steps, gp, t_new, n_heads, lam_init):
    k_refs, v_refs = refs[:gp], refs[gp:2 * gp]
    (kn_ref, vn_ref, lq1_ref, lk1_ref, lq2_ref, lk2_ref, o_ref,
     m_s, l_s, acc_s, kpad, vpad) = refs[2 * gp:]
    b = pl.program_id(0)
    p = pl.program_id(1)
    page = k_refs[0].shape[1]
    grp = 2 * t_new

    @pl.when((b == 0) & (p == 0))
    def _():
        kpad[...] = jnp.zeros(kpad.shape, F32)
        vpad[...] = jnp.zeros(vpad.shape, F32)

    @pl.when(p == 0)
    def _():
        m_s[...] = jnp.full(m_s.shape, -jnp.inf, F32)
        l_s[...] = jnp.zeros(l_s.shape, F32)
        acc_s[...] = jnp.zeros(acc_s.shape, F32)

    def update(sc, vs):
        m_prev = m_s[...]
        m_new = jnp.maximum(m_prev, jnp.max(sc, axis=1, keepdims=True))
        alpha = jnp.exp(m_prev - m_new)
        pe = jnp.exp(sc - m_new)
        l_s[...] = alpha * l_s[...] + jnp.sum(pe, axis=1, keepdims=True)
        m_s[...] = m_new
        pb = pe.astype(BF16)
        for hv in range(n_heads):
            rs = slice(hv * grp, (hv + 1) * grp)
            vh = jnp.concatenate(
                [v[pl.ds(hv, page, stride=n_heads), :].astype(BF16) for v in vs], axis=0)
            acc_s[rs, :] = alpha[rs, :] * acc_s[rs, :] + _dot(pb[rs, :], vh)

    kt = jnp.concatenate([k[...].astype(BF16) for k in k_refs], axis=1)
    update(_dot(qx_ref[...], kt), v_refs)

    @pl.when(p == n_steps - 1)
    def _():
        kpad[0:t_new, :] = kn_ref[...]
        vpad[0:t_new * n_heads, :] = vn_ref[...]
        sc = _dot_nt(qx_ref[...], kpad[...].astype(BF16))
        row = lax.broadcasted_iota(jnp.int32, sc.shape, 0)
        col = lax.broadcasted_iota(jnp.int32, sc.shape, 1)
        update(jnp.where(col <= row % t_new, sc, -jnp.inf), [vpad])
        o = acc_s[...] / l_s[...]
        lam = _diff_lambda(lq1_ref, lk1_ref, lq2_ref, lk2_ref, lam_init)
        vd = o.shape[1]
        for hv in range(n_heads):
            o1 = o[hv * grp:hv * grp + t_new, :]
            o2 = o[hv * grp + t_new:(hv + 1) * grp, :]
            o_ref[:, hv * vd:(hv + 1) * vd] = o1 - lam * o2


def _decode_call(page_table, qx, cache_kt, cache_vr, layer, k_new, v_new, lq1, lk1, lq2, lk2, *,
                 gp, n_heads, lam_init):
    bsz, n_pages = page_table.shape
    _, _, kd, page = cache_kt.shape
    vd = cache_vr.shape[-1]
    rows = qx.shape[1]
    t_new = k_new.shape[1]
    head_dim = lq1.shape[-1]
    n_steps = n_pages // gp
    body = functools.partial(_decode_body, n_steps=n_steps, gp=gp, t_new=t_new, n_heads=n_heads,
                             lam_init=lam_init)
    l_spec = pl.BlockSpec((1, head_dim), lambda b, p, pt: (0, 0))

    def page_spec(shape, g):
        return pl.BlockSpec((None, None) + shape,
                            lambda b, p, pt: (layer, pt[b * n_pages + p * gp + g], 0, 0))

    grid_spec = pltpu.PrefetchScalarGridSpec(
        num_scalar_prefetch=1,
        grid=(bsz, n_steps),
        in_specs=([pl.BlockSpec((None, rows, kd), lambda b, p, pt: (b, 0, 0))]
                  + [page_spec((kd, page), g) for g in range(gp)]
                  + [page_spec((page * n_heads, vd), g) for g in range(gp)]
                  + [pl.BlockSpec((None, t_new, kd), lambda b, p, pt: (b, 0, 0)),
                     pl.BlockSpec((None, t_new * n_heads, vd), lambda b, p, pt: (b, 0, 0)),
                     l_spec, l_spec, l_spec, l_spec]),
        out_specs=pl.BlockSpec((None, t_new, n_heads * vd), lambda b, p, pt: (b, 0, 0)),
        scratch_shapes=[pltpu.VMEM((rows, 1), F32), pltpu.VMEM((rows, 1), F32),
                        pltpu.VMEM((rows, vd), F32),
                        pltpu.VMEM((page, kd), F32),
                        pltpu.VMEM((page * n_heads, vd), F32)],
    )
    return pl.pallas_call(
        body,
        grid_spec=grid_spec,
        out_shape=jax.ShapeDtypeStruct((bsz, t_new, n_heads * vd), F32),
        compiler_params=_cparams(("arbitrary", "arbitrary")),
        name="attn_decode",
    )(page_table.reshape(-1), qx, *([cache_kt] * gp), *([cache_vr] * gp), k_new, v_new,
      lq1, lk1, lq2, lk2)


def _attn_out_body(x_ref, gt_ref, o_ref, z_ref, sg_ref, wout_ref, g_ref, b_ref, y_ref, gbuf, *,
                   alpha, out_scale):
    d = x_ref.shape[-1]
    vd = sg_ref.shape[-1]
    for hv in range(d // vd):
        cols = slice(hv * vd, (hv + 1) * vd)
        of = o_ref[:, cols]
        of = of * lax.rsqrt(jnp.mean(of * of, axis=-1, keepdims=True) + SUBLN_EPS)
        of = of * sg_ref[...] * out_scale
        gbuf[:, cols] = (of * _silu(z_ref[:, cols])).astype(BF16)
    out = _dot(gbuf[...], wout_ref[...])
    res = alpha * x_ref[...] + gt_ref[...] * out
    y_ref[...] = _layer_norm(res, g_ref[...], b_ref[...])


def _attn_out_call(x, gt, o, z, subln_g, w_out, ln_g, ln_b, *, tm, alpha, out_scale):
    bsz, t_rows, d = x.shape
    r = gt.shape[1]
    vd = subln_g.shape[-1]
    body = functools.partial(_attn_out_body, alpha=alpha, out_scale=out_scale)
    row_spec = pl.BlockSpec((None, tm, d), lambda b, t: (b, t, 0))
    mod_spec = pl.BlockSpec((None, r, d), lambda b, t: (b, 0, 0))
    const = lambda shape: pl.BlockSpec(shape, lambda b, t: (0,) * len(shape))
    return pl.pallas_call(
        body,
        grid=(bsz, t_rows // tm),
        in_specs=[row_spec, mod_spec, row_spec, row_spec, const((1, vd)), const(w_out.shape),
                  const((1, d)), const((1, d))],
        out_specs=row_spec,
        out_shape=jax.ShapeDtypeStruct((bsz, t_rows, d), F32),
        scratch_shapes=[pltpu.VMEM((tm, d), BF16)],
        compiler_params=_cparams(("arbitrary", "arbitrary")),
        name="attn_out",
    )(x, gt, o, z, subln_g.reshape(1, vd), w_out, ln_g.reshape(1, d), ln_b.reshape(1, d))


def kernel(x_prompt, x_sample, state_conv_a, state_lru_h, state_lru_conv, cache_k, cache_v, page_table, c_prompt, c_sample, w_ada, b_ada, ln_g, ln_b, a_w_in, a_conv_w, a_w_out, r_w_in, r_conv_w, r_conv_b, r_w_ga, r_b_ga, r_w_gx, r_b_gx, r_lru_param, r_w_out, d_w_in, d_lq1, d_lk1, d_lq2, d_lk2, d_subln_g, d_w_out):
    bp, tp, d = x_prompt.shape
    bs, ts, _ = x_sample.shape
    depth = w_ada.shape[0]
    n_pages = page_table.shape[1]
    page = cache_k.shape[2]
    n_sub, head_dim = cache_k.shape[3], cache_k.shape[4]
    n_heads, vd = cache_v.shape[3], cache_v.shape[4]
    past_len = n_pages * page
    rot_dim = head_dim // 4
    alpha = (2 * depth) ** 0.25
    rows_s = bs * ts

    n_c = _round_up(bp + bs, SUBLANES)
    c_all = jnp.concatenate([c_prompt, c_sample, jnp.zeros((n_c - bp - bs, d), F32)], axis=0)
    ada = _ada_call(c_all, w_ada, b_ada)

    def mods(i):
        m = ada[i]
        mp = [m[:bp, k * d:(k + 1) * d].reshape(bp, 1, d) for k in range(3)]
        ms = [jnp.tile(m[bp:bp + bs, k * d:(k + 1) * d], (ts, 1)).reshape(1, rows_s, d)
              for k in range(3)]
        return mp, ms

    def to_time_major(a):
        return jnp.swapaxes(a, 0, 1).reshape((1, a.shape[1] * bs) + a.shape[2:])

    def from_time_major(a, n):
        return jnp.swapaxes(a.reshape(n, bs, a.shape[-1]), 0, 1)

    xp = x_prompt
    xs = to_time_major(x_sample)
    bf = lambda w: w.astype(BF16)

    conv_p, conv_s, lruh_p, lruh_s, lruc_p, lruc_s = [], [], [], [], [], []
    k_p, v_p, k_s, v_s = [], [], [], []
    for i in range(depth):
        kind, j = i % N_MIXERS, i // N_MIXERS
        (shp, scp, gtp), (shs, scs, gts) = mods(i)
        if kind == 0:
            w_in, w_out = bf(a_w_in[j]), bf(a_w_out[j])
            width = a_conv_w.shape[1]
            args = (w_in, a_conv_w[j], w_out, ln_g[i], ln_b[i])
            xp, nbp = _conv_layer(xp, shp, scp, gtp, jnp.zeros((bp, width - 1, d), F32), *args,
                                  s=1, tm=512, alpha=alpha)
            xs, nbs = _conv_layer(xs, shs, scs, gts, to_time_major(state_conv_a[j]), *args,
                                  s=bs, tm=rows_s, alpha=alpha)
            conv_p.append(nbp)
            conv_s.append(from_time_major(nbs, width - 1))
        elif kind == 1:
            width = r_conv_w.shape[1]
            args = (bf(r_w_in[j]), r_conv_w[j], r_conv_b[j], bf(r_w_ga[j]), r_b_ga[j],
                    bf(r_w_gx[j]), r_b_gx[j], r_lru_param[j], bf(r_w_out[j]), ln_g[i], ln_b[i])
            xp, nbp, hp = _lru_layer(xp, shp, scp, gtp, jnp.zeros((bp, width - 1, d), F32),
                                     jnp.zeros((bp, 1, d), F32), *args, s=1, tm=256, alpha=alpha)
            xs, nbs, hs = _lru_layer(xs, shs, scs, gts, to_time_major(state_lru_conv[j]),
                                     state_lru_h[j].reshape(1, bs, d), *args,
                                     s=bs, tm=rows_s, alpha=alpha)
            lruh_p.append(hp.reshape(bp, d))
            lruh_s.append(hs.reshape(bs, d))
            lruc_p.append(nbp)
            lruc_s.append(from_time_major(nbs, width - 1))
        else:
            lam_init = 0.8 - 0.6 * math.exp(-0.3 * i)
            w_in, w_out = bf(d_w_in[j]), bf(d_w_out[j])
            half = rot_dim // 2
            qscale = head_dim ** -0.5
            lvec = [v[j].reshape(1, head_dim) for v in (d_lq1, d_lk1, d_lq2, d_lk2)]
            tabs_p = _rope_tables(jnp.arange(tp), head_dim, rot_dim)
            kp, vp, zp, qb, kb, vt = _qkv_call(xp, shp, scp, *tabs_p, w_in, tm=256, half=half,
                                               qscale=qscale, attn_vd=vd)
            op = _flash_call(qb, kb, vt, *lvec, tq=256, head_dim=head_dim, nh=2,
                             lam_init=lam_init)
            xp = _attn_out_call(xp, gtp, op, zp, d_subln_g[j], w_out, ln_g[i], ln_b[i], tm=512,
                                alpha=alpha, out_scale=1.0 - lam_init)
            k_p.append(kp.reshape(bp, tp, n_sub, head_dim))
            v_p.append(vp.reshape(bp, tp, n_heads, vd))
            pos_s = past_len + jnp.repeat(jnp.arange(ts), bs)
            tabs_s = _rope_tables(pos_s, head_dim, rot_dim)
            ks_, vs_, zs, qsb = _qkv_call(xs, shs, scs, *tabs_s, w_in, tm=rows_s, half=half,
                                          qscale=qscale)
            ksn = from_time_major(ks_, ts)
            vsn = from_time_major(vs_, ts)
            q4 = from_time_major(qsb, ts).reshape(bs, 1, ts, n_sub, head_dim)
            eye = jnp.eye(n_sub, dtype=BF16).reshape(1, n_sub, 1, n_sub, 1)
            qx = (eye * q4).reshape(bs, n_sub * ts, n_sub * head_dim)
            n_layers, n_phys = cache_k.shape[0], cache_k.shape[1]
            cache_kt = jnp.transpose(cache_k, (0, 1, 3, 4, 2)).reshape(
                n_layers, n_phys, n_sub * head_dim, page)
            cache_vr = cache_v.reshape(n_layers, n_phys, page * n_heads, vd)
            os_ = _decode_call(page_table, qx, cache_kt, cache_vr, j, ksn,
                               vsn.reshape(bs, ts * n_heads, vd), *lvec,
                               gp=16, n_heads=n_heads, lam_init=lam_init)
            xs = _attn_out_call(xs, gts, to_time_major(os_), zs, d_subln_g[j], w_out, ln_g[i],
                                ln_b[i], tm=rows_s, alpha=alpha, out_scale=1.0 - lam_init)
            k_s.append(ksn.reshape(bs, ts, n_sub, head_dim))
            v_s.append(vsn.reshape(bs, ts, n_heads, vd))

    return (xp, from_time_major(xs, ts),
            jnp.stack(conv_p), jnp.stack(conv_s),
            jnp.stack(lruh_p), jnp.stack(lruh_s),
            jnp.stack(lruc_p), jnp.stack(lruc_s),
            jnp.stack(k_p), jnp.stack(v_p), jnp.stack(k_s), jnp.stack(v_s))
```

```python
import functools
import math

import jax
import jax.numpy as jnp
import numpy as np
from jax import lax
from jax.experimental import pallas as pl
from jax.experimental.pallas import tpu as pltpu

F32 = jnp.float32
BF16 = jnp.bfloat16

LN_EPS = 1e-5
SUBLN_EPS = 1e-5
LRU_C = 8.0
ROPE_THETA = 500000.0
N_MIXERS = 3

SUBLANES = 8
LANES = 128
VMEM_LIMIT_BYTES = 56 * 1024 * 1024


def _cparams(semantics):
    return pltpu.CompilerParams(dimension_semantics=semantics, vmem_limit_bytes=VMEM_LIMIT_BYTES)


def _dot(a, b):
    return jnp.dot(a, b, preferred_element_type=F32)


def _dot_nt(a, b):
    return lax.dot_general(a, b, (((1,), (1,)), ((), ())), preferred_element_type=F32)


def _silu(z):
    return z * jax.nn.sigmoid(z)


def _layer_norm(y, g, b):
    mu = jnp.mean(y, axis=-1, keepdims=True)
    yc = y - mu
    var = jnp.mean(yc * yc, axis=-1, keepdims=True)
    return yc * lax.rsqrt(var + LN_EPS) * g + b


def _round_up(n, m):
    return (n + m - 1) // m * m


def _ada_body(c_ref, w_ref, b_ref, op_ref, os_ref, *, bp, bs, ts):
    c = c_ref[...]
    a = _silu(c).astype(BF16)
    m = _dot(a, w_ref[...].astype(BF16)) + b_ref[...]
    for r in range(bp):
        op_ref[r] = m[bs + r:bs + r + 1, :]
    for t in range(ts):
        os_ref[t * bs:(t + 1) * bs, :] = m[0:bs, :]


def _ada_call(c_all, w_ada, b_ada, *, bp, bs, ts):
    depth, d, d3 = w_ada.shape
    rows = c_all.shape[0]
    nt = d3 // d
    return pl.pallas_call(
        functools.partial(_ada_body, bp=bp, bs=bs, ts=ts),
        grid=(depth, nt),
        in_specs=[
            pl.BlockSpec((rows, d), lambda i, n: (0, 0)),
            pl.BlockSpec((None, d, d), lambda i, n: (i, 0, n)),
            pl.BlockSpec((None, 1, d), lambda i, n: (i, 0, n)),
        ],
        out_specs=[pl.BlockSpec((None, None, bp, 1, d), lambda i, n: (i, n, 0, 0, 0)),
                   pl.BlockSpec((None, None, None, ts * bs, d), lambda i, n: (i, n, 0, 0, 0))],
        out_shape=[jax.ShapeDtypeStruct((depth, nt, bp, 1, d), F32),
                   jax.ShapeDtypeStruct((depth, nt, 1, ts * bs, d), F32)],
        compiler_params=_cparams(("arbitrary", "arbitrary")),
        name="adaln",
    )(c_all, w_ada, b_ada.reshape(depth, 1, d3))


def _mod_spec(mod, layer, k):
    _, _, _, r, d = mod.shape
    return pl.BlockSpec((None, None, None, r, d), lambda b, t, *_: (layer, k, b, 0, 0))


def _load_history(t, buf, st0_ref, pad, ks, tm):
    @pl.when(t == 0)
    def _():
        buf[pad - ks:pad, :] = st0_ref[...]

    @pl.when(t > 0)
    def _():
        buf[pad - ks:pad, :] = buf[pad + tm - ks:pad + tm, :]


def _conv_taps(buf, cw_ref, cols, width, s, pad, tm):
    y = None
    if s % SUBLANES == 0:
        for k in range(width):
            r0 = pad - (width - 1 - k) * s
            term = cw_ref[k:k + 1, cols] * buf[r0:r0 + tm, cols]
            y = term if y is None else y + term
        return y
    full = buf[0:pad + tm, cols]
    for k in range(width):
        back = (width - 1 - k) * s
        src = full if back == 0 else pltpu.roll(full, back, axis=0)
        term = cw_ref[k:k + 1, cols] * src[pad:pad + tm, :]
        y = term if y is None else y + term
    return y


def _conv_body(x_ref, sh_ref, sc_ref, gt_ref, st0_ref, win_ref, cw_ref, wout_ref, g_ref, b_ref,
               y_ref, st_ref, abuf, acc, *, s, tm, cw, width, pad, alpha, extra=None):
    d = x_ref.shape[-1]
    ks = (width - 1) * s
    t = pl.program_id(1)
    _load_history(t, abuf, st0_ref, pad, ks, tm)
    if extra is not None:
        extra()
    x = x_ref[...]
    u = (x * (1.0 + sc_ref[...]) + sh_ref[...]).astype(BF16)
    for c in range(d // cw):
        cols = slice(c * cw, (c + 1) * cw)
        h = _dot(u, win_ref[:, c * cw:(c + 1) * cw])
        cg = _dot(u, win_ref[:, 2 * d + c * cw:2 * d + (c + 1) * cw])
        abuf[pad:pad + tm, cols] = cg * h
        y = _conv_taps(abuf, cw_ref, cols, width, s, pad, tm)
        bg = _dot(u, win_ref[:, d + c * cw:d + (c + 1) * cw])
        z = _dot(u, win_ref[:, 3 * d + c * cw:3 * d + (c + 1) * cw])
        gated = (_silu(z) * bg * y).astype(BF16)
        part = _dot(gated, wout_ref[cols, :])
        if c == 0:
            acc[...] = part
        else:
            acc[...] += part
    st_ref[...] = abuf[pad + tm - ks:pad + tm, :]
    res = alpha * x + gt_ref[...] * acc[...]
    y_ref[...] = _layer_norm(res, g_ref[...], b_ref[...])


def _conv_layer(x, mod, layer, st0, w_in, conv_w, w_out, ln_g, ln_b, *, s, tm, alpha, side=None):
    bsz, t_rows, d = x.shape
    width = conv_w.shape[0]
    ks = (width - 1) * s
    pad = _round_up(ks, SUBLANES)
    cw = 256
    body = functools.partial(_conv_body, s=s, tm=tm, cw=cw, width=width, pad=pad, alpha=alpha)
    row_spec = pl.BlockSpec((None, tm, d), lambda b, t, *_: (b, t, 0))
    st_spec = pl.BlockSpec((None, ks, d), lambda b, t, *_: (b, 0, 0))
    const = lambda shape: pl.BlockSpec(shape, lambda b, t, *_: (0,) * len(shape))
    (y, st), o_side = _run(
        body, (bsz, t_rows // tm),
        [row_spec, _mod_spec(mod, layer, 0), _mod_spec(mod, layer, 1),
         _mod_spec(mod, layer, 2), st_spec,
         const(w_in.shape), const(conv_w.shape), const(w_out.shape),
         const((1, d)), const((1, d))],
        [row_spec, st_spec],
        [jax.ShapeDtypeStruct((bsz, t_rows, d), F32), jax.ShapeDtypeStruct((bsz, ks, d), F32)],
        [pltpu.VMEM((pad + tm, d), F32), pltpu.VMEM((tm, d), F32)],
        (x, mod, mod, mod, st0, w_in, conv_w, w_out, ln_g.reshape(1, d), ln_b.reshape(1, d)),
        "conv_layer", side)
    return y, st, o_side


def _lru_body(x_ref, sh_ref, sc_ref, gt_ref, st0_ref, h0_ref, win_ref, cw_ref, cb_ref,
              wga_ref, bga_ref, wgx_ref, bgx_ref, prm_ref, wout_ref, g_ref, b_ref,
              y_ref, st_ref, hl_ref, xbuf, a_s, b_s, h_s, hc, *, s, tm, width, pad, alpha,
              extra=None):
    d = x_ref.shape[-1]
    nblk, blk, _ = wga_ref.shape
    ks = (width - 1) * s
    t = pl.program_id(1)
    _load_history(t, xbuf, st0_ref, pad, ks, tm)

    @pl.when(t == 0)
    def _():
        hc[...] = h0_ref[...]

    if extra is not None:
        extra()
    x = x_ref[...]
    u = (x * (1.0 + sc_ref[...]) + sh_ref[...]).astype(BF16)
    xbuf[pad:pad + tm, :] = _dot(u, win_ref[:, 0:d])
    st_ref[...] = xbuf[pad + tm - ks:pad + tm, :]
    for n in range(nblk):
        cols = slice(n * blk, (n + 1) * blk)
        xc = _conv_taps(xbuf, cw_ref, cols, width, s, pad, tm) + cb_ref[:, cols]
        xcb = xc.astype(BF16)
        r = jax.nn.sigmoid(_dot(xcb, wga_ref[n]) + bga_ref[:, cols])
        gi = jax.nn.sigmoid(_dot(xcb, wgx_ref[n]) + bgx_ref[:, cols])
        log_a = LRU_C * r * jax.nn.log_sigmoid(prm_ref[:, cols])
        a = jnp.exp(log_a)
        a_s[:, cols] = a
        b_s[:, cols] = jnp.sqrt(-jnp.tanh(log_a) * (a * a + 1.0)) * (gi * xc)

    steps = tm // s

    def step(i, h):
        r0 = pl.multiple_of(i * s, s)
        h = a_s[pl.ds(r0, s), :] * h + b_s[pl.ds(r0, s), :]
        h_s[pl.ds(r0, s), :] = h
        return h

    h_last = lax.fori_loop(0, steps, step, hc[...], unroll=True)
    hc[...] = h_last
    hl_ref[...] = h_last

    z = _dot(u, win_ref[:, d:2 * d])
    yy = (h_s[...] * _silu(z)).astype(BF16)
    out = _dot(yy, wout_ref[...])
    res = alpha * x + gt_ref[...] * out
    y_ref[...] = _layer_norm(res, g_ref[...], b_ref[...])


def _lru_layer(x, mod, layer, st0, h0, w_in, conv_w, conv_b, w_ga, b_ga, w_gx, b_gx, prm, w_out,
               ln_g, ln_b, *, s, tm, alpha, side=None):
    bsz, t_rows, d = x.shape
    width = conv_w.shape[0]
    ks = (width - 1) * s
    pad = _round_up(ks, SUBLANES)
    body = functools.partial(_lru_body, s=s, tm=tm, width=width, pad=pad, alpha=alpha)
    row_spec = pl.BlockSpec((None, tm, d), lambda b, t, *_: (b, t, 0))
    st_spec = pl.BlockSpec((None, ks, d), lambda b, t, *_: (b, 0, 0))
    h_spec = pl.BlockSpec((None, s, d), lambda b, t, *_: (b, 0, 0))
    const = lambda shape: pl.BlockSpec(shape, lambda b, t, *_: (0,) * len(shape))
    vec = const((1, d))
    (y, st, hl), o_side = _run(
        body, (bsz, t_rows // tm),
        [row_spec, _mod_spec(mod, layer, 0), _mod_spec(mod, layer, 1),
         _mod_spec(mod, layer, 2), st_spec, h_spec,
         const(w_in.shape), const(conv_w.shape), vec,
         const(w_ga.shape), vec, const(w_gx.shape), vec, vec,
         const(w_out.shape), vec, vec],
        [row_spec, st_spec, h_spec],
        [jax.ShapeDtypeStruct((bsz, t_rows, d), F32), jax.ShapeDtypeStruct((bsz, ks, d), F32),
         jax.ShapeDtypeStruct((bsz, s, d), F32)],
        [pltpu.VMEM((pad + tm, d), F32), pltpu.VMEM((tm, d), F32), pltpu.VMEM((tm, d), F32),
         pltpu.VMEM((tm, d), F32), pltpu.VMEM((s, d), F32)],
        (x, mod, mod, mod, st0, h0, w_in, conv_w, conv_b.reshape(1, d), w_ga,
         b_ga.reshape(1, d), w_gx, b_gx.reshape(1, d), prm.reshape(1, d), w_out,
         ln_g.reshape(1, d), ln_b.reshape(1, d)),
        "lru_layer", side)
    return y, st, hl, o_side


def _rope_block(xb, cos, sin_lo, sin_hi, half):
    return (xb * cos + pltpu.roll(xb, LANES - half, axis=1) * sin_lo
            + pltpu.roll(xb, half, axis=1) * sin_hi)


def _qkv_body(x_ref, sh_ref, sc_ref, cos_ref, slo_ref, shi_ref, win_ref,
              k_ref, v_ref, z_ref, qb_ref, *attn_refs, half, qscale):
    d = x_ref.shape[-1]
    x = x_ref[...]
    u = (x * (1.0 + sc_ref[...]) + sh_ref[...]).astype(BF16)
    cos, slo, shi = cos_ref[...], slo_ref[...], shi_ref[...]
    q = _dot(u, win_ref[:, 0:d])
    for j in range(d // LANES):
        cols = slice(j * LANES, (j + 1) * LANES)
        qb_ref[:, cols] = (_rope_block(q[:, cols], cos, slo, shi, half) * qscale).astype(BF16)
    k = _dot(u, win_ref[:, d:2 * d])
    for j in range(d // LANES):
        cols = slice(j * LANES, (j + 1) * LANES)
        kr = _rope_block(k[:, cols], cos, slo, shi, half)
        k_ref[:, cols] = kr
        if attn_refs:
            attn_refs[0][:, cols] = kr.astype(BF16)
    v = _dot(u, win_ref[:, 2 * d:3 * d])
    v_ref[...] = v
    if attn_refs:
        vt_ref = attn_refs[1]
        vd = vt_ref.shape[1]
        for hv in range(vt_ref.shape[0]):
            vt_ref[hv] = v[:, hv * vd:(hv + 1) * vd].T.astype(BF16)
    z_ref[...] = _dot(u, win_ref[:, 3 * d:4 * d]).astype(z_ref.dtype)


def _qkv_call(x, mod, layer, cos_t, slo_t, shi_t, w_in, *, tm, half, qscale, attn_vd=None):
    bsz, t_rows, d = x.shape
    body = functools.partial(_qkv_body, half=half, qscale=qscale)
    row_spec = pl.BlockSpec((None, tm, d), lambda b, t: (b, t, 0))
    tab_spec = pl.BlockSpec((tm, LANES), lambda b, t: (t, 0))
    f32_out = jax.ShapeDtypeStruct((bsz, t_rows, d), F32)
    bf_out = jax.ShapeDtypeStruct((bsz, t_rows, d), BF16)
    out_specs = [row_spec] * 4
    out_shape = [f32_out, f32_out, bf_out, bf_out]
    if attn_vd is not None:
        n_heads = d // attn_vd
        out_specs += [row_spec, pl.BlockSpec((None, n_heads, None, attn_vd, tm),
                                             lambda b, t: (b, 0, t, 0, 0))]
        out_shape += [bf_out,
                      jax.ShapeDtypeStruct((bsz, n_heads, t_rows // tm, attn_vd, tm), BF16)]
    return pl.pallas_call(
        body,
        grid=(bsz, t_rows // tm),
        in_specs=[row_spec, _mod_spec(mod, layer, 0), _mod_spec(mod, layer, 1),
                  tab_spec, tab_spec, tab_spec,
                  pl.BlockSpec(w_in.shape, lambda b, t: (0, 0))],
        out_specs=out_specs,
        out_shape=out_shape,
        compiler_params=_cparams(("arbitrary", "arbitrary")),
        name="attn_qkv",
    )(x, mod, mod, cos_t, slo_t, shi_t, w_in)


def _rope_tables(pos, head_dim, rot_dim):
    half = rot_dim // 2
    inv_freq = np.exp(np.arange(half, dtype=np.float64) * (-2.0 * math.log(ROPE_THETA) / rot_dim))
    ang = np.asarray(pos, dtype=np.float64)[:, None] * inv_freq[None, :]
    cos, sin = np.cos(ang), np.sin(ang)
    n = ang.shape[0]
    ones = np.ones((n, head_dim - rot_dim))
    zeros = np.zeros((n, head_dim - rot_dim))
    zh = np.zeros((n, half))
    cos_h = np.concatenate([cos, cos, ones], axis=1)
    slo_h = np.concatenate([-sin, zh, zeros], axis=1)
    shi_h = np.concatenate([zh, sin, zeros], axis=1)
    rep = LANES // head_dim
    return tuple(jnp.asarray(np.tile(t, (1, rep)), dtype=F32) for t in (cos_h, slo_h, shi_h))


def _diff_lambda(lq1_ref, lk1_ref, lq2_ref, lk2_ref, lam_init):
    s1 = jnp.sum(lq1_ref[...] * lk1_ref[...], axis=-1, keepdims=True)
    s2 = jnp.sum(lq2_ref[...] * lk2_ref[...], axis=-1, keepdims=True)
    return jnp.exp(s1) - jnp.exp(s2) + lam_init


def _flash_body(q_ref, k_ref, vt_ref, lq1_ref, lk1_ref, lq2_ref, lk2_ref, o_ref,
                qq_s, sa_s, sb_s, m_s, acc_s, *, tq, head_dim, nh, lam_init, extra=None):
    qi = pl.program_id(2)
    vd, tk = vt_ref.shape[2], vt_ref.shape[3]
    nd = tq // tk
    lane = lax.broadcasted_iota(jnp.int32, (tq, LANES), 1)
    for h in range(nh):
        q = q_ref[:, h * LANES:(h + 1) * LANES]
        zero = jnp.zeros_like(q)
        qq_s[h, 0:tq, :] = jnp.where(lane < head_dim, q, zero)
        qq_s[h, tq:2 * tq, :] = jnp.where(lane >= head_dim, q, zero)
    m_s[...] = jnp.full(m_s.shape, -jnp.inf, F32)
    acc_s[...] = jnp.zeros(acc_s.shape, F32)
    ones = jnp.ones((2 * SUBLANES, tk), BF16)

    def scores(j, s_ref):
        r0 = pl.multiple_of(j * tk, tk)
        for h in range(nh):
            s_ref[h] = _dot_nt(k_ref[pl.ds(r0, tk), h * LANES:(h + 1) * LANES], qq_s[h])

    def consume(j, s_ref, key0):
        for h in range(nh):
            st = s_ref[h]
            if key0 is not None:
                key = lax.broadcasted_iota(jnp.int32, st.shape, 0) + key0
                row = lax.broadcasted_iota(jnp.int32, st.shape, 1)
                row = jnp.where(row >= tq, row - tq, row)
                st = jnp.where(key <= row, st, -jnp.inf)
            m_prev = m_s[h]
            m_new = jnp.maximum(m_prev, jnp.max(st, axis=0, keepdims=True))
            alpha = jnp.exp2(m_prev - m_new)
            pt = jnp.exp2(st - m_new).astype(BF16)
            v1 = jnp.concatenate([vt_ref[h, j], ones], axis=0)
            acc_s[h] = alpha * acc_s[h] + _dot(v1, pt)
            m_s[h] = m_new

    if extra is not None:
        extra()
    scores(0, sa_s)

    def pair(i, carry):
        j = 2 * i
        scores(j + 1, sb_s)
        consume(j, sa_s, None)
        scores(j + 2, sa_s)
        consume(j + 1, sb_s, None)
        return carry

    lax.fori_loop(0, qi * (nd // 2), pair, 0)

    bufs = (sa_s, sb_s)
    for g in range(nd):
        j = nd * qi + g
        if g + 1 < nd:
            scores(j + 1, bufs[(g + 1) % 2])
        consume(j, bufs[g % 2], g * tk)

    lam = _diff_lambda(lq1_ref, lk1_ref, lq2_ref, lk2_ref, lam_init)
    for h in range(nh):
        acc = acc_s[h]
        ot = acc[0:vd, :] / acc[vd:vd + 1, :]
        o_ref[:, h * LANES:(h + 1) * LANES] = (
            (ot[:, 0:tq] - lam * ot[:, tq:2 * tq]).T.astype(o_ref.dtype))


def _flash_call(qb, kb, vt, lq1, lk1, lq2, lk2, *, tq, head_dim, nh, lam_init, side=None):
    bsz, t_rows, d = qb.shape
    _, n_heads, n_blk, vd, tk = vt.shape
    assert tq % (2 * tk) == 0 and n_blk * tk == t_rows and vd == LANES
    body = functools.partial(_flash_body, tq=tq, head_dim=head_dim, nh=nh, lam_init=lam_init)
    q_spec = pl.BlockSpec((None, tq, nh * LANES), lambda b, h, i, *_: (b, i, h))
    k_spec = pl.BlockSpec((None, t_rows, nh * LANES), lambda b, h, i, *_: (b, 0, h))
    vt_spec = pl.BlockSpec((None, nh, n_blk, vd, tk), lambda b, h, i, *_: (b, h, 0, 0, 0))
    l_spec = pl.BlockSpec((1, head_dim), lambda b, h, i, *_: (0, 0))
    (o,), o_side = _run(
        body, (bsz, n_heads // nh, t_rows // tq),
        [q_spec, k_spec, vt_spec, l_spec, l_spec, l_spec, l_spec],
        [q_spec],
        [jax.ShapeDtypeStruct((bsz, t_rows, d), BF16)],
        [pltpu.VMEM((nh, 2 * tq, LANES), BF16), pltpu.VMEM((nh, tk, 2 * tq), F32),
         pltpu.VMEM((nh, tk, 2 * tq), F32), pltpu.VMEM((nh, 1, 2 * tq), F32),
         pltpu.VMEM((nh, vd + 2 * SUBLANES, 2 * tq), F32)],
        (qb, kb, vt, lq1, lk1, lq2, lk2),
        "attn_flash", side)
    return o, o_side


def _linear_index(grid, idx):
    s = idx[0]
    for n, i in zip(grid[1:], idx[1:]):
        s = s * n + i
    return s


class _DecodeSide:
    def __init__(self, page_table, qs, cache_kt, cache_vr, layer, k_new, v_new, lvec, *, first,
                 gp, n_heads, lam_init):
        self.pt = page_table.reshape(-1)
        self.n_pages = page_table.shape[1]
        self.qs, self.kt, self.vr, self.layer = qs, cache_kt, cache_vr, layer
        self.k_new, self.v_new, self.lvec = k_new, v_new, list(lvec)
        self.first, self.gp, self.n_heads, self.lam_init = first, gp, n_heads, lam_init
        self.kd, self.page = cache_kt.shape[2], cache_kt.shape[3]
        self.vd = cache_vr.shape[-1]
        self.t_new = k_new.shape[1]
        self.head_dim = lvec[0].shape[-1]
        self.n_sub = self.kd // self.head_dim
        self.rows = self.n_sub * self.t_new
        self.spb = self.n_pages // gp

    def count(self, grid):
        total = math.prod(grid)
        assert total % self.spb == 0
        return total // self.spb

    def in_specs(self, grid):
        first, spb, gp, n_pages, layer = self.first, self.spb, self.gp, self.n_pages, self.layer

        def page_spec(shape, g):
            def imap(*a):
                s = _linear_index(grid, a[:-1])
                return (layer, a[-1][(first + s // spb) * n_pages + (s % spb) * gp + g], 0, 0)
            return pl.BlockSpec((None, None) + shape, imap)

        def per_seq(shape):
            return pl.BlockSpec((None,) + shape,
                                lambda *a: (first + _linear_index(grid, a[:-1]) // spb, 0, 0))

        l_spec = pl.BlockSpec((1, self.head_dim), lambda *a: (0, 0))
        return ([per_seq((self.t_new, self.kd))]
                + [page_spec((self.kd, self.page), g) for g in range(gp)]
                + [page_spec((self.page * self.n_heads, self.vd), g) for g in range(gp)]
                + [per_seq((self.t_new, self.kd)), per_seq((self.t_new * self.n_heads, self.vd)),
                   l_spec, l_spec, l_spec, l_spec])

    def args(self):
        return ([self.qs] + [self.kt] * self.gp + [self.vr] * self.gp
                + [self.k_new, self.v_new] + self.lvec)

    def out_spec(self, grid):
        spb = self.spb
        return pl.BlockSpec((None, self.t_new, self.n_heads * self.vd),
                            lambda *a: (_linear_index(grid, a[:-1]) // spb, 0, 0))

    def out_shape(self, grid):
        return jax.ShapeDtypeStruct((self.count(grid), self.t_new, self.n_heads * self.vd), F32)

    def scratch_shapes(self):
        return [pltpu.VMEM((self.rows, self.kd), F32), pltpu.VMEM((self.rows, self.kd), BF16),
                pltpu.VMEM((self.rows, 1), F32), pltpu.VMEM((self.rows, 1), F32),
                pltpu.VMEM((self.rows, self.vd), F32),
                pltpu.VMEM((self.page, self.kd), F32),
                pltpu.VMEM((self.page * self.n_heads, self.vd), F32)]


class _DecodeStep:
    def __init__(self, side, grid, in_refs, out_ref, scratch_refs):
        gp = side.gp
        self.c = side
        self.q_ref = in_refs[0]
        self.k_refs, self.v_refs = in_refs[1:1 + gp], in_refs[1 + gp:1 + 2 * gp]
        self.kn_ref, self.vn_ref = in_refs[1 + 2 * gp], in_refs[2 + 2 * gp]
        self.l_refs = in_refs[3 + 2 * gp:]
        self.o_ref = out_ref
        (self.qf_s, self.qx_s, self.m_s, self.l_s, self.acc_s, self.kpad, self.vpad) = scratch_refs
        self.s = _linear_index(grid, [pl.program_id(i) for i in range(len(grid))])
        self.p = lax.rem(self.s, side.spb)

    def preamble(self):
        c = self.c

        @pl.when(self.s == 0)
        def _():
            self.kpad[...] = jnp.zeros(self.kpad.shape, F32)
            self.vpad[...] = jnp.zeros(self.vpad.shape, F32)

        @pl.when(self.p == 0)
        def _():
            self.m_s[...] = jnp.full(self.m_s.shape, -jnp.inf, F32)
            self.l_s[...] = jnp.zeros(self.l_s.shape, F32)
            self.acc_s[...] = jnp.zeros(self.acc_s.shape, F32)
            q = self.q_ref[...].astype(F32)
            lane = lax.broadcasted_iota(jnp.int32, q.shape, 1)
            for h in range(c.n_sub):
                own = (lane >= h * c.head_dim) & (lane < (h + 1) * c.head_dim)
                self.qf_s[h * c.t_new:(h + 1) * c.t_new, :] = jnp.where(own, q, 0.0)
            self.qx_s[...] = self.qf_s[...].astype(BF16)

    def _update(self, sc, vs):
        c = self.c
        grp = 2 * c.t_new
        m_prev = self.m_s[...]
        m_new = jnp.maximum(m_prev, jnp.max(sc, axis=1, keepdims=True))
        alpha = jnp.exp(m_prev - m_new)
        pe = jnp.exp(sc - m_new)
        self.l_s[...] = alpha * self.l_s[...] + jnp.sum(pe, axis=1, keepdims=True)
        self.m_s[...] = m_new
        pb = pe.astype(BF16)
        for hv in range(c.n_heads):
            rs = slice(hv * grp, (hv + 1) * grp)
            vh = jnp.concatenate(
                [v[pl.ds(hv, c.page, stride=c.n_heads), :].astype(BF16) for v in vs], axis=0)
            self.acc_s[rs, :] = alpha[rs, :] * self.acc_s[rs, :] + _dot(pb[rs, :], vh)

    def main(self):
        kt = jnp.concatenate([k[...].astype(BF16) for k in self.k_refs], axis=1)
        self._update(_dot(self.qx_s[...], kt), self.v_refs)

    def finalize(self):
        c = self.c

        @pl.when(self.p == c.spb - 1)
        def _():
            self.kpad[0:c.t_new, :] = self.kn_ref[...]
            self.vpad[0:c.t_new * c.n_heads, :] = self.vn_ref[...]
            sc = _dot_nt(self.qx_s[...], self.kpad[...].astype(BF16))
            row = lax.broadcasted_iota(jnp.int32, sc.shape, 0)
            col = lax.broadcasted_iota(jnp.int32, sc.shape, 1)
            self._update(jnp.where(col <= row % c.t_new, sc, -jnp.inf), [self.vpad])
            o = self.acc_s[...] / self.l_s[...]
            lam = _diff_lambda(*self.l_refs, c.lam_init)
            grp = 2 * c.t_new
            for hv in range(c.n_heads):
                o1 = o[hv * grp:hv * grp + c.t_new, :]
                o2 = o[hv * grp + c.t_new:(hv + 1) * grp, :]
                self.o_ref[:, hv * c.vd:(hv + 1) * c.vd] = o1 - lam * o2


def _idle_body(extra):
    extra()


def _run(body, grid, in_specs, out_specs, out_shape, scratch_shapes, args, name, side=None):
    sem = ("arbitrary",) * len(grid)
    if side is None:
        outs = pl.pallas_call(body, grid=grid, in_specs=in_specs, out_specs=out_specs,
                              out_shape=out_shape, scratch_shapes=scratch_shapes,
                              compiler_params=_cparams(sem), name=name)(*args)
        return outs, None
    s_in = side.in_specs(grid)
    n_hi, n_si, n_ho, n_hs = len(in_specs), len(s_in), len(out_specs), len(scratch_shapes)

    def fused(pt_ref, *refs):
        del pt_ref
        hi, si = refs[:n_hi], refs[n_hi:n_hi + n_si]
        o0 = n_hi + n_si
        ho, so = refs[o0:o0 + n_ho], refs[o0 + n_ho]
        c0 = o0 + n_ho + 1
        hs, ss = refs[c0:c0 + n_hs], refs[c0 + n_hs:]
        step = _DecodeStep(side, grid, si, so, ss)
        step.preamble()
        body(*hi, *ho, *hs, extra=step.main)
        step.finalize()

    grid_spec = pltpu.PrefetchScalarGridSpec(
        num_scalar_prefetch=1, grid=grid, in_specs=list(in_specs) + s_in,
        out_specs=list(out_specs) + [side.out_spec(grid)],
        scratch_shapes=list(scratch_shapes) + side.scratch_shapes())
    outs = pl.pallas_call(fused, grid_spec=grid_spec,
                          out_shape=list(out_shape) + [side.out_shape(grid)],
                          compiler_params=_cparams(sem), name=name)(side.pt, *args, *side.args())
    return outs[:-1], outs[-1]


def _attn_out_body(x_ref, gt_ref, o_ref, z_ref, sg_ref, wout_ref, g_ref, b_ref, y_ref, gbuf, *,
                   alpha, out_scale):
    d = x_ref.shape[-1]
    vd = sg_ref.shape[-1]
    for hv in range(d // vd):
        cols = slice(hv * vd, (hv + 1) * vd)
        of = o_ref[:, cols].astype(F32)
        of = of * lax.rsqrt(jnp.mean(of * of, axis=-1, keepdims=True) + SUBLN_EPS)
        of = of * sg_ref[...] * out_scale
        gbuf[:, cols] = (of * _silu(z_ref[:, cols].astype(F32))).astype(BF16)
    out = _dot(gbuf[...], wout_ref[...])
    res = alpha * x_ref[...] + gt_ref[...] * out
    y_ref[...] = _layer_norm(res, g_ref[...], b_ref[...])


def _attn_out_call(x, mod, layer, o, z, subln_g, w_out, ln_g, ln_b, *, tm, alpha, out_scale):
    bsz, t_rows, d = x.shape
    vd = subln_g.shape[-1]
    body = functools.partial(_attn_out_body, alpha=alpha, out_scale=out_scale)
    row_spec = pl.BlockSpec((None, tm, d), lambda b, t: (b, t, 0))
    const = lambda shape: pl.BlockSpec(shape, lambda b, t: (0,) * len(shape))
    return pl.pallas_call(
        body,
        grid=(bsz, t_rows // tm),
        in_specs=[row_spec, _mod_spec(mod, layer, 2), row_spec, row_spec, const((1, vd)),
                  const(w_out.shape),
                  const((1, d)), const((1, d))],
        out_specs=row_spec,
        out_shape=jax.ShapeDtypeStruct((bsz, t_rows, d), F32),
        scratch_shapes=[pltpu.VMEM((tm, d), BF16)],
        compiler_params=_cparams(("arbitrary", "arbitrary")),
        name="attn_out",
    )(x, mod, o, z, subln_g.reshape(1, vd), w_out, ln_g.reshape(1, d), ln_b.reshape(1, d))


def kernel(x_prompt, x_sample, state_conv_a, state_lru_h, state_lru_conv, cache_k, cache_v, page_table, c_prompt, c_sample, w_ada, b_ada, ln_g, ln_b, a_w_in, a_conv_w, a_w_out, r_w_in, r_conv_w, r_conv_b, r_w_ga, r_b_ga, r_w_gx, r_b_gx, r_lru_param, r_w_out, d_w_in, d_lq1, d_lk1, d_lq2, d_lk2, d_subln_g, d_w_out):
    bp, tp, d = x_prompt.shape
    bs, ts, _ = x_sample.shape
    depth = w_ada.shape[0]
    n_pages = page_table.shape[1]
    page = cache_k.shape[2]
    n_sub, head_dim = cache_k.shape[3], cache_k.shape[4]
    n_heads, vd = cache_v.shape[3], cache_v.shape[4]
    past_len = n_pages * page
    rot_dim = head_dim // 4
    alpha = (2 * depth) ** 0.25
    rows_s = bs * ts

    n_c = _round_up(bp + bs, SUBLANES)
    c_all = jnp.concatenate([c_sample, c_prompt, jnp.zeros((n_c - bp - bs, d), F32)], axis=0)
    mod_p, mod_s = _ada_call(c_all, w_ada, b_ada, bp=bp, bs=bs, ts=ts)

    def to_time_major(a):
        return jnp.swapaxes(a, 0, 1).reshape((1, a.shape[1] * bs) + a.shape[2:])

    def from_time_major(a, n):
        return jnp.swapaxes(a.reshape(n, bs, a.shape[-1]), 0, 1)

    bf = lambda w: w.astype(BF16)
    half = rot_dim // 2
    qscale = head_dim ** -0.5

    conv_p, conv_s, lruh_p, lruh_s, lruc_p, lruc_s = [], [], [], [], [], []
    k_p, v_p, k_s, v_s = [], [], [], []

    def layer_weights(i):
        kind, j = i % N_MIXERS, i // N_MIXERS
        if kind == 0:
            return (bf(a_w_in[j]), a_conv_w[j], bf(a_w_out[j]), ln_g[i], ln_b[i])
        if kind == 1:
            return (bf(r_w_in[j]), r_conv_w[j], r_conv_b[j], bf(r_w_ga[j]), r_b_ga[j],
                    bf(r_w_gx[j]), r_b_gx[j], r_lru_param[j], bf(r_w_out[j]), ln_g[i], ln_b[i])
        return (bf(d_w_in[j]), bf(d_w_out[j]),
                [v[j].reshape(1, head_dim) for v in (d_lq1, d_lk1, d_lq2, d_lk2)],
                0.8 - 0.6 * math.exp(-0.3 * i))

    xp = x_prompt
    xs = to_time_major(x_sample)
    for i in range(depth):
        kind, j = i % N_MIXERS, i // N_MIXERS
        w = layer_weights(i)
        if kind == 0:
            width = a_conv_w.shape[1]
            xp, nbp, _ = _conv_layer(xp, mod_p, i, jnp.zeros((bp, width - 1, d), F32), *w,
                                     s=1, tm=512, alpha=alpha)
            xs, nbs, _ = _conv_layer(xs, mod_s, i, to_time_major(state_conv_a[j]), *w,
                                     s=bs, tm=rows_s, alpha=alpha)
            conv_p.append(nbp)
            conv_s.append(from_time_major(nbs, width - 1))
        elif kind == 1:
            width = r_conv_w.shape[1]
            xp, nbp, hp, _ = _lru_layer(xp, mod_p, i, jnp.zeros((bp, width - 1, d), F32),
                                        jnp.zeros((bp, 1, d), F32), *w, s=1, tm=256, alpha=alpha)
            xs, nbs, hs, _ = _lru_layer(xs, mod_s, i, to_time_major(state_lru_conv[j]),
                                        state_lru_h[j].reshape(1, bs, d), *w, s=bs, tm=rows_s,
                                        alpha=alpha)
            lruh_p.append(hp.reshape(bp, d))
            lruh_s.append(hs.reshape(bs, d))
            lruc_p.append(nbp)
            lruc_s.append(from_time_major(nbs, width - 1))
        else:
            w_in, w_out, lvec, lam_init = w
            tabs_p = _rope_tables(np.arange(tp), head_dim, rot_dim)
            kp, vp, zp, qb, kb, vt = _qkv_call(xp, mod_p, i, *tabs_p, w_in, tm=256, half=half,
                                               qscale=qscale * math.log2(math.e), attn_vd=vd)
            op, _ = _flash_call(qb, kb, vt, *lvec, tq=512, head_dim=head_dim, nh=4,
                                lam_init=lam_init)
            xp = _attn_out_call(xp, mod_p, i, op, zp, d_subln_g[j], w_out, ln_g[i], ln_b[i],
                                tm=512, alpha=alpha, out_scale=1.0 - lam_init)
            k_p.append(kp.reshape(bp, tp, n_sub, head_dim))
            v_p.append(vp.reshape(bp, tp, n_heads, vd))
            pos_s = past_len + np.repeat(np.arange(ts), bs)
            tabs_s = _rope_tables(pos_s, head_dim, rot_dim)
            ks_, vs_, zs, qsb = _qkv_call(xs, mod_s, i, *tabs_s, w_in, tm=rows_s, half=half,
                                          qscale=qscale)
            ksn = from_time_major(ks_, ts)
            vsn = from_time_major(vs_, ts)
            n_layers, n_phys = cache_k.shape[0], cache_k.shape[1]
            cache_kt = jnp.transpose(cache_k, (0, 1, 3, 4, 2)).reshape(
                n_layers, n_phys, n_sub * head_dim, page)
            cache_vr = cache_v.reshape(n_layers, n_phys, page * n_heads, vd)
            side = _DecodeSide(page_table, from_time_major(qsb, ts), cache_kt, cache_vr, j, ksn,
                               vsn.reshape(bs, ts * n_heads, vd), lvec, first=0, gp=16,
                               n_heads=n_heads, lam_init=lam_init)
            _, os_ = _run(_idle_body, (bs * side.spb,), [], [], [], [], (), "attn_decode", side)
            xs = _attn_out_call(xs, mod_s, i, to_time_major(os_), zs, d_subln_g[j], w_out,
                                ln_g[i], ln_b[i], tm=rows_s, alpha=alpha,
                                out_scale=1.0 - lam_init)
            k_s.append(ksn.reshape(bs, ts, n_sub, head_dim))
            v_s.append(vsn.reshape(bs, ts, n_heads, vd))

    return (xp, from_time_major(xs, ts),
            jnp.stack(conv_p), jnp.stack(conv_s),
            jnp.stack(lruh_p), jnp.stack(lruh_s),
            jnp.stack(lruc_p), jnp.stack(lruc_s),
            jnp.stack(k_p), jnp.stack(v_p), jnp.stack(k_s), jnp.stack(v_s))
```

```python
import functools
import math

import jax
import jax.numpy as jnp
import numpy as np
from jax import lax
from jax.experimental import pallas as pl
from jax.experimental.pallas import tpu as pltpu

F32 = jnp.float32
BF16 = jnp.bfloat16

LN_EPS = 1e-5
SUBLN_EPS = 1e-5
LRU_C = 8.0
ROPE_THETA = 500000.0
N_MIXERS = 3

SUBLANES = 8
LANES = 128
VMEM_LIMIT_BYTES = 56 * 1024 * 1024


def _cparams(semantics):
    return pltpu.CompilerParams(dimension_semantics=semantics, vmem_limit_bytes=VMEM_LIMIT_BYTES)


def _dot(a, b):
    return jnp.dot(a, b, preferred_element_type=F32)


def _dot_nt(a, b):
    return lax.dot_general(a, b, (((1,), (1,)), ((), ())), preferred_element_type=F32)


def _silu(z):
    return z * jax.nn.sigmoid(z)


def _layer_norm(y, g, b):
    mu = jnp.mean(y, axis=-1, keepdims=True)
    yc = y - mu
    var = jnp.mean(yc * yc, axis=-1, keepdims=True)
    return yc * lax.rsqrt(var + LN_EPS) * g + b


def _round_up(n, m):
    return (n + m - 1) // m * m


def _ada_body(c_ref, w_ref, b_ref, op_ref, os_ref, *, bp, bs, ts):
    c = c_ref[...]
    a = _silu(c).astype(BF16)
    m = _dot(a, w_ref[...].astype(BF16)) + b_ref[...]
    for r in range(bp):
        op_ref[r] = m[bs + r:bs + r + 1, :]
    for t in range(ts):
        os_ref[t * bs:(t + 1) * bs, :] = m[0:bs, :]


def _ada_call(c_all, w_ada, b_ada, *, bp, bs, ts):
    depth, d, d3 = w_ada.shape
    rows = c_all.shape[0]
    nt = d3 // d
    return pl.pallas_call(
        functools.partial(_ada_body, bp=bp, bs=bs, ts=ts),
        grid=(depth, nt),
        in_specs=[
            pl.BlockSpec((rows, d), lambda i, n: (0, 0)),
            pl.BlockSpec((None, d, d), lambda i, n: (i, 0, n)),
            pl.BlockSpec((None, 1, d), lambda i, n: (i, 0, n)),
        ],
        out_specs=[pl.BlockSpec((None, None, bp, 1, d), lambda i, n: (i, n, 0, 0, 0)),
                   pl.BlockSpec((None, None, None, ts * bs, d), lambda i, n: (i, n, 0, 0, 0))],
        out_shape=[jax.ShapeDtypeStruct((depth, nt, bp, 1, d), F32),
                   jax.ShapeDtypeStruct((depth, nt, 1, ts * bs, d), F32)],
        compiler_params=_cparams(("arbitrary", "arbitrary")),
        name="adaln",
    )(c_all, w_ada, b_ada.reshape(depth, 1, d3))


def _mod_spec(mod, layer, k):
    _, _, _, r, d = mod.shape
    return pl.BlockSpec((None, None, None, r, d), lambda b, t, *_: (layer, k, b, 0, 0))


def _load_history(t, buf, st0_ref, pad, ks, tm):
    @pl.when(t == 0)
    def _():
        buf[pad - ks:pad, :] = st0_ref[...]

    @pl.when(t > 0)
    def _():
        buf[pad - ks:pad, :] = buf[pad + tm - ks:pad + tm, :]


def _conv_taps(buf, cw_ref, cols, width, s, pad, tm):
    y = None
    if s % SUBLANES == 0:
        for k in range(width):
            r0 = pad - (width - 1 - k) * s
            term = cw_ref[k:k + 1, cols] * buf[r0:r0 + tm, cols]
            y = term if y is None else y + term
        return y
    full = buf[0:pad + tm, cols]
    for k in range(width):
        back = (width - 1 - k) * s
        src = full if back == 0 else pltpu.roll(full, back, axis=0)
        term = cw_ref[k:k + 1, cols] * src[pad:pad + tm, :]
        y = term if y is None else y + term
    return y


def _conv_body(x_ref, sh_ref, sc_ref, gt_ref, st0_ref, win_ref, cw_ref, wout_ref, g_ref, b_ref,
               y_ref, st_ref, abuf, acc, *, s, tm, cw, width, pad, alpha):
    d = x_ref.shape[-1]
    ks = (width - 1) * s
    t = pl.program_id(1)
    _load_history(t, abuf, st0_ref, pad, ks, tm)
    x = x_ref[...]
    u = (x * (1.0 + sc_ref[...]) + sh_ref[...]).astype(BF16)
    for c in range(d // cw):
        cols = slice(c * cw, (c + 1) * cw)
        h = _dot(u, win_ref[:, c * cw:(c + 1) * cw])
        cg = _dot(u, win_ref[:, 2 * d + c * cw:2 * d + (c + 1) * cw])
        abuf[pad:pad + tm, cols] = cg * h
        y = _conv_taps(abuf, cw_ref, cols, width, s, pad, tm)
        bg = _dot(u, win_ref[:, d + c * cw:d + (c + 1) * cw])
        z = _dot(u, win_ref[:, 3 * d + c * cw:3 * d + (c + 1) * cw])
        gated = (_silu(z) * bg * y).astype(BF16)
        part = _dot(gated, wout_ref[cols, :])
        if c == 0:
            acc[...] = part
        else:
            acc[...] += part
    st_ref[...] = abuf[pad + tm - ks:pad + tm, :]
    res = alpha * x + gt_ref[...] * acc[...]
    y_ref[...] = _layer_norm(res, g_ref[...], b_ref[...])


def _conv_layer(x, mod, layer, st0, w_in, conv_w, w_out, ln_g, ln_b, *, s, tm, alpha):
    bsz, t_rows, d = x.shape
    width = conv_w.shape[0]
    ks = (width - 1) * s
    pad = _round_up(ks, SUBLANES)
    cw = 256
    body = functools.partial(_conv_body, s=s, tm=tm, cw=cw, width=width, pad=pad, alpha=alpha)
    row_spec = pl.BlockSpec((None, tm, d), lambda b, t, *_: (b, t, 0))
    st_spec = pl.BlockSpec((None, ks, d), lambda b, t, *_: (b, 0, 0))
    const = lambda shape: pl.BlockSpec(shape, lambda b, t, *_: (0,) * len(shape))
    return _call(
        body, (bsz, t_rows // tm),
        [row_spec, _mod_spec(mod, layer, 0), _mod_spec(mod, layer, 1),
         _mod_spec(mod, layer, 2), st_spec,
         const(w_in.shape), const(conv_w.shape), const(w_out.shape),
         const((1, d)), const((1, d))],
        [row_spec, st_spec],
        [jax.ShapeDtypeStruct((bsz, t_rows, d), F32), jax.ShapeDtypeStruct((bsz, ks, d), F32)],
        [pltpu.VMEM((pad + tm, d), F32), pltpu.VMEM((tm, d), F32)],
        (x, mod, mod, mod, st0, w_in, conv_w, w_out, ln_g.reshape(1, d), ln_b.reshape(1, d)),
        "conv_layer")


def _lru_body(x_ref, sh_ref, sc_ref, gt_ref, st0_ref, h0_ref, win_ref, cw_ref, cb_ref,
              wga_ref, bga_ref, wgx_ref, bgx_ref, prm_ref, wout_ref, g_ref, b_ref,
              y_ref, st_ref, hl_ref, xbuf, a_s, b_s, h_s, hc, *, s, tm, width, pad, alpha):
    d = x_ref.shape[-1]
    nblk, blk, _ = wga_ref.shape
    ks = (width - 1) * s
    t = pl.program_id(1)
    _load_history(t, xbuf, st0_ref, pad, ks, tm)

    @pl.when(t == 0)
    def _():
        hc[...] = h0_ref[...]

    x = x_ref[...]
    u = (x * (1.0 + sc_ref[...]) + sh_ref[...]).astype(BF16)
    xbuf[pad:pad + tm, :] = _dot(u, win_ref[:, 0:d])
    st_ref[...] = xbuf[pad + tm - ks:pad + tm, :]
    for n in range(nblk):
        cols = slice(n * blk, (n + 1) * blk)
        xc = _conv_taps(xbuf, cw_ref, cols, width, s, pad, tm) + cb_ref[:, cols]
        xcb = xc.astype(BF16)
        r = jax.nn.sigmoid(_dot(xcb, wga_ref[n]) + bga_ref[:, cols])
        gi = jax.nn.sigmoid(_dot(xcb, wgx_ref[n]) + bgx_ref[:, cols])
        log_a = LRU_C * r * jax.nn.log_sigmoid(prm_ref[:, cols])
        a = jnp.exp(log_a)
        a_s[:, cols] = a
        b_s[:, cols] = jnp.sqrt(-jnp.tanh(log_a) * (a * a + 1.0)) * (gi * xc)

    steps = tm // s

    def step(i, h):
        r0 = pl.multiple_of(i * s, s)
        h = a_s[pl.ds(r0, s), :] * h + b_s[pl.ds(r0, s), :]
        h_s[pl.ds(r0, s), :] = h
        return h

    h_last = lax.fori_loop(0, steps, step, hc[...], unroll=True)
    hc[...] = h_last
    hl_ref[...] = h_last

    z = _dot(u, win_ref[:, d:2 * d])
    yy = (h_s[...] * _silu(z)).astype(BF16)
    out = _dot(yy, wout_ref[...])
    res = alpha * x + gt_ref[...] * out
    y_ref[...] = _layer_norm(res, g_ref[...], b_ref[...])


def _lru_layer(x, mod, layer, st0, h0, w_in, conv_w, conv_b, w_ga, b_ga, w_gx, b_gx, prm, w_out,
               ln_g, ln_b, *, s, tm, alpha):
    bsz, t_rows, d = x.shape
    width = conv_w.shape[0]
    ks = (width - 1) * s
    pad = _round_up(ks, SUBLANES)
    body = functools.partial(_lru_body, s=s, tm=tm, width=width, pad=pad, alpha=alpha)
    row_spec = pl.BlockSpec((None, tm, d), lambda b, t, *_: (b, t, 0))
    st_spec = pl.BlockSpec((None, ks, d), lambda b, t, *_: (b, 0, 0))
    h_spec = pl.BlockSpec((None, s, d), lambda b, t, *_: (b, 0, 0))
    const = lambda shape: pl.BlockSpec(shape, lambda b, t, *_: (0,) * len(shape))
    vec = const((1, d))
    return _call(
        body, (bsz, t_rows // tm),
        [row_spec, _mod_spec(mod, layer, 0), _mod_spec(mod, layer, 1),
         _mod_spec(mod, layer, 2), st_spec, h_spec,
         const(w_in.shape), const(conv_w.shape), vec,
         const(w_ga.shape), vec, const(w_gx.shape), vec, vec,
         const(w_out.shape), vec, vec],
        [row_spec, st_spec, h_spec],
        [jax.ShapeDtypeStruct((bsz, t_rows, d), F32), jax.ShapeDtypeStruct((bsz, ks, d), F32),
         jax.ShapeDtypeStruct((bsz, s, d), F32)],
        [pltpu.VMEM((pad + tm, d), F32), pltpu.VMEM((tm, d), F32), pltpu.VMEM((tm, d), F32),
         pltpu.VMEM((tm, d), F32), pltpu.VMEM((s, d), F32)],
        (x, mod, mod, mod, st0, h0, w_in, conv_w, conv_b.reshape(1, d), w_ga,
         b_ga.reshape(1, d), w_gx, b_gx.reshape(1, d), prm.reshape(1, d), w_out,
         ln_g.reshape(1, d), ln_b.reshape(1, d)),
        "lru_layer")


def _rope_block(xb, cos, sin_lo, sin_hi, half):
    return (xb * cos + pltpu.roll(xb, LANES - half, axis=1) * sin_lo
            + pltpu.roll(xb, half, axis=1) * sin_hi)


def _qkv_body(x_ref, sh_ref, sc_ref, cos_ref, slo_ref, shi_ref, win_ref,
              k_ref, v_ref, z_ref, qb_ref, *attn_refs, half, qscale):
    d = x_ref.shape[-1]
    x = x_ref[...]
    u = (x * (1.0 + sc_ref[...]) + sh_ref[...]).astype(BF16)
    cos, slo, shi = cos_ref[...], slo_ref[...], shi_ref[...]
    q = _dot(u, win_ref[:, 0:d])
    for j in range(d // LANES):
        cols = slice(j * LANES, (j + 1) * LANES)
        qb_ref[:, cols] = (_rope_block(q[:, cols], cos, slo, shi, half) * qscale).astype(BF16)
    k = _dot(u, win_ref[:, d:2 * d])
    for j in range(d // LANES):
        cols = slice(j * LANES, (j + 1) * LANES)
        kr = _rope_block(k[:, cols], cos, slo, shi, half)
        k_ref[:, cols] = kr
        if attn_refs:
            attn_refs[0][:, cols] = kr.astype(BF16)
    v = _dot(u, win_ref[:, 2 * d:3 * d])
    v_ref[...] = v
    if attn_refs:
        vt_ref = attn_refs[1]
        vd = vt_ref.shape[1]
        for hv in range(vt_ref.shape[0]):
            vt_ref[hv] = v[:, hv * vd:(hv + 1) * vd].T.astype(BF16)
    z_ref[...] = _dot(u, win_ref[:, 3 * d:4 * d]).astype(z_ref.dtype)


def _qkv_call(x, mod, layer, cos_t, slo_t, shi_t, w_in, *, tm, half, qscale, attn_vd=None):
    bsz, t_rows, d = x.shape
    body = functools.partial(_qkv_body, half=half, qscale=qscale)
    row_spec = pl.BlockSpec((None, tm, d), lambda b, t: (b, t, 0))
    tab_spec = pl.BlockSpec((tm, LANES), lambda b, t: (t, 0))
    f32_out = jax.ShapeDtypeStruct((bsz, t_rows, d), F32)
    bf_out = jax.ShapeDtypeStruct((bsz, t_rows, d), BF16)
    out_specs = [row_spec] * 4
    out_shape = [f32_out, f32_out, bf_out, bf_out]
    if attn_vd is not None:
        n_heads = d // attn_vd
        out_specs += [row_spec, pl.BlockSpec((None, n_heads, None, attn_vd, tm),
                                             lambda b, t: (b, 0, t, 0, 0))]
        out_shape += [bf_out,
                      jax.ShapeDtypeStruct((bsz, n_heads, t_rows // tm, attn_vd, tm), BF16)]
    return pl.pallas_call(
        body,
        grid=(bsz, t_rows // tm),
        in_specs=[row_spec, _mod_spec(mod, layer, 0), _mod_spec(mod, layer, 1),
                  tab_spec, tab_spec, tab_spec,
                  pl.BlockSpec(w_in.shape, lambda b, t: (0, 0))],
        out_specs=out_specs,
        out_shape=out_shape,
        compiler_params=_cparams(("arbitrary", "arbitrary")),
        name="attn_qkv",
    )(x, mod, mod, cos_t, slo_t, shi_t, w_in)


def _rope_tables(pos, head_dim, rot_dim):
    half = rot_dim // 2
    inv_freq = np.exp(np.arange(half, dtype=np.float64) * (-2.0 * math.log(ROPE_THETA) / rot_dim))
    ang = np.asarray(pos, dtype=np.float64)[:, None] * inv_freq[None, :]
    cos, sin = np.cos(ang), np.sin(ang)
    n = ang.shape[0]
    ones = np.ones((n, head_dim - rot_dim))
    zeros = np.zeros((n, head_dim - rot_dim))
    zh = np.zeros((n, half))
    cos_h = np.concatenate([cos, cos, ones], axis=1)
    slo_h = np.concatenate([-sin, zh, zeros], axis=1)
    shi_h = np.concatenate([zh, sin, zeros], axis=1)
    rep = LANES // head_dim
    return tuple(jnp.asarray(np.tile(t, (1, rep)), dtype=F32) for t in (cos_h, slo_h, shi_h))


def _diff_lambda(lq1_ref, lk1_ref, lq2_ref, lk2_ref, lam_init):
    s1 = jnp.sum(lq1_ref[...] * lk1_ref[...], axis=-1, keepdims=True)
    s2 = jnp.sum(lq2_ref[...] * lk2_ref[...], axis=-1, keepdims=True)
    return jnp.exp(s1) - jnp.exp(s2) + lam_init


def _flash_body(q_ref, k_ref, vt_ref, lq1_ref, lk1_ref, lq2_ref, lk2_ref, o_ref,
                qq_s, sa_s, sb_s, m_s, acc_s, *, tq, head_dim, nh, lam_init):
    qi = pl.program_id(2)
    vd, tk = vt_ref.shape[2], vt_ref.shape[3]
    nd = tq // tk
    lane = lax.broadcasted_iota(jnp.int32, (tq, LANES), 1)
    for h in range(nh):
        q = q_ref[:, h * LANES:(h + 1) * LANES]
        zero = jnp.zeros_like(q)
        qq_s[h, 0:tq, :] = jnp.where(lane < head_dim, q, zero)
        qq_s[h, tq:2 * tq, :] = jnp.where(lane >= head_dim, q, zero)
    m_s[...] = jnp.full(m_s.shape, -jnp.inf, F32)
    acc_s[...] = jnp.zeros(acc_s.shape, F32)
    ones = jnp.ones((2 * SUBLANES, tk), BF16)

    def scores(j, s_ref, q0=0):
        r0 = pl.multiple_of(j * tk, tk)
        for h in range(nh):
            k = k_ref[pl.ds(r0, tk), h * LANES:(h + 1) * LANES]
            if q0 == 0:
                s_ref[h] = _dot_nt(k, qq_s[h])
            else:
                qq = jnp.concatenate([qq_s[h, q0:tq, :], qq_s[h, tq + q0:2 * tq, :]], axis=0)
                s_ref[h, :, 0:2 * (tq - q0)] = _dot_nt(k, qq)

    def consume(j, s_ref, diag=False, q0=0):
        w = tq - q0
        lo, hi = slice(q0, tq), slice(tq + q0, 2 * tq)
        for h in range(nh):
            if q0 == 0:
                st, m_prev, acc_prev = s_ref[h], m_s[h], acc_s[h]
            else:
                st = s_ref[h, :, 0:2 * w]
                m_prev = jnp.concatenate([m_s[h, :, lo], m_s[h, :, hi]], axis=1)
                acc_prev = jnp.concatenate([acc_s[h, :, lo], acc_s[h, :, hi]], axis=1)
            if diag:
                key = lax.broadcasted_iota(jnp.int32, st.shape, 0)
                row = lax.broadcasted_iota(jnp.int32, st.shape, 1)
                row = jnp.where(row >= w, row - w, row)
                st = jnp.where(key <= row, st, -jnp.inf)
            m_new = jnp.maximum(m_prev, jnp.max(st, axis=0, keepdims=True))
            alpha = jnp.exp2(m_prev - m_new)
            pt = jnp.exp2(st - m_new).astype(BF16)
            v1 = jnp.concatenate([vt_ref[h, j], ones], axis=0)
            acc_new = alpha * acc_prev + _dot(v1, pt)
            if q0 == 0:
                acc_s[h] = acc_new
                m_s[h] = m_new
            else:
                acc_s[h, :, lo] = acc_new[:, 0:w]
                acc_s[h, :, hi] = acc_new[:, w:2 * w]
                m_s[h, :, lo] = m_new[:, 0:w]
                m_s[h, :, hi] = m_new[:, w:2 * w]

    scores(0, sa_s)

    def pair(i, carry):
        j = 2 * i
        scores(j + 1, sb_s)
        consume(j, sa_s)
        scores(j + 2, sa_s)
        consume(j + 1, sb_s)
        return carry

    lax.fori_loop(0, qi * (nd // 2), pair, 0)

    bufs = (sa_s, sb_s)
    for g in range(nd):
        j = nd * qi + g
        if g + 1 < nd:
            scores(j + 1, bufs[(g + 1) % 2], q0=(g + 1) * tk)
        consume(j, bufs[g % 2], diag=True, q0=g * tk)

    lam = _diff_lambda(lq1_ref, lk1_ref, lq2_ref, lk2_ref, lam_init)
    for h in range(nh):
        acc = acc_s[h]
        ot = acc[0:vd, :] / acc[vd:vd + 1, :]
        o_ref[:, h * LANES:(h + 1) * LANES] = (
            (ot[:, 0:tq] - lam * ot[:, tq:2 * tq]).T.astype(o_ref.dtype))


def _flash_call(qb, kb, vt, lq1, lk1, lq2, lk2, *, tq, head_dim, nh, lam_init):
    bsz, t_rows, d = qb.shape
    _, n_heads, n_blk, vd, tk = vt.shape
    assert tq % (2 * tk) == 0 and n_blk * tk == t_rows and vd == LANES
    body = functools.partial(_flash_body, tq=tq, head_dim=head_dim, nh=nh, lam_init=lam_init)
    q_spec = pl.BlockSpec((None, tq, nh * LANES), lambda b, h, i, *_: (b, i, h))
    k_spec = pl.BlockSpec((None, t_rows, nh * LANES), lambda b, h, i, *_: (b, 0, h))
    vt_spec = pl.BlockSpec((None, nh, n_blk, vd, tk), lambda b, h, i, *_: (b, h, 0, 0, 0))
    l_spec = pl.BlockSpec((1, head_dim), lambda b, h, i, *_: (0, 0))
    return _call(
        body, (bsz, n_heads // nh, t_rows // tq),
        [q_spec, k_spec, vt_spec, l_spec, l_spec, l_spec, l_spec],
        q_spec,
        jax.ShapeDtypeStruct((bsz, t_rows, d), BF16),
        [pltpu.VMEM((nh, 2 * tq, LANES), BF16), pltpu.VMEM((nh, tk, 2 * tq), F32),
         pltpu.VMEM((nh, tk, 2 * tq), F32), pltpu.VMEM((nh, 1, 2 * tq), F32),
         pltpu.VMEM((nh, vd + 2 * SUBLANES, 2 * tq), F32)],
        (qb, kb, vt, lq1, lk1, lq2, lk2),
        "attn_flash")


class _DecodeConfig:
    def __init__(self, page_table, qs, cache_kt, cache_vr, layer, k_new, v_new, lvec, *, first,
                 gp, n_heads, lam_init):
        self.pt = page_table.reshape(-1)
        self.n_pages = page_table.shape[1]
        self.qs, self.kt, self.vr, self.layer = qs, cache_kt, cache_vr, layer
        self.k_new, self.v_new, self.lvec = k_new, v_new, list(lvec)
        self.first, self.gp, self.n_heads, self.lam_init = first, gp, n_heads, lam_init
        self.kd, self.page = cache_kt.shape[2], cache_kt.shape[3]
        self.vd = cache_vr.shape[-1]
        self.t_new = k_new.shape[1]
        self.head_dim = lvec[0].shape[-1]
        self.n_sub = self.kd // self.head_dim
        self.rows = self.n_sub * self.t_new
        self.spb = self.n_pages // gp

    def count(self, grid):
        assert len(grid) == 2 and grid[1] == self.spb
        return grid[0]

    def in_specs(self, grid):
        first, gp, n_pages, layer = self.first, self.gp, self.n_pages, self.layer
        self.count(grid)

        def page_spec(shape, g):
            return pl.BlockSpec(
                (None, None) + shape,
                lambda b, p, pt: (layer, pt[(first + b) * n_pages + p * gp + g], 0, 0))

        def per_seq(shape):
            return pl.BlockSpec((None,) + shape, lambda b, p, pt: (first + b, 0, 0))

        l_spec = pl.BlockSpec((1, self.head_dim), lambda *a: (0, 0))
        return ([per_seq((self.t_new, self.kd))]
                + [page_spec((self.kd, self.page), g) for g in range(gp)]
                + [page_spec((self.page * self.n_heads, self.vd), g) for g in range(gp)]
                + [per_seq((self.t_new, self.kd)), per_seq((self.t_new * self.n_heads, self.vd)),
                   l_spec, l_spec, l_spec, l_spec])

    def args(self):
        return ([self.qs] + [self.kt] * self.gp + [self.vr] * self.gp
                + [self.k_new, self.v_new] + self.lvec)

    def out_spec(self, grid):
        return pl.BlockSpec((None, self.t_new, self.n_heads * self.vd),
                            lambda b, p, pt: (b, 0, 0))

    def out_shape(self, grid):
        return jax.ShapeDtypeStruct((self.count(grid), self.t_new, self.n_heads * self.vd), F32)

    def scratch_shapes(self):
        return [pltpu.VMEM((self.rows, self.kd), F32), pltpu.VMEM((self.rows, self.kd), BF16),
                pltpu.VMEM((self.rows, 1), F32), pltpu.VMEM((self.rows, 1), F32),
                pltpu.VMEM((self.rows, self.vd), F32),
                pltpu.VMEM((self.page, self.kd), F32),
                pltpu.VMEM((self.page * self.n_heads, self.vd), F32)]


class _DecodeStep:
    def __init__(self, side, grid, in_refs, out_ref, scratch_refs):
        gp = side.gp
        self.c = side
        self.q_ref = in_refs[0]
        self.k_refs, self.v_refs = in_refs[1:1 + gp], in_refs[1 + gp:1 + 2 * gp]
        self.kn_ref, self.vn_ref = in_refs[1 + 2 * gp], in_refs[2 + 2 * gp]
        self.l_refs = in_refs[3 + 2 * gp:]
        self.o_ref = out_ref
        (self.qf_s, self.qx_s, self.m_s, self.l_s, self.acc_s, self.kpad, self.vpad) = scratch_refs
        self.b = pl.program_id(0)
        self.p = pl.program_id(1)

    def preamble(self):
        c = self.c

        @pl.when((self.b == 0) & (self.p == 0))
        def _():
            self.kpad[...] = jnp.zeros(self.kpad.shape, F32)
            self.vpad[...] = jnp.zeros(self.vpad.shape, F32)

        @pl.when(self.p == 0)
        def _():
            self.m_s[...] = jnp.full(self.m_s.shape, -jnp.inf, F32)
            self.l_s[...] = jnp.zeros(self.l_s.shape, F32)
            self.acc_s[...] = jnp.zeros(self.acc_s.shape, F32)
            q = self.q_ref[...].astype(F32)
            lane = lax.broadcasted_iota(jnp.int32, q.shape, 1)
            for h in range(c.n_sub):
                own = (lane >= h * c.head_dim) & (lane < (h + 1) * c.head_dim)
                self.qf_s[h * c.t_new:(h + 1) * c.t_new, :] = jnp.where(own, q, 0.0)
            self.qx_s[...] = self.qf_s[...].astype(BF16)

    def _update(self, sc, vs):
        c = self.c
        m_prev = self.m_s[...]
        m_new = jnp.maximum(m_prev, jnp.max(sc, axis=1, keepdims=True))
        alpha = jnp.exp(m_prev - m_new)
        pe = jnp.exp(sc - m_new)
        self.l_s[...] = alpha * self.l_s[...] + jnp.sum(pe, axis=1, keepdims=True)
        self.m_s[...] = m_new
        pb = pe.astype(BF16)
        grp = 2 * c.t_new
        for hv in range(c.n_heads):
            rs = slice(hv * grp, (hv + 1) * grp)
            vh = jnp.concatenate(
                [v[pl.ds(hv, c.page, stride=c.n_heads), :].astype(BF16) for v in vs], axis=0)
            self.acc_s[rs, :] = alpha[rs, :] * self.acc_s[rs, :] + _dot(pb[rs, :], vh)

    def main(self):
        kt = jnp.concatenate([k[...].astype(BF16) for k in self.k_refs], axis=1)
        self._update(_dot(self.qx_s[...], kt), self.v_refs)

    def finalize(self):
        c = self.c

        @pl.when(self.p == c.spb - 1)
        def _():
            self.kpad[0:c.t_new, :] = self.kn_ref[...]
            self.vpad[0:c.t_new * c.n_heads, :] = self.vn_ref[...]
            sc = _dot_nt(self.qx_s[...], self.kpad[...].astype(BF16))
            row = lax.broadcasted_iota(jnp.int32, sc.shape, 0)
            col = lax.broadcasted_iota(jnp.int32, sc.shape, 1)
            self._update(jnp.where(col <= row % c.t_new, sc, -jnp.inf), [self.vpad])
            o = self.acc_s[...] / self.l_s[...]
            lam = _diff_lambda(*self.l_refs, c.lam_init)
            grp = 2 * c.t_new
            for hv in range(c.n_heads):
                o1 = o[hv * grp:hv * grp + c.t_new, :]
                o2 = o[hv * grp + c.t_new:(hv + 1) * grp, :]
                self.o_ref[:, hv * c.vd:(hv + 1) * c.vd] = o1 - lam * o2


def _decode_call(cfg, n_seq):
    grid = (n_seq, cfg.spb)
    in_specs = cfg.in_specs(grid)
    n_in = len(in_specs)

    def body(pt_ref, *refs):
        del pt_ref
        step = _DecodeStep(cfg, grid, refs[:n_in], refs[n_in], refs[n_in + 1:])
        step.preamble()
        step.main()
        step.finalize()

    grid_spec = pltpu.PrefetchScalarGridSpec(
        num_scalar_prefetch=1, grid=grid, in_specs=in_specs, out_specs=cfg.out_spec(grid),
        scratch_shapes=cfg.scratch_shapes())
    return pl.pallas_call(body, grid_spec=grid_spec, out_shape=cfg.out_shape(grid),
                          compiler_params=_cparams(("arbitrary", "arbitrary")),
                          name="attn_decode")(cfg.pt, *cfg.args())


def _call(body, grid, in_specs, out_specs, out_shape, scratch_shapes, args, name):
    return pl.pallas_call(body, grid=grid, in_specs=in_specs, out_specs=out_specs,
                          out_shape=out_shape, scratch_shapes=scratch_shapes,
                          compiler_params=_cparams(("arbitrary",) * len(grid)), name=name)(*args)


def _attn_out_body(x_ref, gt_ref, o_ref, z_ref, sg_ref, wout_ref, g_ref, b_ref, y_ref, gbuf, *,
                   alpha, out_scale):
    d = x_ref.shape[-1]
    vd = sg_ref.shape[-1]
    for hv in range(d // vd):
        cols = slice(hv * vd, (hv + 1) * vd)
        of = o_ref[:, cols].astype(F32)
        of = of * lax.rsqrt(jnp.mean(of * of, axis=-1, keepdims=True) + SUBLN_EPS)
        of = of * sg_ref[...] * out_scale
        gbuf[:, cols] = (of * _silu(z_ref[:, cols].astype(F32))).astype(BF16)
    out = _dot(gbuf[...], wout_ref[...])
    res = alpha * x_ref[...] + gt_ref[...] * out
    y_ref[...] = _layer_norm(res, g_ref[...], b_ref[...])


def _attn_out_call(x, mod, layer, o, z, subln_g, w_out, ln_g, ln_b, *, tm, alpha, out_scale):
    bsz, t_rows, d = x.shape
    vd = subln_g.shape[-1]
    body = functools.partial(_attn_out_body, alpha=alpha, out_scale=out_scale)
    row_spec = pl.BlockSpec((None, tm, d), lambda b, t: (b, t, 0))
    const = lambda shape: pl.BlockSpec(shape, lambda b, t: (0,) * len(shape))
    return pl.pallas_call(
        body,
        grid=(bsz, t_rows // tm),
        in_specs=[row_spec, _mod_spec(mod, layer, 2), row_spec, row_spec, const((1, vd)),
                  const(w_out.shape),
                  const((1, d)), const((1, d))],
        out_specs=row_spec,
        out_shape=jax.ShapeDtypeStruct((bsz, t_rows, d), F32),
        scratch_shapes=[pltpu.VMEM((tm, d), BF16)],
        compiler_params=_cparams(("arbitrary", "arbitrary")),
        name="attn_out",
    )(x, mod, o, z, subln_g.reshape(1, vd), w_out, ln_g.reshape(1, d), ln_b.reshape(1, d))


def kernel(x_prompt, x_sample, state_conv_a, state_lru_h, state_lru_conv, cache_k, cache_v, page_table, c_prompt, c_sample, w_ada, b_ada, ln_g, ln_b, a_w_in, a_conv_w, a_w_out, r_w_in, r_conv_w, r_conv_b, r_w_ga, r_b_ga, r_w_gx, r_b_gx, r_lru_param, r_w_out, d_w_in, d_lq1, d_lk1, d_lq2, d_lk2, d_subln_g, d_w_out):
    bp, tp, d = x_prompt.shape
    bs, ts, _ = x_sample.shape
    depth = w_ada.shape[0]
    n_pages = page_table.shape[1]
    page = cache_k.shape[2]
    n_sub, head_dim = cache_k.shape[3], cache_k.shape[4]
    n_heads, vd = cache_v.shape[3], cache_v.shape[4]
    past_len = n_pages * page
    rot_dim = head_dim // 4
    alpha = (2 * depth) ** 0.25
    rows_s = bs * ts

    n_c = _round_up(bp + bs, SUBLANES)
    c_all = jnp.concatenate([c_sample, c_prompt, jnp.zeros((n_c - bp - bs, d), F32)], axis=0)
    mod_p, mod_s = _ada_call(c_all, w_ada, b_ada, bp=bp, bs=bs, ts=ts)

    def to_time_major(a):
        return jnp.swapaxes(a, 0, 1).reshape((1, a.shape[1] * bs) + a.shape[2:])

    def from_time_major(a, n):
        return jnp.swapaxes(a.reshape(n, bs, a.shape[-1]), 0, 1)

    bf = lambda w: w.astype(BF16)
    half = rot_dim // 2
    qscale = head_dim ** -0.5

    conv_p, conv_s, lruh_p, lruh_s, lruc_p, lruc_s = [], [], [], [], [], []
    k_p, v_p, k_s, v_s = [], [], [], []

    def layer_weights(i):
        kind, j = i % N_MIXERS, i // N_MIXERS
        if kind == 0:
            return (bf(a_w_in[j]), a_conv_w[j], bf(a_w_out[j]), ln_g[i], ln_b[i])
        if kind == 1:
            return (bf(r_w_in[j]), r_conv_w[j], r_conv_b[j], bf(r_w_ga[j]), r_b_ga[j],
                    bf(r_w_gx[j]), r_b_gx[j], r_lru_param[j], bf(r_w_out[j]), ln_g[i], ln_b[i])
        return (bf(d_w_in[j]), bf(d_w_out[j]),
                [v[j].reshape(1, head_dim) for v in (d_lq1, d_lk1, d_lq2, d_lk2)],
                0.8 - 0.6 * math.exp(-0.3 * i))

    xp = x_prompt
    xs = to_time_major(x_sample)
    for i in range(depth):
        kind, j = i % N_MIXERS, i // N_MIXERS
        w = layer_weights(i)
        if kind == 0:
            width = a_conv_w.shape[1]
            xp, nbp = _conv_layer(xp, mod_p, i, jnp.zeros((bp, width - 1, d), F32), *w,
                                  s=1, tm=512, alpha=alpha)
            xs, nbs = _conv_layer(xs, mod_s, i, to_time_major(state_conv_a[j]), *w,
                                  s=bs, tm=rows_s, alpha=alpha)
            conv_p.append(nbp)
            conv_s.append(from_time_major(nbs, width - 1))
        elif kind == 1:
            width = r_conv_w.shape[1]
            xp, nbp, hp = _lru_layer(xp, mod_p, i, jnp.zeros((bp, width - 1, d), F32),
                                     jnp.zeros((bp, 1, d), F32), *w, s=1, tm=256, alpha=alpha)
            xs, nbs, hs = _lru_layer(xs, mod_s, i, to_time_major(state_lru_conv[j]),
                                     state_lru_h[j].reshape(1, bs, d), *w, s=bs, tm=rows_s,
                                     alpha=alpha)
            lruh_p.append(hp.reshape(bp, d))
            lruh_s.append(hs.reshape(bs, d))
            lruc_p.append(nbp)
            lruc_s.append(from_time_major(nbs, width - 1))
        else:
            w_in, w_out, lvec, lam_init = w
            tabs_p = _rope_tables(np.arange(tp), head_dim, rot_dim)
            kp, vp, zp, qb, kb, vt = _qkv_call(xp, mod_p, i, *tabs_p, w_in, tm=256, half=half,
                                               qscale=qscale * math.log2(math.e), attn_vd=vd)
            op = _flash_call(qb, kb, vt, *lvec, tq=512, head_dim=head_dim, nh=4,
                             lam_init=lam_init)
            xp = _attn_out_call(xp, mod_p, i, op, zp, d_subln_g[j], w_out, ln_g[i], ln_b[i],
                                tm=512, alpha=alpha, out_scale=1.0 - lam_init)
            k_p.append(kp.reshape(bp, tp, n_sub, head_dim))
            v_p.append(vp.reshape(bp, tp, n_heads, vd))
            pos_s = past_len + np.repeat(np.arange(ts), bs)
            tabs_s = _rope_tables(pos_s, head_dim, rot_dim)
            ks_, vs_, zs, qsb = _qkv_call(xs, mod_s, i, *tabs_s, w_in, tm=rows_s, half=half,
                                          qscale=qscale)
            ksn = from_time_major(ks_, ts)
            vsn = from_time_major(vs_, ts)
            n_layers, n_phys = cache_k.shape[0], cache_k.shape[1]
            cache_kt = jnp.transpose(cache_k, (0, 1, 3, 4, 2)).reshape(
                n_layers, n_phys, n_sub * head_dim, page)
            cache_vr = cache_v.reshape(n_layers, n_phys, page * n_heads, vd)
            cfg = _DecodeConfig(page_table, from_time_major(qsb, ts), cache_kt, cache_vr, j, ksn,
                                vsn.reshape(bs, ts * n_heads, vd), lvec, first=0, gp=16,
                                n_heads=n_heads, lam_init=lam_init)
            os_ = _decode_call(cfg, bs)
            xs = _attn_out_call(xs, mod_s, i, to_time_major(os_), zs, d_subln_g[j], w_out,
                                ln_g[i], ln_b[i], tm=rows_s, alpha=alpha,
                                out_scale=1.0 - lam_init)
            k_s.append(ksn.reshape(bs, ts, n_sub, head_dim))
            v_s.append(vsn.reshape(bs, ts, n_heads, vd))

    return (xp, from_time_major(xs, ts),
            jnp.stack(conv_p), jnp.stack(conv_s),
            jnp.stack(lruh_p), jnp.stack(lruh_s),
            jnp.stack(lruc_p), jnp.stack(lruc_s),
            jnp.stack(k_p), jnp.stack(v_p), jnp.stack(k_s), jnp.stack(v_s))
```

```python
import functools
import math

import jax
import jax.numpy as jnp
import numpy as np
from jax import lax
from jax.experimental import pallas as pl
from jax.experimental.pallas import tpu as pltpu

F32 = jnp.float32
BF16 = jnp.bfloat16

LN_EPS = 1e-5
SUBLN_EPS = 1e-5
LRU_C = 8.0
ROPE_THETA = 500000.0
N_MIXERS = 3

SUBLANES = 8
LANES = 128
VMEM_LIMIT_BYTES = 56 * 1024 * 1024
PAGE_BUFFERS = 2


def _cparams(semantics):
    return pltpu.CompilerParams(dimension_semantics=semantics, vmem_limit_bytes=VMEM_LIMIT_BYTES)


def _dot(a, b):
    return jnp.dot(a, b, preferred_element_type=F32)


def _dot_nt(a, b):
    return lax.dot_general(a, b, (((1,), (1,)), ((), ())), preferred_element_type=F32)


def _silu(z):
    return z * jax.nn.sigmoid(z)


def _layer_norm(y, g, b):
    mu = jnp.mean(y, axis=-1, keepdims=True)
    yc = y - mu
    var = jnp.mean(yc * yc, axis=-1, keepdims=True)
    return yc * lax.rsqrt(var + LN_EPS) * g + b


def _round_up(n, m):
    return (n + m - 1) // m * m


def _ada_body(c_ref, w_ref, b_ref, op_ref, os_ref, *, bp, bs, ts):
    c = c_ref[...]
    a = _silu(c).astype(BF16)
    m = _dot(a, w_ref[...].astype(BF16)) + b_ref[...]
    for r in range(bp):
        op_ref[r] = m[bs + r:bs + r + 1, :]
    for t in range(ts):
        os_ref[t * bs:(t + 1) * bs, :] = m[0:bs, :]


def _ada_call(c_all, w_ada, b_ada, *, bp, bs, ts):
    depth, d, d3 = w_ada.shape
    rows = c_all.shape[0]
    nt = d3 // d
    return pl.pallas_call(
        functools.partial(_ada_body, bp=bp, bs=bs, ts=ts),
        grid=(depth, nt),
        in_specs=[
            pl.BlockSpec((rows, d), lambda i, n: (0, 0)),
            pl.BlockSpec((None, d, d), lambda i, n: (i, 0, n)),
            pl.BlockSpec((None, 1, d), lambda i, n: (i, 0, n)),
        ],
        out_specs=[pl.BlockSpec((None, None, bp, 1, d), lambda i, n: (i, n, 0, 0, 0)),
                   pl.BlockSpec((None, None, None, ts * bs, d), lambda i, n: (i, n, 0, 0, 0))],
        out_shape=[jax.ShapeDtypeStruct((depth, nt, bp, 1, d), F32),
                   jax.ShapeDtypeStruct((depth, nt, 1, ts * bs, d), F32)],
        compiler_params=_cparams(("arbitrary", "arbitrary")),
        name="adaln",
    )(c_all, w_ada, b_ada.reshape(depth, 1, d3))


def _mod_spec(mod, layer, k):
    _, _, _, r, d = mod.shape
    return pl.BlockSpec((None, None, None, r, d), lambda b, t, *_: (layer, k, b, 0, 0))


def _load_history(t, buf, st0_ref, pad, ks, tm):
    @pl.when(t == 0)
    def _():
        buf[pad - ks:pad, :] = st0_ref[...]

    @pl.when(t > 0)
    def _():
        buf[pad - ks:pad, :] = buf[pad + tm - ks:pad + tm, :]


def _conv_taps(buf, cw_ref, cols, width, s, pad, tm):
    y = None
    if s % SUBLANES == 0:
        for k in range(width):
            r0 = pad - (width - 1 - k) * s
            term = cw_ref[k:k + 1, cols] * buf[r0:r0 + tm, cols]
            y = term if y is None else y + term
        return y
    full = buf[0:pad + tm, cols]
    for k in range(width):
        back = (width - 1 - k) * s
        src = full if back == 0 else pltpu.roll(full, back, axis=0)
        term = cw_ref[k:k + 1, cols] * src[pad:pad + tm, :]
        y = term if y is None else y + term
    return y


def _conv_body(x_ref, sh_ref, sc_ref, gt_ref, st0_ref, win_ref, cw_ref, wout_ref, g_ref, b_ref,
               y_ref, st_ref, abuf, acc, *, s, tm, cw, width, pad, alpha):
    d = x_ref.shape[-1]
    ks = (width - 1) * s
    t = pl.program_id(1)
    _load_history(t, abuf, st0_ref, pad, ks, tm)
    x = x_ref[...]
    u = (x * (1.0 + sc_ref[...]) + sh_ref[...]).astype(BF16)
    for c in range(d // cw):
        cols = slice(c * cw, (c + 1) * cw)
        h = _dot(u, win_ref[:, c * cw:(c + 1) * cw])
        cg = _dot(u, win_ref[:, 2 * d + c * cw:2 * d + (c + 1) * cw])
        abuf[pad:pad + tm, cols] = cg * h
        y = _conv_taps(abuf, cw_ref, cols, width, s, pad, tm)
        bg = _dot(u, win_ref[:, d + c * cw:d + (c + 1) * cw])
        z = _dot(u, win_ref[:, 3 * d + c * cw:3 * d + (c + 1) * cw])
        gated = (_silu(z) * bg * y).astype(BF16)
        part = _dot(gated, wout_ref[cols, :])
        if c == 0:
            acc[...] = part
        else:
            acc[...] += part
    st_ref[...] = abuf[pad + tm - ks:pad + tm, :]
    res = alpha * x + gt_ref[...] * acc[...]
    y_ref[...] = _layer_norm(res, g_ref[...], b_ref[...])


def _conv_layer(x, mod, layer, st0, w_in, conv_w, w_out, ln_g, ln_b, *, s, tm, alpha):
    bsz, t_rows, d = x.shape
    width = conv_w.shape[0]
    ks = (width - 1) * s
    pad = _round_up(ks, SUBLANES)
    cw = 256
    body = functools.partial(_conv_body, s=s, tm=tm, cw=cw, width=width, pad=pad, alpha=alpha)
    row_spec = pl.BlockSpec((None, tm, d), lambda b, t, *_: (b, t, 0))
    st_spec = pl.BlockSpec((None, ks, d), lambda b, t, *_: (b, 0, 0))
    const = lambda shape: pl.BlockSpec(shape, lambda b, t, *_: (0,) * len(shape))
    return _call(
        body, (bsz, t_rows // tm),
        [row_spec, _mod_spec(mod, layer, 0), _mod_spec(mod, layer, 1),
         _mod_spec(mod, layer, 2), st_spec,
         const(w_in.shape), const(conv_w.shape), const(w_out.shape),
         const((1, d)), const((1, d))],
        [row_spec, st_spec],
        [jax.ShapeDtypeStruct((bsz, t_rows, d), F32), jax.ShapeDtypeStruct((bsz, ks, d), F32)],
        [pltpu.VMEM((pad + tm, d), F32), pltpu.VMEM((tm, d), F32)],
        (x, mod, mod, mod, st0, w_in, conv_w, w_out, ln_g.reshape(1, d), ln_b.reshape(1, d)),
        "conv_layer")


def _lru_body(x_ref, sh_ref, sc_ref, gt_ref, st0_ref, h0_ref, win_ref, cw_ref, cb_ref,
              wga_ref, bga_ref, wgx_ref, bgx_ref, prm_ref, wout_ref, g_ref, b_ref,
              y_ref, st_ref, hl_ref, xbuf, a_s, b_s, h_s, hc, *, s, tm, width, pad, alpha):
    d = x_ref.shape[-1]
    nblk, blk, _ = wga_ref.shape
    ks = (width - 1) * s
    t = pl.program_id(1)
    _load_history(t, xbuf, st0_ref, pad, ks, tm)

    @pl.when(t == 0)
    def _():
        hc[...] = h0_ref[...]

    x = x_ref[...]
    u = (x * (1.0 + sc_ref[...]) + sh_ref[...]).astype(BF16)
    xbuf[pad:pad + tm, :] = _dot(u, win_ref[:, 0:d])
    st_ref[...] = xbuf[pad + tm - ks:pad + tm, :]
    for n in range(nblk):
        cols = slice(n * blk, (n + 1) * blk)
        xc = _conv_taps(xbuf, cw_ref, cols, width, s, pad, tm) + cb_ref[:, cols]
        xcb = xc.astype(BF16)
        r = jax.nn.sigmoid(_dot(xcb, wga_ref[n]) + bga_ref[:, cols])
        gi = jax.nn.sigmoid(_dot(xcb, wgx_ref[n]) + bgx_ref[:, cols])
        log_a = LRU_C * r * jax.nn.log_sigmoid(prm_ref[:, cols])
        a = jnp.exp(log_a)
        a_s[:, cols] = a
        b_s[:, cols] = jnp.sqrt(-jnp.tanh(log_a) * (a * a + 1.0)) * (gi * xc)

    steps = tm // s

    def step(i, h):
        r0 = pl.multiple_of(i * s, s)
        h = a_s[pl.ds(r0, s), :] * h + b_s[pl.ds(r0, s), :]
        h_s[pl.ds(r0, s), :] = h
        return h

    h_last = lax.fori_loop(0, steps, step, hc[...], unroll=True)
    hc[...] = h_last
    hl_ref[...] = h_last

    z = _dot(u, win_ref[:, d:2 * d])
    yy = (h_s[...] * _silu(z)).astype(BF16)
    out = _dot(yy, wout_ref[...])
    res = alpha * x + gt_ref[...] * out
    y_ref[...] = _layer_norm(res, g_ref[...], b_ref[...])


def _lru_layer(x, mod, layer, st0, h0, w_in, conv_w, conv_b, w_ga, b_ga, w_gx, b_gx, prm, w_out,
               ln_g, ln_b, *, s, tm, alpha):
    bsz, t_rows, d = x.shape
    width = conv_w.shape[0]
    ks = (width - 1) * s
    pad = _round_up(ks, SUBLANES)
    body = functools.partial(_lru_body, s=s, tm=tm, width=width, pad=pad, alpha=alpha)
    row_spec = pl.BlockSpec((None, tm, d), lambda b, t, *_: (b, t, 0))
    st_spec = pl.BlockSpec((None, ks, d), lambda b, t, *_: (b, 0, 0))
    h_spec = pl.BlockSpec((None, s, d), lambda b, t, *_: (b, 0, 0))
    const = lambda shape: pl.BlockSpec(shape, lambda b, t, *_: (0,) * len(shape))
    vec = const((1, d))
    return _call(
        body, (bsz, t_rows // tm),
        [row_spec, _mod_spec(mod, layer, 0), _mod_spec(mod, layer, 1),
         _mod_spec(mod, layer, 2), st_spec, h_spec,
         const(w_in.shape), const(conv_w.shape), vec,
         const(w_ga.shape), vec, const(w_gx.shape), vec, vec,
         const(w_out.shape), vec, vec],
        [row_spec, st_spec, h_spec],
        [jax.ShapeDtypeStruct((bsz, t_rows, d), F32), jax.ShapeDtypeStruct((bsz, ks, d), F32),
         jax.ShapeDtypeStruct((bsz, s, d), F32)],
        [pltpu.VMEM((pad + tm, d), F32), pltpu.VMEM((tm, d), F32), pltpu.VMEM((tm, d), F32),
         pltpu.VMEM((tm, d), F32), pltpu.VMEM((s, d), F32)],
        (x, mod, mod, mod, st0, h0, w_in, conv_w, conv_b.reshape(1, d), w_ga,
         b_ga.reshape(1, d), w_gx, b_gx.reshape(1, d), prm.reshape(1, d), w_out,
         ln_g.reshape(1, d), ln_b.reshape(1, d)),
        "lru_layer")


def _rope_block(xb, cos, sin_lo, sin_hi, half):
    return (xb * cos + pltpu.roll(xb, LANES - half, axis=1) * sin_lo
            + pltpu.roll(xb, half, axis=1) * sin_hi)


def _qkv_body(x_ref, sh_ref, sc_ref, cos_ref, slo_ref, shi_ref, win_ref,
              k_ref, v_ref, z_ref, qb_ref, *attn_refs, half, qscale):
    d = x_ref.shape[-1]
    x = x_ref[...]
    u = (x * (1.0 + sc_ref[...]) + sh_ref[...]).astype(BF16)
    cos, slo, shi = cos_ref[...], slo_ref[...], shi_ref[...]
    q = _dot(u, win_ref[:, 0:d])
    for j in range(d // LANES):
        cols = slice(j * LANES, (j + 1) * LANES)
        qb_ref[:, cols] = (_rope_block(q[:, cols], cos, slo, shi, half) * qscale).astype(BF16)
    k = _dot(u, win_ref[:, d:2 * d])
    for j in range(d // LANES):
        cols = slice(j * LANES, (j + 1) * LANES)
        kr = _rope_block(k[:, cols], cos, slo, shi, half)
        k_ref[:, cols] = kr
        if attn_refs:
            attn_refs[0][:, cols] = kr.astype(BF16)
    v = _dot(u, win_ref[:, 2 * d:3 * d])
    v_ref[...] = v
    if attn_refs:
        vt_ref = attn_refs[1]
        vd = vt_ref.shape[1]
        for hv in range(vt_ref.shape[0]):
            vt_ref[hv] = v[:, hv * vd:(hv + 1) * vd].T.astype(BF16)
    z_ref[...] = _dot(u, win_ref[:, 3 * d:4 * d]).astype(z_ref.dtype)


def _qkv_call(x, mod, layer, cos_t, slo_t, shi_t, w_in, *, tm, half, qscale, attn_vd=None):
    bsz, t_rows, d = x.shape
    body = functools.partial(_qkv_body, half=half, qscale=qscale)
    row_spec = pl.BlockSpec((None, tm, d), lambda b, t: (b, t, 0))
    tab_spec = pl.BlockSpec((tm, LANES), lambda b, t: (t, 0))
    f32_out = jax.ShapeDtypeStruct((bsz, t_rows, d), F32)
    bf_out = jax.ShapeDtypeStruct((bsz, t_rows, d), BF16)
    out_specs = [row_spec] * 4
    out_shape = [f32_out, f32_out, bf_out, bf_out]
    if attn_vd is not None:
        n_heads = d // attn_vd
        out_specs += [row_spec, pl.BlockSpec((None, n_heads, None, attn_vd, tm),
                                             lambda b, t: (b, 0, t, 0, 0))]
        out_shape += [bf_out,
                      jax.ShapeDtypeStruct((bsz, n_heads, t_rows // tm, attn_vd, tm), BF16)]
    return pl.pallas_call(
        body,
        grid=(bsz, t_rows // tm),
        in_specs=[row_spec, _mod_spec(mod, layer, 0), _mod_spec(mod, layer, 1),
                  tab_spec, tab_spec, tab_spec,
                  pl.BlockSpec(w_in.shape, lambda b, t: (0, 0))],
        out_specs=out_specs,
        out_shape=out_shape,
        compiler_params=_cparams(("arbitrary", "arbitrary")),
        name="attn_qkv",
    )(x, mod, mod, cos_t, slo_t, shi_t, w_in)


def _rope_tables(pos, head_dim, rot_dim):
    half = rot_dim // 2
    inv_freq = np.exp(np.arange(half, dtype=np.float64) * (-2.0 * math.log(ROPE_THETA) / rot_dim))
    ang = np.asarray(pos, dtype=np.float64)[:, None] * inv_freq[None, :]
    cos, sin = np.cos(ang), np.sin(ang)
    n = ang.shape[0]
    ones = np.ones((n, head_dim - rot_dim))
    zeros = np.zeros((n, head_dim - rot_dim))
    zh = np.zeros((n, half))
    cos_h = np.concatenate([cos, cos, ones], axis=1)
    slo_h = np.concatenate([-sin, zh, zeros], axis=1)
    shi_h = np.concatenate([zh, sin, zeros], axis=1)
    rep = LANES // head_dim
    return tuple(jnp.asarray(np.tile(t, (1, rep)), dtype=F32) for t in (cos_h, slo_h, shi_h))


def _diff_lambda(lq1_ref, lk1_ref, lq2_ref, lk2_ref, lam_init):
    s1 = jnp.sum(lq1_ref[...] * lk1_ref[...], axis=-1, keepdims=True)
    s2 = jnp.sum(lq2_ref[...] * lk2_ref[...], axis=-1, keepdims=True)
    return jnp.exp(s1) - jnp.exp(s2) + lam_init


def _flash_body(q_ref, k_ref, vt_ref, lq1_ref, lk1_ref, lq2_ref, lk2_ref, o_ref,
                qq_s, sa_s, sb_s, m_s, acc_s, *, tq, head_dim, nh, lam_init):
    qi = pl.program_id(2)
    vd, tk = vt_ref.shape[2], vt_ref.shape[3]
    nd = tq // tk
    lane = lax.broadcasted_iota(jnp.int32, (tq, LANES), 1)
    for h in range(nh):
        q = q_ref[:, h * LANES:(h + 1) * LANES]
        zero = jnp.zeros_like(q)
        qq_s[h, 0:tq, :] = jnp.where(lane < head_dim, q, zero)
        qq_s[h, tq:2 * tq, :] = jnp.where(lane >= head_dim, q, zero)
    m_s[...] = jnp.full(m_s.shape, -jnp.inf, F32)
    acc_s[...] = jnp.zeros(acc_s.shape, F32)
    ones = jnp.ones((2 * SUBLANES, tk), BF16)

    def scores(j, s_ref, q0=0):
        r0 = pl.multiple_of(j * tk, tk)
        for h in range(nh):
            k = k_ref[pl.ds(r0, tk), h * LANES:(h + 1) * LANES]
            if q0 == 0:
                s_ref[h] = _dot_nt(k, qq_s[h])
            else:
                qq = jnp.concatenate([qq_s[h, q0:tq, :], qq_s[h, tq + q0:2 * tq, :]], axis=0)
                s_ref[h, :, 0:2 * (tq - q0)] = _dot_nt(k, qq)

    def consume(j, s_ref, diag=False, q0=0):
        w = tq - q0
        lo, hi = slice(q0, tq), slice(tq + q0, 2 * tq)
        for h in range(nh):
            if q0 == 0:
                st, m_prev, acc_prev = s_ref[h], m_s[h], acc_s[h]
            else:
                st = s_ref[h, :, 0:2 * w]
                m_prev = jnp.concatenate([m_s[h, :, lo], m_s[h, :, hi]], axis=1)
                acc_prev = jnp.concatenate([acc_s[h, :, lo], acc_s[h, :, hi]], axis=1)
            if diag:
                key = lax.broadcasted_iota(jnp.int32, st.shape, 0)
                row = lax.broadcasted_iota(jnp.int32, st.shape, 1)
                row = jnp.where(row >= w, row - w, row)
                st = jnp.where(key <= row, st, -jnp.inf)
            m_new = jnp.maximum(m_prev, jnp.max(st, axis=0, keepdims=True))
            alpha = jnp.exp2(m_prev - m_new)
            pt = jnp.exp2(st - m_new).astype(BF16)
            v1 = jnp.concatenate([vt_ref[h, j], ones], axis=0)
            acc_new = alpha * acc_prev + _dot(v1, pt)
            if q0 == 0:
                acc_s[h] = acc_new
                m_s[h] = m_new
            else:
                acc_s[h, :, lo] = acc_new[:, 0:w]
                acc_s[h, :, hi] = acc_new[:, w:2 * w]
                m_s[h, :, lo] = m_new[:, 0:w]
                m_s[h, :, hi] = m_new[:, w:2 * w]

    scores(0, sa_s)

    def pair(i, carry):
        j = 2 * i
        scores(j + 1, sb_s)
        consume(j, sa_s)
        scores(j + 2, sa_s)
        consume(j + 1, sb_s)
        return carry

    lax.fori_loop(0, qi * (nd // 2), pair, 0)

    bufs = (sa_s, sb_s)
    for g in range(nd):
        j = nd * qi + g
        if g + 1 < nd:
            scores(j + 1, bufs[(g + 1) % 2], q0=(g + 1) * tk)
        consume(j, bufs[g % 2], diag=True, q0=g * tk)

    lam = _diff_lambda(lq1_ref, lk1_ref, lq2_ref, lk2_ref, lam_init)
    for h in range(nh):
        acc = acc_s[h]
        ot = acc[0:vd, :] / acc[vd:vd + 1, :]
        o_ref[:, h * LANES:(h + 1) * LANES] = (
            (ot[:, 0:tq] - lam * ot[:, tq:2 * tq]).T.astype(o_ref.dtype))


def _flash_call(qb, kb, vt, lq1, lk1, lq2, lk2, *, tq, head_dim, nh, lam_init):
    bsz, t_rows, d = qb.shape
    _, n_heads, n_blk, vd, tk = vt.shape
    assert tq % (2 * tk) == 0 and n_blk * tk == t_rows and vd == LANES
    body = functools.partial(_flash_body, tq=tq, head_dim=head_dim, nh=nh, lam_init=lam_init)
    q_spec = pl.BlockSpec((None, tq, nh * LANES), lambda b, h, i, *_: (b, i, h))
    k_spec = pl.BlockSpec((None, t_rows, nh * LANES), lambda b, h, i, *_: (b, 0, h))
    vt_spec = pl.BlockSpec((None, nh, n_blk, vd, tk), lambda b, h, i, *_: (b, h, 0, 0, 0))
    l_spec = pl.BlockSpec((1, head_dim), lambda b, h, i, *_: (0, 0))
    return _call(
        body, (bsz, n_heads // nh, t_rows // tq),
        [q_spec, k_spec, vt_spec, l_spec, l_spec, l_spec, l_spec],
        q_spec,
        jax.ShapeDtypeStruct((bsz, t_rows, d), BF16),
        [pltpu.VMEM((nh, 2 * tq, LANES), BF16), pltpu.VMEM((nh, tk, 2 * tq), F32),
         pltpu.VMEM((nh, tk, 2 * tq), F32), pltpu.VMEM((nh, 1, 2 * tq), F32),
         pltpu.VMEM((nh, vd + 2 * SUBLANES, 2 * tq), F32)],
        (qb, kb, vt, lq1, lk1, lq2, lk2),
        "attn_flash")


class _DecodeConfig:
    def __init__(self, page_table, qs, cache_kt, cache_vr, layer, k_new, v_new, lvec, *, first,
                 gp, n_heads, lam_init):
        self.pt = page_table.reshape(-1)
        self.n_pages = page_table.shape[1]
        self.qs, self.kt, self.vr, self.layer = qs, cache_kt, cache_vr, layer
        self.k_new, self.v_new, self.lvec = k_new, v_new, list(lvec)
        self.first, self.gp, self.n_heads, self.lam_init = first, gp, n_heads, lam_init
        self.kd, self.page = cache_kt.shape[2], cache_kt.shape[3]
        self.vd = cache_vr.shape[-1]
        self.t_new = k_new.shape[1]
        self.head_dim = lvec[0].shape[-1]
        self.n_sub = self.kd // self.head_dim
        self.rows = self.n_sub * self.t_new
        self.spb = self.n_pages // gp

    def count(self, grid):
        assert len(grid) == 2 and grid[1] == self.spb
        return grid[0]

    def in_specs(self, grid):
        first = self.first
        self.count(grid)

        def per_seq(shape):
            return pl.BlockSpec((None,) + shape, lambda b, p, pt: (first + b, 0, 0))

        l_spec = pl.BlockSpec((1, self.head_dim), lambda *a: (0, 0))
        in_hbm = pl.BlockSpec(memory_space=pl.ANY)
        return [per_seq((self.t_new, self.kd)), in_hbm, in_hbm,
                per_seq((self.t_new, self.kd)), per_seq((self.t_new * self.n_heads, self.vd)),
                l_spec, l_spec, l_spec, l_spec]

    def args(self):
        return [self.qs, self.kt, self.vr, self.k_new, self.v_new] + self.lvec

    def out_spec(self, grid):
        return pl.BlockSpec((None, self.t_new, self.n_heads * self.vd),
                            lambda b, p, pt: (b, 0, 0))

    def out_shape(self, grid):
        return jax.ShapeDtypeStruct((self.count(grid), self.t_new, self.n_heads * self.vd), F32)

    def scratch_shapes(self):
        return [pltpu.VMEM((self.rows, self.kd), F32), pltpu.VMEM((self.rows, self.kd), BF16),
                pltpu.VMEM((self.rows, 1), F32), pltpu.VMEM((self.rows, 1), F32),
                pltpu.VMEM((self.rows, self.vd), F32),
                pltpu.VMEM((self.page, self.kd), F32),
                pltpu.VMEM((self.page * self.n_heads, self.vd), F32),
                pltpu.VMEM((PAGE_BUFFERS, self.gp, self.kd, self.page), F32),
                pltpu.VMEM((PAGE_BUFFERS, self.gp, self.page * self.n_heads, self.vd), F32),
                pltpu.SemaphoreType.DMA((PAGE_BUFFERS, 2))]


class _DecodeStep:
    def __init__(self, cfg, grid, pt_ref, in_refs, out_ref, scratch_refs):
        self.c = cfg
        self.pt_ref = pt_ref
        (self.q_ref, self.kt_hbm, self.vr_hbm, self.kn_ref, self.vn_ref) = in_refs[:5]
        self.l_refs = in_refs[5:]
        self.o_ref = out_ref
        (self.qf_s, self.qx_s, self.m_s, self.l_s, self.acc_s, self.kpad, self.vpad,
         self.kbuf, self.vbuf, self.sem) = scratch_refs
        self.b = pl.program_id(0)
        self.p = pl.program_id(1)
        self.n_steps = grid[0] * grid[1]
        self.step = self.b * grid[1] + self.p
        slot = lax.rem(self.step, PAGE_BUFFERS)
        self.k_refs = [self.kbuf.at[slot, g] for g in range(cfg.gp)]
        self.v_refs = [self.vbuf.at[slot, g] for g in range(cfg.gp)]

    def _page_copies(self, step):
        c = self.c
        slot = step % PAGE_BUFFERS if isinstance(step, int) else lax.rem(step, PAGE_BUFFERS)
        copies = []
        for g in range(c.gp):
            page = self.pt_ref[c.first * c.n_pages + step * c.gp + g]
            copies.append(pltpu.make_async_copy(self.kt_hbm.at[c.layer, page],
                                                self.kbuf.at[slot, g], self.sem.at[slot, 0]))
            copies.append(pltpu.make_async_copy(self.vr_hbm.at[c.layer, page],
                                                self.vbuf.at[slot, g], self.sem.at[slot, 1]))
        return copies

    def preamble(self):
        c = self.c
        ahead = PAGE_BUFFERS - 1

        @pl.when(self.step == 0)
        def _():
            for s in range(min(ahead, self.n_steps)):
                for cp in self._page_copies(s):
                    cp.start()

        @pl.when(self.step + ahead < self.n_steps)
        def _():
            for cp in self._page_copies(self.step + ahead):
                cp.start()

        for cp in self._page_copies(self.step):
            cp.wait()

        @pl.when((self.b == 0) & (self.p == 0))
        def _():
            self.kpad[...] = jnp.zeros(self.kpad.shape, F32)
            self.vpad[...] = jnp.zeros(self.vpad.shape, F32)

        @pl.when(self.p == 0)
        def _():
            self.m_s[...] = jnp.full(self.m_s.shape, -jnp.inf, F32)
            self.l_s[...] = jnp.zeros(self.l_s.shape, F32)
            self.acc_s[...] = jnp.zeros(self.acc_s.shape, F32)
            q = self.q_ref[...].astype(F32)
            lane = lax.broadcasted_iota(jnp.int32, q.shape, 1)
            for h in range(c.n_sub):
                own = (lane >= h * c.head_dim) & (lane < (h + 1) * c.head_dim)
                self.qf_s[h * c.t_new:(h + 1) * c.t_new, :] = jnp.where(own, q, 0.0)
            self.qx_s[...] = self.qf_s[...].astype(BF16)

    def _update(self, sc, vs):
        c = self.c
        m_prev = self.m_s[...]
        m_new = jnp.maximum(m_prev, jnp.max(sc, axis=1, keepdims=True))
        alpha = jnp.exp(m_prev - m_new)
        pe = jnp.exp(sc - m_new)
        self.l_s[...] = alpha * self.l_s[...] + jnp.sum(pe, axis=1, keepdims=True)
        self.m_s[...] = m_new
        pb = pe.astype(BF16)
        grp = 2 * c.t_new
        for hv in range(c.n_heads):
            rs = slice(hv * grp, (hv + 1) * grp)
            vh = jnp.concatenate(
                [v[pl.ds(hv, c.page, stride=c.n_heads), :].astype(BF16) for v in vs], axis=0)
            self.acc_s[rs, :] = alpha[rs, :] * self.acc_s[rs, :] + _dot(pb[rs, :], vh)

    def main(self):
        kt = jnp.concatenate([k[...].astype(BF16) for k in self.k_refs], axis=1)
        self._update(_dot(self.qx_s[...], kt), self.v_refs)

    def finalize(self):
        c = self.c

        @pl.when(self.p == c.spb - 1)
        def _():
            self.kpad[0:c.t_new, :] = self.kn_ref[...]
            self.vpad[0:c.t_new * c.n_heads, :] = self.vn_ref[...]
            sc = _dot_nt(self.qx_s[...], self.kpad[...].astype(BF16))
            row = lax.broadcasted_iota(jnp.int32, sc.shape, 0)
            col = lax.broadcasted_iota(jnp.int32, sc.shape, 1)
            self._update(jnp.where(col <= row % c.t_new, sc, -jnp.inf), [self.vpad])
            o = self.acc_s[...] / self.l_s[...]
            lam = _diff_lambda(*self.l_refs, c.lam_init)
            grp = 2 * c.t_new
            for hv in range(c.n_heads):
                o1 = o[hv * grp:hv * grp + c.t_new, :]
                o2 = o[hv * grp + c.t_new:(hv + 1) * grp, :]
                self.o_ref[:, hv * c.vd:(hv + 1) * c.vd] = o1 - lam * o2


def _decode_call(cfg, n_seq):
    grid = (n_seq, cfg.spb)
    in_specs = cfg.in_specs(grid)
    n_in = len(in_specs)

    def body(pt_ref, *refs):
        step = _DecodeStep(cfg, grid, pt_ref, refs[:n_in], refs[n_in], refs[n_in + 1:])
        step.preamble()
        step.main()
        step.finalize()

    grid_spec = pltpu.PrefetchScalarGridSpec(
        num_scalar_prefetch=1, grid=grid, in_specs=in_specs, out_specs=cfg.out_spec(grid),
        scratch_shapes=cfg.scratch_shapes())
    return pl.pallas_call(body, grid_spec=grid_spec, out_shape=cfg.out_shape(grid),
                          compiler_params=_cparams(("arbitrary", "arbitrary")),
                          name="attn_decode")(cfg.pt, *cfg.args())


def _call(body, grid, in_specs, out_specs, out_shape, scratch_shapes, args, name):
    return pl.pallas_call(body, grid=grid, in_specs=in_specs, out_specs=out_specs,
                          out_shape=out_shape, scratch_shapes=scratch_shapes,
                          compiler_params=_cparams(("arbitrary",) * len(grid)), name=name)(*args)


def _attn_out_body(x_ref, gt_ref, o_ref, z_ref, sg_ref, wout_ref, g_ref, b_ref, y_ref, gbuf, *,
                   alpha, out_scale):
    d = x_ref.shape[-1]
    vd = sg_ref.shape[-1]
    for hv in range(d // vd):
        cols = slice(hv * vd, (hv + 1) * vd)
        of = o_ref[:, cols].astype(F32)
        of = of * lax.rsqrt(jnp.mean(of * of, axis=-1, keepdims=True) + SUBLN_EPS)
        of = of * sg_ref[...] * out_scale
        gbuf[:, cols] = (of * _silu(z_ref[:, cols].astype(F32))).astype(BF16)
    out = _dot(gbuf[...], wout_ref[...])
    res = alpha * x_ref[...] + gt_ref[...] * out
    y_ref[...] = _layer_norm(res, g_ref[...], b_ref[...])


def _attn_out_call(x, mod, layer, o, z, subln_g, w_out, ln_g, ln_b, *, tm, alpha, out_scale):
    bsz, t_rows, d = x.shape
    vd = subln_g.shape[-1]
    body = functools.partial(_attn_out_body, alpha=alpha, out_scale=out_scale)
    row_spec = pl.BlockSpec((None, tm, d), lambda b, t: (b, t, 0))
    const = lambda shape: pl.BlockSpec(shape, lambda b, t: (0,) * len(shape))
    return pl.pallas_call(
        body,
        grid=(bsz, t_rows // tm),
        in_specs=[row_spec, _mod_spec(mod, layer, 2), row_spec, row_spec, const((1, vd)),
                  const(w_out.shape),
                  const((1, d)), const((1, d))],
        out_specs=row_spec,
        out_shape=jax.ShapeDtypeStruct((bsz, t_rows, d), F32),
        scratch_shapes=[pltpu.VMEM((tm, d), BF16)],
        compiler_params=_cparams(("arbitrary", "arbitrary")),
        name="attn_out",
    )(x, mod, o, z, subln_g.reshape(1, vd), w_out, ln_g.reshape(1, d), ln_b.reshape(1, d))


def kernel(x_prompt, x_sample, state_conv_a, state_lru_h, state_lru_conv, cache_k, cache_v, page_table, c_prompt, c_sample, w_ada, b_ada, ln_g, ln_b, a_w_in, a_conv_w, a_w_out, r_w_in, r_conv_w, r_conv_b, r_w_ga, r_b_ga, r_w_gx, r_b_gx, r_lru_param, r_w_out, d_w_in, d_lq1, d_lk1, d_lq2, d_lk2, d_subln_g, d_w_out):
    bp, tp, d = x_prompt.shape
    bs, ts, _ = x_sample.shape
    depth = w_ada.shape[0]
    n_pages = page_table.shape[1]
    page = cache_k.shape[2]
    n_sub, head_dim = cache_k.shape[3], cache_k.shape[4]
    n_heads, vd = cache_v.shape[3], cache_v.shape[4]
    past_len = n_pages * page
    rot_dim = head_dim // 4
    alpha = (2 * depth) ** 0.25
    rows_s = bs * ts

    n_c = _round_up(bp + bs, SUBLANES)
    c_all = jnp.concatenate([c_sample, c_prompt, jnp.zeros((n_c - bp - bs, d), F32)], axis=0)
    mod_p, mod_s = _ada_call(c_all, w_ada, b_ada, bp=bp, bs=bs, ts=ts)

    def to_time_major(a):
        return jnp.swapaxes(a, 0, 1).reshape((1, a.shape[1] * bs) + a.shape[2:])

    def from_time_major(a, n):
        return jnp.swapaxes(a.reshape(n, bs, a.shape[-1]), 0, 1)

    bf = lambda w: w.astype(BF16)
    half = rot_dim // 2
    qscale = head_dim ** -0.5

    conv_p, conv_s, lruh_p, lruh_s, lruc_p, lruc_s = [], [], [], [], [], []
    k_p, v_p, k_s, v_s = [], [], [], []

    def layer_weights(i):
        kind, j = i % N_MIXERS, i // N_MIXERS
        if kind == 0:
            return (bf(a_w_in[j]), a_conv_w[j], bf(a_w_out[j]), ln_g[i], ln_b[i])
        if kind == 1:
            return (bf(r_w_in[j]), r_conv_w[j], r_conv_b[j], bf(r_w_ga[j]), r_b_ga[j],
                    bf(r_w_gx[j]), r_b_gx[j], r_lru_param[j], bf(r_w_out[j]), ln_g[i], ln_b[i])
        return (bf(d_w_in[j]), bf(d_w_out[j]),
                [v[j].reshape(1, head_dim) for v in (d_lq1, d_lk1, d_lq2, d_lk2)],
                0.8 - 0.6 * math.exp(-0.3 * i))

    xp = x_prompt
    xs = to_time_major(x_sample)
    for i in range(depth):
        kind, j = i % N_MIXERS, i // N_MIXERS
        w = layer_weights(i)
        if kind == 0:
            width = a_conv_w.shape[1]
            xp, nbp = _conv_layer(xp, mod_p, i, jnp.zeros((bp, width - 1, d), F32), *w,
                                  s=1, tm=512, alpha=alpha)
            xs, nbs = _conv_layer(xs, mod_s, i, to_time_major(state_conv_a[j]), *w,
                                  s=bs, tm=rows_s, alpha=alpha)
            conv_p.append(nbp)
            conv_s.append(from_time_major(nbs, width - 1))
        elif kind == 1:
            width = r_conv_w.shape[1]
            xp, nbp, hp = _lru_layer(xp, mod_p, i, jnp.zeros((bp, width - 1, d), F32),
                                     jnp.zeros((bp, 1, d), F32), *w, s=1, tm=256, alpha=alpha)
            xs, nbs, hs = _lru_layer(xs, mod_s, i, to_time_major(state_lru_conv[j]),
                                     state_lru_h[j].reshape(1, bs, d), *w, s=bs, tm=rows_s,
                                     alpha=alpha)
            lruh_p.append(hp.reshape(bp, d))
            lruh_s.append(hs.reshape(bs, d))
            lruc_p.append(nbp)
            lruc_s.append(from_time_major(nbs, width - 1))
        else:
            w_in, w_out, lvec, lam_init = w
            tabs_p = _rope_tables(np.arange(tp), head_dim, rot_dim)
            kp, vp, zp, qb, kb, vt = _qkv_call(xp, mod_p, i, *tabs_p, w_in, tm=256, half=half,
                                               qscale=qscale * math.log2(math.e), attn_vd=vd)
            op = _flash_call(qb, kb, vt, *lvec, tq=512, head_dim=head_dim, nh=4,
                             lam_init=lam_init)
            xp = _attn_out_call(xp, mod_p, i, op, zp, d_subln_g[j], w_out, ln_g[i], ln_b[i],
                                tm=512, alpha=alpha, out_scale=1.0 - lam_init)
            k_p.append(kp.reshape(bp, tp, n_sub, head_dim))
            v_p.append(vp.reshape(bp, tp, n_heads, vd))
            pos_s = past_len + np.repeat(np.arange(ts), bs)
            tabs_s = _rope_tables(pos_s, head_dim, rot_dim)
            ks_, vs_, zs, qsb = _qkv_call(xs, mod_s, i, *tabs_s, w_in, tm=rows_s, half=half,
                                          qscale=qscale)
            ksn = from_time_major(ks_, ts)
            vsn = from_time_major(vs_, ts)
            n_layers, n_phys = cache_k.shape[0], cache_k.shape[1]
            cache_kt = jnp.transpose(cache_k, (0, 1, 3, 4, 2)).reshape(
                n_layers, n_phys, n_sub * head_dim, page)
            cache_vr = cache_v.reshape(n_layers, n_phys, page * n_heads, vd)
            cfg = _DecodeConfig(page_table, from_time_major(qsb, ts), cache_kt, cache_vr, j, ksn,
                                vsn.reshape(bs, ts * n_heads, vd), lvec, first=0, gp=16,
                                n_heads=n_heads, lam_init=lam_init)
            os_ = _decode_call(cfg, bs)
            xs = _attn_out_call(xs, mod_s, i, to_time_major(os_), zs, d_subln_g[j], w_out,
                                ln_g[i], ln_b[i], tm=rows_s, alpha=alpha,
                                out_scale=1.0 - lam_init)
            k_s.append(ksn.reshape(bs, ts, n_sub, head_dim))
            v_s.append(vsn.reshape(bs, ts, n_heads, vd))

    return (xp, from_time_major(xs, ts),
            jnp.stack(conv_p), jnp.stack(conv_s),
            jnp.stack(lruh_p), jnp.stack(lruh_s),
            jnp.stack(lruc_p), jnp.stack(lruc_s),
            jnp.stack(k_p), jnp.stack(v_p), jnp.stack(k_s), jnp.stack(v_s))
```

```python
import functools
import math

import jax
import jax.numpy as jnp
import numpy as np
from jax import lax
from jax.experimental import pallas as pl
from jax.experimental.pallas import tpu as pltpu

F32 = jnp.float32
BF16 = jnp.bfloat16

LN_EPS = 1e-5
SUBLN_EPS = 1e-5
LRU_C = 8.0
ROPE_THETA = 500000.0
N_MIXERS = 3

SUBLANES = 8
LANES = 128
VMEM_LIMIT_BYTES = 56 * 1024 * 1024
PAGE_BUFFERS = 2


def _cparams(semantics):
    return pltpu.CompilerParams(dimension_semantics=semantics, vmem_limit_bytes=VMEM_LIMIT_BYTES)


def _dot(a, b):
    return jnp.dot(a, b, preferred_element_type=F32)


def _dot_nt(a, b):
    return lax.dot_general(a, b, (((1,), (1,)), ((), ())), preferred_element_type=F32)


def _silu(z):
    return z * jax.nn.sigmoid(z)


def _layer_norm(y, g, b):
    mu = jnp.mean(y, axis=-1, keepdims=True)
    yc = y - mu
    var = jnp.mean(yc * yc, axis=-1, keepdims=True)
    return yc * lax.rsqrt(var + LN_EPS) * g + b


def _round_up(n, m):
    return (n + m - 1) // m * m


def _ada_body(c_ref, w_ref, b_ref, op_ref, os_ref, *, bp, bs, ts):
    c = c_ref[...]
    a = _silu(c).astype(BF16)
    m = _dot(a, w_ref[...].astype(BF16)) + b_ref[...]
    for r in range(bp):
        op_ref[r] = m[bs + r:bs + r + 1, :]
    for t in range(ts):
        os_ref[t * bs:(t + 1) * bs, :] = m[0:bs, :]


def _ada_call(c_all, w_ada, b_ada, *, bp, bs, ts):
    depth, d, d3 = w_ada.shape
    rows = c_all.shape[0]
    nt = d3 // d
    return pl.pallas_call(
        functools.partial(_ada_body, bp=bp, bs=bs, ts=ts),
        grid=(depth, nt),
        in_specs=[
            pl.BlockSpec((rows, d), lambda i, n: (0, 0)),
            pl.BlockSpec((None, d, d), lambda i, n: (i, 0, n)),
            pl.BlockSpec((None, 1, d), lambda i, n: (i, 0, n)),
        ],
        out_specs=[pl.BlockSpec((None, None, bp, 1, d), lambda i, n: (i, n, 0, 0, 0)),
                   pl.BlockSpec((None, None, None, ts * bs, d), lambda i, n: (i, n, 0, 0, 0))],
        out_shape=[jax.ShapeDtypeStruct((depth, nt, bp, 1, d), F32),
                   jax.ShapeDtypeStruct((depth, nt, 1, ts * bs, d), F32)],
        compiler_params=_cparams(("arbitrary", "arbitrary")),
        name="adaln",
    )(c_all, w_ada, b_ada.reshape(depth, 1, d3))


def _mod_spec(mod, layer, k):
    _, _, _, r, d = mod.shape
    return pl.BlockSpec((None, None, None, r, d), lambda b, t, *_: (layer, k, b, 0, 0))


def _load_history(t, buf, st0_ref, pad, ks, tm):
    @pl.when(t == 0)
    def _():
        buf[pad - ks:pad, :] = st0_ref[...]

    @pl.when(t > 0)
    def _():
        buf[pad - ks:pad, :] = buf[pad + tm - ks:pad + tm, :]


def _conv_taps(buf, cw_ref, cols, width, s, pad, tm):
    y = None
    if s % SUBLANES == 0:
        for k in range(width):
            r0 = pad - (width - 1 - k) * s
            term = cw_ref[k:k + 1, cols] * buf[r0:r0 + tm, cols]
            y = term if y is None else y + term
        return y
    full = buf[0:pad + tm, cols]
    for k in range(width):
        back = (width - 1 - k) * s
        src = full if back == 0 else pltpu.roll(full, back, axis=0)
        term = cw_ref[k:k + 1, cols] * src[pad:pad + tm, :]
        y = term if y is None else y + term
    return y


def _conv_body(x_ref, sh_ref, sc_ref, gt_ref, st0_ref, win_ref, cw_ref, wout_ref, g_ref, b_ref,
               y_ref, st_ref, abuf, acc, *, s, tm, cw, width, pad, alpha):
    d = x_ref.shape[-1]
    ks = (width - 1) * s
    t = pl.program_id(1)
    _load_history(t, abuf, st0_ref, pad, ks, tm)
    x = x_ref[...]
    u = (x * (1.0 + sc_ref[...]) + sh_ref[...]).astype(BF16)
    for c in range(d // cw):
        cols = slice(c * cw, (c + 1) * cw)
        h = _dot(u, win_ref[:, c * cw:(c + 1) * cw])
        cg = _dot(u, win_ref[:, 2 * d + c * cw:2 * d + (c + 1) * cw])
        abuf[pad:pad + tm, cols] = cg * h
        y = _conv_taps(abuf, cw_ref, cols, width, s, pad, tm)
        bg = _dot(u, win_ref[:, d + c * cw:d + (c + 1) * cw])
        z = _dot(u, win_ref[:, 3 * d + c * cw:3 * d + (c + 1) * cw])
        gated = (_silu(z) * bg * y).astype(BF16)
        part = _dot(gated, wout_ref[cols, :])
        if c == 0:
            acc[...] = part
        else:
            acc[...] += part
    st_ref[...] = abuf[pad + tm - ks:pad + tm, :]
    res = alpha * x + gt_ref[...] * acc[...]
    y_ref[...] = _layer_norm(res, g_ref[...], b_ref[...])


def _conv_layer(x, mod, layer, st0, w_in, conv_w, w_out, ln_g, ln_b, *, s, tm, alpha):
    bsz, t_rows, d = x.shape
    width = conv_w.shape[0]
    ks = (width - 1) * s
    pad = _round_up(ks, SUBLANES)
    cw = 256
    body = functools.partial(_conv_body, s=s, tm=tm, cw=cw, width=width, pad=pad, alpha=alpha)
    row_spec = pl.BlockSpec((None, tm, d), lambda b, t, *_: (b, t, 0))
    st_spec = pl.BlockSpec((None, ks, d), lambda b, t, *_: (b, 0, 0))
    const = lambda shape: pl.BlockSpec(shape, lambda b, t, *_: (0,) * len(shape))
    return _call(
        body, (bsz, t_rows // tm),
        [row_spec, _mod_spec(mod, layer, 0), _mod_spec(mod, layer, 1),
         _mod_spec(mod, layer, 2), st_spec,
         const(w_in.shape), const(conv_w.shape), const(w_out.shape),
         const((1, d)), const((1, d))],
        [row_spec, st_spec],
        [jax.ShapeDtypeStruct((bsz, t_rows, d), F32), jax.ShapeDtypeStruct((bsz, ks, d), F32)],
        [pltpu.VMEM((pad + tm, d), F32), pltpu.VMEM((tm, d), F32)],
        (x, mod, mod, mod, st0, w_in, conv_w, w_out, ln_g.reshape(1, d), ln_b.reshape(1, d)),
        "conv_layer")


def _lru_body(x_ref, sh_ref, sc_ref, gt_ref, st0_ref, h0_ref, win_ref, cw_ref, cb_ref,
              wga_ref, bga_ref, wgx_ref, bgx_ref, prm_ref, wout_ref, g_ref, b_ref,
              y_ref, st_ref, hl_ref, xbuf, a_s, b_s, h_s, hc, *, s, tm, width, pad, alpha):
    d = x_ref.shape[-1]
    nblk, blk, _ = wga_ref.shape
    ks = (width - 1) * s
    t = pl.program_id(1)
    _load_history(t, xbuf, st0_ref, pad, ks, tm)

    @pl.when(t == 0)
    def _():
        hc[...] = h0_ref[...]

    x = x_ref[...]
    u = (x * (1.0 + sc_ref[...]) + sh_ref[...]).astype(BF16)
    xbuf[pad:pad + tm, :] = _dot(u, win_ref[:, 0:d])
    st_ref[...] = xbuf[pad + tm - ks:pad + tm, :]
    for n in range(nblk):
        cols = slice(n * blk, (n + 1) * blk)
        xc = _conv_taps(xbuf, cw_ref, cols, width, s, pad, tm) + cb_ref[:, cols]
        xcb = xc.astype(BF16)
        r = jax.nn.sigmoid(_dot(xcb, wga_ref[n]) + bga_ref[:, cols])
        gi = jax.nn.sigmoid(_dot(xcb, wgx_ref[n]) + bgx_ref[:, cols])
        log_a = LRU_C * r * jax.nn.log_sigmoid(prm_ref[:, cols])
        a = jnp.exp(log_a)
        a_s[:, cols] = a
        b_s[:, cols] = jnp.sqrt(-jnp.tanh(log_a) * (a * a + 1.0)) * (gi * xc)

    steps = tm // s

    def step(i, h):
        r0 = pl.multiple_of(i * s, s)
        h = a_s[pl.ds(r0, s), :] * h + b_s[pl.ds(r0, s), :]
        h_s[pl.ds(r0, s), :] = h
        return h

    h_last = lax.fori_loop(0, steps, step, hc[...], unroll=True)
    hc[...] = h_last
    hl_ref[...] = h_last

    z = _dot(u, win_ref[:, d:2 * d])
    yy = (h_s[...] * _silu(z)).astype(BF16)
    out = _dot(yy, wout_ref[...])
    res = alpha * x + gt_ref[...] * out
    y_ref[...] = _layer_norm(res, g_ref[...], b_ref[...])


def _lru_layer(x, mod, layer, st0, h0, w_in, conv_w, conv_b, w_ga, b_ga, w_gx, b_gx, prm, w_out,
               ln_g, ln_b, *, s, tm, alpha):
    bsz, t_rows, d = x.shape
    width = conv_w.shape[0]
    ks = (width - 1) * s
    pad = _round_up(ks, SUBLANES)
    body = functools.partial(_lru_body, s=s, tm=tm, width=width, pad=pad, alpha=alpha)
    row_spec = pl.BlockSpec((None, tm, d), lambda b, t, *_: (b, t, 0))
    st_spec = pl.BlockSpec((None, ks, d), lambda b, t, *_: (b, 0, 0))
    h_spec = pl.BlockSpec((None, s, d), lambda b, t, *_: (b, 0, 0))
    const = lambda shape: pl.BlockSpec(shape, lambda b, t, *_: (0,) * len(shape))
    vec = const((1, d))
    return _call(
        body, (bsz, t_rows // tm),
        [row_spec, _mod_spec(mod, layer, 0), _mod_spec(mod, layer, 1),
         _mod_spec(mod, layer, 2), st_spec, h_spec,
         const(w_in.shape), const(conv_w.shape), vec,
         const(w_ga.shape), vec, const(w_gx.shape), vec, vec,
         const(w_out.shape), vec, vec],
        [row_spec, st_spec, h_spec],
        [jax.ShapeDtypeStruct((bsz, t_rows, d), F32), jax.ShapeDtypeStruct((bsz, ks, d), F32),
         jax.ShapeDtypeStruct((bsz, s, d), F32)],
        [pltpu.VMEM((pad + tm, d), F32), pltpu.VMEM((tm, d), F32), pltpu.VMEM((tm, d), F32),
         pltpu.VMEM((tm, d), F32), pltpu.VMEM((s, d), F32)],
        (x, mod, mod, mod, st0, h0, w_in, conv_w, conv_b.reshape(1, d), w_ga,
         b_ga.reshape(1, d), w_gx, b_gx.reshape(1, d), prm.reshape(1, d), w_out,
         ln_g.reshape(1, d), ln_b.reshape(1, d)),
        "lru_layer")


def _rope_block(xb, cos, sin_lo, sin_hi, half):
    return (xb * cos + pltpu.roll(xb, LANES - half, axis=1) * sin_lo
            + pltpu.roll(xb, half, axis=1) * sin_hi)


def _qkv_body(x_ref, sh_ref, sc_ref, cos_ref, slo_ref, shi_ref, win_ref,
              k_ref, v_ref, z_ref, qb_ref, *attn_refs, half, qscale):
    d = x_ref.shape[-1]
    x = x_ref[...]
    u = (x * (1.0 + sc_ref[...]) + sh_ref[...]).astype(BF16)
    cos, slo, shi = cos_ref[...], slo_ref[...], shi_ref[...]
    q = _dot(u, win_ref[:, 0:d])
    for j in range(d // LANES):
        cols = slice(j * LANES, (j + 1) * LANES)
        qb_ref[:, cols] = (_rope_block(q[:, cols], cos, slo, shi, half) * qscale).astype(BF16)
    k = _dot(u, win_ref[:, d:2 * d])
    for j in range(d // LANES):
        cols = slice(j * LANES, (j + 1) * LANES)
        kr = _rope_block(k[:, cols], cos, slo, shi, half)
        k_ref[:, cols] = kr
        if attn_refs:
            attn_refs[0][:, cols] = kr.astype(BF16)
    v = _dot(u, win_ref[:, 2 * d:3 * d])
    v_ref[...] = v
    if attn_refs:
        vt_ref = attn_refs[1]
        vd = vt_ref.shape[1]
        for hv in range(vt_ref.shape[0]):
            vt_ref[hv] = v[:, hv * vd:(hv + 1) * vd].T.astype(BF16)
    z_ref[...] = _dot(u, win_ref[:, 3 * d:4 * d]).astype(z_ref.dtype)


def _qkv_call(x, mod, layer, cos_t, slo_t, shi_t, w_in, *, tm, half, qscale, attn_vd=None):
    bsz, t_rows, d = x.shape
    body = functools.partial(_qkv_body, half=half, qscale=qscale)
    row_spec = pl.BlockSpec((None, tm, d), lambda b, t: (b, t, 0))
    tab_spec = pl.BlockSpec((tm, LANES), lambda b, t: (t, 0))
    f32_out = jax.ShapeDtypeStruct((bsz, t_rows, d), F32)
    bf_out = jax.ShapeDtypeStruct((bsz, t_rows, d), BF16)
    out_specs = [row_spec] * 4
    out_shape = [f32_out, f32_out, bf_out, bf_out]
    if attn_vd is not None:
        n_heads = d // attn_vd
        out_specs += [row_spec, pl.BlockSpec((None, n_heads, None, attn_vd, tm),
                                             lambda b, t: (b, 0, t, 0, 0))]
        out_shape += [bf_out,
                      jax.ShapeDtypeStruct((bsz, n_heads, t_rows // tm, attn_vd, tm), BF16)]
    return pl.pallas_call(
        body,
        grid=(bsz, t_rows // tm),
        in_specs=[row_spec, _mod_spec(mod, layer, 0), _mod_spec(mod, layer, 1),
                  tab_spec, tab_spec, tab_spec,
                  pl.BlockSpec(w_in.shape, lambda b, t: (0, 0))],
        out_specs=out_specs,
        out_shape=out_shape,
        compiler_params=_cparams(("arbitrary", "arbitrary")),
        name="attn_qkv",
    )(x, mod, mod, cos_t, slo_t, shi_t, w_in)


def _rope_tables(pos, head_dim, rot_dim):
    half = rot_dim // 2
    inv_freq = np.exp(np.arange(half, dtype=np.float64) * (-2.0 * math.log(ROPE_THETA) / rot_dim))
    ang = np.asarray(pos, dtype=np.float64)[:, None] * inv_freq[None, :]
    cos, sin = np.cos(ang), np.sin(ang)
    n = ang.shape[0]
    ones = np.ones((n, head_dim - rot_dim))
    zeros = np.zeros((n, head_dim - rot_dim))
    zh = np.zeros((n, half))
    cos_h = np.concatenate([cos, cos, ones], axis=1)
    slo_h = np.concatenate([-sin, zh, zeros], axis=1)
    shi_h = np.concatenate([zh, sin, zeros], axis=1)
    rep = LANES // head_dim
    return tuple(jnp.asarray(np.tile(t, (1, rep)), dtype=F32) for t in (cos_h, slo_h, shi_h))


def _diff_lambda(lq1_ref, lk1_ref, lq2_ref, lk2_ref, lam_init):
    s1 = jnp.sum(lq1_ref[...] * lk1_ref[...], axis=-1, keepdims=True)
    s2 = jnp.sum(lq2_ref[...] * lk2_ref[...], axis=-1, keepdims=True)
    return jnp.exp(s1) - jnp.exp(s2) + lam_init


def _flash_body(q_ref, k_ref, vt_ref, lq1_ref, lk1_ref, lq2_ref, lk2_ref, o_ref,
                qq_s, sa_s, sb_s, m_s, acc_s, *, tq, head_dim, nh, lam_init):
    qi = pl.program_id(2)
    vd, tk = vt_ref.shape[2], vt_ref.shape[3]
    nd = tq // tk
    lane = lax.broadcasted_iota(jnp.int32, (tq, LANES), 1)
    for h in range(nh):
        q = q_ref[:, h * LANES:(h + 1) * LANES]
        zero = jnp.zeros_like(q)
        qq_s[h, 0:tq, :] = jnp.where(lane < head_dim, q, zero)
        qq_s[h, tq:2 * tq, :] = jnp.where(lane >= head_dim, q, zero)
    m_s[...] = jnp.full(m_s.shape, -jnp.inf, F32)
    acc_s[...] = jnp.zeros(acc_s.shape, F32)
    ones = jnp.ones((2 * SUBLANES, tk), BF16)

    def scores(j, s_ref, q0=0):
        r0 = pl.multiple_of(j * tk, tk)
        for h in range(nh):
            k = k_ref[pl.ds(r0, tk), h * LANES:(h + 1) * LANES]
            if q0 == 0:
                s_ref[h] = _dot_nt(k, qq_s[h])
            else:
                qq = jnp.concatenate([qq_s[h, q0:tq, :], qq_s[h, tq + q0:2 * tq, :]], axis=0)
                s_ref[h, :, 0:2 * (tq - q0)] = _dot_nt(k, qq)

    def consume(j, s_ref, diag=False, q0=0):
        w = tq - q0
        for h in range(nh):
            v1 = jnp.concatenate([vt_ref[h, j], ones], axis=0)
            for sub in range(2):
                for cc in range(w // tk):
                    src = slice(sub * w + cc * tk, sub * w + (cc + 1) * tk)
                    dst = slice(sub * tq + q0 + cc * tk, sub * tq + q0 + (cc + 1) * tk)
                    st = s_ref[h, :, src]
                    if diag and cc == 0:
                        key = lax.broadcasted_iota(jnp.int32, st.shape, 0)
                        row = lax.broadcasted_iota(jnp.int32, st.shape, 1)
                        st = jnp.where(key <= row, st, -jnp.inf)
                    m_prev = m_s[h, :, dst]
                    m_new = jnp.maximum(m_prev, jnp.max(st, axis=0, keepdims=True))
                    alpha = jnp.exp2(m_prev - m_new)
                    pt = jnp.exp2(st - m_new).astype(BF16)
                    acc_s[h, :, dst] = alpha * acc_s[h, :, dst] + _dot(v1, pt)
                    m_s[h, :, dst] = m_new

    scores(0, sa_s)

    def pair(i, carry):
        j = 2 * i
        scores(j + 1, sb_s)
        consume(j, sa_s)
        scores(j + 2, sa_s)
        consume(j + 1, sb_s)
        return carry

    lax.fori_loop(0, qi * (nd // 2), pair, 0)

    bufs = (sa_s, sb_s)
    for g in range(nd):
        j = nd * qi + g
        if g + 1 < nd:
            scores(j + 1, bufs[(g + 1) % 2], q0=(g + 1) * tk)
        consume(j, bufs[g % 2], diag=True, q0=g * tk)

    lam = _diff_lambda(lq1_ref, lk1_ref, lq2_ref, lk2_ref, lam_init)
    for h in range(nh):
        acc = acc_s[h]
        ot = acc[0:vd, :] / acc[vd:vd + 1, :]
        o_ref[:, h * LANES:(h + 1) * LANES] = (
            (ot[:, 0:tq] - lam * ot[:, tq:2 * tq]).T.astype(o_ref.dtype))


def _flash_call(qb, kb, vt, lq1, lk1, lq2, lk2, *, tq, head_dim, nh, lam_init):
    bsz, t_rows, d = qb.shape
    _, n_heads, n_blk, vd, tk = vt.shape
    assert tq % (2 * tk) == 0 and n_blk * tk == t_rows and vd == LANES
    body = functools.partial(_flash_body, tq=tq, head_dim=head_dim, nh=nh, lam_init=lam_init)
    q_spec = pl.BlockSpec((None, tq, nh * LANES), lambda b, h, i, *_: (b, i, h))
    k_spec = pl.BlockSpec((None, t_rows, nh * LANES), lambda b, h, i, *_: (b, 0, h))
    vt_spec = pl.BlockSpec((None, nh, n_blk, vd, tk), lambda b, h, i, *_: (b, h, 0, 0, 0))
    l_spec = pl.BlockSpec((1, head_dim), lambda b, h, i, *_: (0, 0))
    return _call(
        body, (bsz, n_heads // nh, t_rows // tq),
        [q_spec, k_spec, vt_spec, l_spec, l_spec, l_spec, l_spec],
        q_spec,
        jax.ShapeDtypeStruct((bsz, t_rows, d), BF16),
        [pltpu.VMEM((nh, 2 * tq, LANES), BF16), pltpu.VMEM((nh, tk, 2 * tq), F32),
         pltpu.VMEM((nh, tk, 2 * tq), F32), pltpu.VMEM((nh, 1, 2 * tq), F32),
         pltpu.VMEM((nh, vd + 2 * SUBLANES, 2 * tq), F32)],
        (qb, kb, vt, lq1, lk1, lq2, lk2),
        "attn_flash")


class _DecodeConfig:
    def __init__(self, page_table, qs, cache_kt, cache_vr, layer, k_new, v_new, lvec, *, first,
                 gp, n_heads, lam_init):
        self.pt = page_table.reshape(-1)
        self.n_pages = page_table.shape[1]
        self.qs, self.kt, self.vr, self.layer = qs, cache_kt, cache_vr, layer
        self.k_new, self.v_new, self.lvec = k_new, v_new, list(lvec)
        self.first, self.gp, self.n_heads, self.lam_init = first, gp, n_heads, lam_init
        self.kd, self.page = cache_kt.shape[2], cache_kt.shape[3]
        self.vd = cache_vr.shape[-1]
        self.t_new = k_new.shape[1]
        self.head_dim = lvec[0].shape[-1]
        self.n_sub = self.kd // self.head_dim
        self.rows = self.n_sub * self.t_new
        self.spb = self.n_pages // gp

    def count(self, grid):
        assert len(grid) == 2 and grid[1] == self.spb
        return grid[0]

    def in_specs(self, grid):
        first = self.first
        self.count(grid)

        def per_seq(shape):
            return pl.BlockSpec((None,) + shape, lambda b, p, pt: (first + b, 0, 0))

        l_spec = pl.BlockSpec((1, self.head_dim), lambda *a: (0, 0))
        in_hbm = pl.BlockSpec(memory_space=pl.ANY)
        return [per_seq((self.t_new, self.kd)), in_hbm, in_hbm,
                per_seq((self.t_new, self.kd)), per_seq((self.t_new * self.n_heads, self.vd)),
                l_spec, l_spec, l_spec, l_spec]

    def args(self):
        return [self.qs, self.kt, self.vr, self.k_new, self.v_new] + self.lvec

    def out_spec(self, grid):
        return pl.BlockSpec((None, self.t_new, self.n_heads * self.vd),
                            lambda b, p, pt: (b, 0, 0))

    def out_shape(self, grid):
        return jax.ShapeDtypeStruct((self.count(grid), self.t_new, self.n_heads * self.vd), F32)

    def scratch_shapes(self):
        return [pltpu.VMEM((self.rows, self.kd), F32), pltpu.VMEM((self.rows, self.kd), BF16),
                pltpu.VMEM((self.rows, 1), F32), pltpu.VMEM((self.rows, 1), F32),
                pltpu.VMEM((self.rows, self.vd), F32),
                pltpu.VMEM((self.page, self.kd), F32),
                pltpu.VMEM((self.page * self.n_heads, self.vd), F32),
                pltpu.VMEM((PAGE_BUFFERS, self.gp, self.kd, self.page), F32),
                pltpu.VMEM((PAGE_BUFFERS, self.gp, self.page * self.n_heads, self.vd), F32),
                pltpu.SemaphoreType.DMA((PAGE_BUFFERS, 2))]


class _DecodeStep:
    def __init__(self, cfg, grid, pt_ref, in_refs, out_ref, scratch_refs):
        self.c = cfg
        self.pt_ref = pt_ref
        (self.q_ref, self.kt_hbm, self.vr_hbm, self.kn_ref, self.vn_ref) = in_refs[:5]
        self.l_refs = in_refs[5:]
        self.o_ref = out_ref
        (self.qf_s, self.qx_s, self.m_s, self.l_s, self.acc_s, self.kpad, self.vpad,
         self.kbuf, self.vbuf, self.sem) = scratch_refs
        self.b = pl.program_id(0)
        self.p = pl.program_id(1)
        self.n_steps = grid[0] * grid[1]
        self.step = self.b * grid[1] + self.p
        slot = lax.rem(self.step, PAGE_BUFFERS)
        self.k_refs = [self.kbuf.at[slot, g] for g in range(cfg.gp)]
        self.v_refs = [self.vbuf.at[slot, g] for g in range(cfg.gp)]

    def _page_copies(self, step):
        c = self.c
        slot = step % PAGE_BUFFERS if isinstance(step, int) else lax.rem(step, PAGE_BUFFERS)
        copies = []
        for g in range(c.gp):
            page = self.pt_ref[c.first * c.n_pages + step * c.gp + g]
            copies.append(pltpu.make_async_copy(self.kt_hbm.at[c.layer, page],
                                                self.kbuf.at[slot, g], self.sem.at[slot, 0]))
            copies.append(pltpu.make_async_copy(self.vr_hbm.at[c.layer, page],
                                                self.vbuf.at[slot, g], self.sem.at[slot, 1]))
        return copies

    def preamble(self):
        c = self.c
        ahead = PAGE_BUFFERS - 1

        @pl.when(self.step == 0)
        def _():
            for s in range(min(ahead, self.n_steps)):
                for cp in self._page_copies(s):
                    cp.start()

        @pl.when(self.step + ahead < self.n_steps)
        def _():
            for cp in self._page_copies(self.step + ahead):
                cp.start()

        for cp in self._page_copies(self.step):
            cp.wait()

        @pl.when((self.b == 0) & (self.p == 0))
        def _():
            self.kpad[...] = jnp.zeros(self.kpad.shape, F32)
            self.vpad[...] = jnp.zeros(self.vpad.shape, F32)

        @pl.when(self.p == 0)
        def _():
            self.m_s[...] = jnp.full(self.m_s.shape, -jnp.inf, F32)
            self.l_s[...] = jnp.zeros(self.l_s.shape, F32)
            self.acc_s[...] = jnp.zeros(self.acc_s.shape, F32)
            q = self.q_ref[...].astype(F32)
            lane = lax.broadcasted_iota(jnp.int32, q.shape, 1)
            for h in range(c.n_sub):
                own = (lane >= h * c.head_dim) & (lane < (h + 1) * c.head_dim)
                self.qf_s[h * c.t_new:(h + 1) * c.t_new, :] = jnp.where(own, q, 0.0)
            self.qx_s[...] = self.qf_s[...].astype(BF16)

    def _update(self, sc, vs):
        c = self.c
        m_prev = self.m_s[...]
        m_new = jnp.maximum(m_prev, jnp.max(sc, axis=1, keepdims=True))
        alpha = jnp.exp(m_prev - m_new)
        pe = jnp.exp(sc - m_new)
        self.l_s[...] = alpha * self.l_s[...] + jnp.sum(pe, axis=1, keepdims=True)
        self.m_s[...] = m_new
        pb = pe.astype(BF16)
        grp = 2 * c.t_new
        for hv in range(c.n_heads):
            rs = slice(hv * grp, (hv + 1) * grp)
            vh = jnp.concatenate(
                [v[pl.ds(hv, c.page, stride=c.n_heads), :].astype(BF16) for v in vs], axis=0)
            self.acc_s[rs, :] = alpha[rs, :] * self.acc_s[rs, :] + _dot(pb[rs, :], vh)

    def main(self):
        kt = jnp.concatenate([k[...].astype(BF16) for k in self.k_refs], axis=1)
        self._update(_dot(self.qx_s[...], kt), self.v_refs)

    def finalize(self):
        c = self.c

        @pl.when(self.p == c.spb - 1)
        def _():
            self.kpad[0:c.t_new, :] = self.kn_ref[...]
            self.vpad[0:c.t_new * c.n_heads, :] = self.vn_ref[...]
            sc = _dot_nt(self.qx_s[...], self.kpad[...].astype(BF16))
            row = lax.broadcasted_iota(jnp.int32, sc.shape, 0)
            col = lax.broadcasted_iota(jnp.int32, sc.shape, 1)
            self._update(jnp.where(col <= row % c.t_new, sc, -jnp.inf), [self.vpad])
            o = self.acc_s[...] / self.l_s[...]
            lam = _diff_lambda(*self.l_refs, c.lam_init)
            grp = 2 * c.t_new
            for hv in range(c.n_heads):
                o1 = o[hv * grp:hv * grp + c.t_new, :]
                o2 = o[hv * grp + c.t_new:(hv + 1) * grp, :]
                self.o_ref[:, hv * c.vd:(hv + 1) * c.vd] = o1 - lam * o2


def _decode_call(cfg, n_seq):
    grid = (n_seq, cfg.spb)
    in_specs = cfg.in_specs(grid)
    n_in = len(in_specs)

    def body(pt_ref, *refs):
        step = _DecodeStep(cfg, grid, pt_ref, refs[:n_in], refs[n_in], refs[n_in + 1:])
        step.preamble()
        step.main()
        step.finalize()

    grid_spec = pltpu.PrefetchScalarGridSpec(
        num_scalar_prefetch=1, grid=grid, in_specs=in_specs, out_specs=cfg.out_spec(grid),
        scratch_shapes=cfg.scratch_shapes())
    return pl.pallas_call(body, grid_spec=grid_spec, out_shape=cfg.out_shape(grid),
                          compiler_params=_cparams(("arbitrary", "arbitrary")),
                          name="attn_decode")(cfg.pt, *cfg.args())


def _call(body, grid, in_specs, out_specs, out_shape, scratch_shapes, args, name):
    return pl.pallas_call(body, grid=grid, in_specs=in_specs, out_specs=out_specs,
                          out_shape=out_shape, scratch_shapes=scratch_shapes,
                          compiler_params=_cparams(("arbitrary",) * len(grid)), name=name)(*args)


def _attn_out_body(x_ref, gt_ref, o_ref, z_ref, sg_ref, wout_ref, g_ref, b_ref, y_ref, gbuf, *,
                   alpha, out_scale):
    d = x_ref.shape[-1]
    vd = sg_ref.shape[-1]
    for hv in range(d // vd):
        cols = slice(hv * vd, (hv + 1) * vd)
        of = o_ref[:, cols].astype(F32)
        of = of * lax.rsqrt(jnp.mean(of * of, axis=-1, keepdims=True) + SUBLN_EPS)
        of = of * sg_ref[...] * out_scale
        gbuf[:, cols] = (of * _silu(z_ref[:, cols].astype(F32))).astype(BF16)
    out = _dot(gbuf[...], wout_ref[...])
    res = alpha * x_ref[...] + gt_ref[...] * out
    y_ref[...] = _layer_norm(res, g_ref[...], b_ref[...])


def _attn_out_call(x, mod, layer, o, z, subln_g, w_out, ln_g, ln_b, *, tm, alpha, out_scale):
    bsz, t_rows, d = x.shape
    vd = subln_g.shape[-1]
    body = functools.partial(_attn_out_body, alpha=alpha, out_scale=out_scale)
    row_spec = pl.BlockSpec((None, tm, d), lambda b, t: (b, t, 0))
    const = lambda shape: pl.BlockSpec(shape, lambda b, t: (0,) * len(shape))
    return pl.pallas_call(
        body,
        grid=(bsz, t_rows // tm),
        in_specs=[row_spec, _mod_spec(mod, layer, 2), row_spec, row_spec, const((1, vd)),
                  const(w_out.shape),
                  const((1, d)), const((1, d))],
        out_specs=row_spec,
        out_shape=jax.ShapeDtypeStruct((bsz, t_rows, d), F32),
        scratch_shapes=[pltpu.VMEM((tm, d), BF16)],
        compiler_params=_cparams(("arbitrary", "arbitrary")),
        name="attn_out",
    )(x, mod, o, z, subln_g.reshape(1, vd), w_out, ln_g.reshape(1, d), ln_b.reshape(1, d))


def kernel(x_prompt, x_sample, state_conv_a, state_lru_h, state_lru_conv, cache_k, cache_v, page_table, c_prompt, c_sample, w_ada, b_ada, ln_g, ln_b, a_w_in, a_conv_w, a_w_out, r_w_in, r_conv_w, r_conv_b, r_w_ga, r_b_ga, r_w_gx, r_b_gx, r_lru_param, r_w_out, d_w_in, d_lq1, d_lk1, d_lq2, d_lk2, d_subln_g, d_w_out):
    bp, tp, d = x_prompt.shape
    bs, ts, _ = x_sample.shape
    depth = w_ada.shape[0]
    n_pages = page_table.shape[1]
    page = cache_k.shape[2]
    n_sub, head_dim = cache_k.shape[3], cache_k.shape[4]
    n_heads, vd = cache_v.shape[3], cache_v.shape[4]
    past_len = n_pages * page
    rot_dim = head_dim // 4
    alpha = (2 * depth) ** 0.25
    rows_s = bs * ts

    n_c = _round_up(bp + bs, SUBLANES)
    c_all = jnp.concatenate([c_sample, c_prompt, jnp.zeros((n_c - bp - bs, d), F32)], axis=0)
    mod_p, mod_s = _ada_call(c_all, w_ada, b_ada, bp=bp, bs=bs, ts=ts)

    def to_time_major(a):
        return jnp.swapaxes(a, 0, 1).reshape((1, a.shape[1] * bs) + a.shape[2:])

    def from_time_major(a, n):
        return jnp.swapaxes(a.reshape(n, bs, a.shape[-1]), 0, 1)

    bf = lambda w: w.astype(BF16)
    half = rot_dim // 2
    qscale = head_dim ** -0.5

    conv_p, conv_s, lruh_p, lruh_s, lruc_p, lruc_s = [], [], [], [], [], []
    k_p, v_p, k_s, v_s = [], [], [], []

    def layer_weights(i):
        kind, j = i % N_MIXERS, i // N_MIXERS
        if kind == 0:
            return (bf(a_w_in[j]), a_conv_w[j], bf(a_w_out[j]), ln_g[i], ln_b[i])
        if kind == 1:
            return (bf(r_w_in[j]), r_conv_w[j], r_conv_b[j], bf(r_w_ga[j]), r_b_ga[j],
                    bf(r_w_gx[j]), r_b_gx[j], r_lru_param[j], bf(r_w_out[j]), ln_g[i], ln_b[i])
        return (bf(d_w_in[j]), bf(d_w_out[j]),
                [v[j].reshape(1, head_dim) for v in (d_lq1, d_lk1, d_lq2, d_lk2)],
                0.8 - 0.6 * math.exp(-0.3 * i))

    xp = x_prompt
    xs = to_time_major(x_sample)
    for i in range(depth):
        kind, j = i % N_MIXERS, i // N_MIXERS
        w = layer_weights(i)
        if kind == 0:
            width = a_conv_w.shape[1]
            xp, nbp = _conv_layer(xp, mod_p, i, jnp.zeros((bp, width - 1, d), F32), *w,
                                  s=1, tm=512, alpha=alpha)
            xs, nbs = _conv_layer(xs, mod_s, i, to_time_major(state_conv_a[j]), *w,
                                  s=bs, tm=rows_s, alpha=alpha)
            conv_p.append(nbp)
            conv_s.append(from_time_major(nbs, width - 1))
        elif kind == 1:
            width = r_conv_w.shape[1]
            xp, nbp, hp = _lru_layer(xp, mod_p, i, jnp.zeros((bp, width - 1, d), F32),
                                     jnp.zeros((bp, 1, d), F32), *w, s=1, tm=256, alpha=alpha)
            xs, nbs, hs = _lru_layer(xs, mod_s, i, to_time_major(state_lru_conv[j]),
                                     state_lru_h[j].reshape(1, bs, d), *w, s=bs, tm=rows_s,
                                     alpha=alpha)
            lruh_p.append(hp.reshape(bp, d))
            lruh_s.append(hs.reshape(bs, d))
            lruc_p.append(nbp)
            lruc_s.append(from_time_major(nbs, width - 1))
        else:
            w_in, w_out, lvec, lam_init = w
            tabs_p = _rope_tables(np.arange(tp), head_dim, rot_dim)
            kp, vp, zp, qb, kb, vt = _qkv_call(xp, mod_p, i, *tabs_p, w_in, tm=256, half=half,
                                               qscale=qscale * math.log2(math.e), attn_vd=vd)
            op = _flash_call(qb, kb, vt, *lvec, tq=1024, head_dim=head_dim, nh=4,
                             lam_init=lam_init)
            xp = _attn_out_call(xp, mod_p, i, op, zp, d_subln_g[j], w_out, ln_g[i], ln_b[i],
                                tm=512, alpha=alpha, out_scale=1.0 - lam_init)
            k_p.append(kp.reshape(bp, tp, n_sub, head_dim))
            v_p.append(vp.reshape(bp, tp, n_heads, vd))
            pos_s = past_len + np.repeat(np.arange(ts), bs)
            tabs_s = _rope_tables(pos_s, head_dim, rot_dim)
            ks_, vs_, zs, qsb = _qkv_call(xs, mod_s, i, *tabs_s, w_in, tm=rows_s, half=half,
                                          qscale=qscale)
            ksn = from_time_major(ks_, ts)
            vsn = from_time_major(vs_, ts)
            n_layers, n_phys = cache_k.shape[0], cache_k.shape[1]
            cache_kt = jnp.transpose(cache_k, (0, 1, 3, 4, 2)).reshape(
                n_layers, n_phys, n_sub * head_dim, page)
            cache_vr = cache_v.reshape(n_layers, n_phys, page * n_heads, vd)
            cfg = _DecodeConfig(page_table, from_time_major(qsb, ts), cache_kt, cache_vr, j, ksn,
                                vsn.reshape(bs, ts * n_heads, vd), lvec, first=0, gp=16,
                                n_heads=n_heads, lam_init=lam_init)
            os_ = _decode_call(cfg, bs)
            xs = _attn_out_call(xs, mod_s, i, to_time_major(os_), zs, d_subln_g[j], w_out,
                                ln_g[i], ln_b[i], tm=rows_s, alpha=alpha,
                                out_scale=1.0 - lam_init)
            k_s.append(ksn.reshape(bs, ts, n_sub, head_dim))
            v_s.append(vsn.reshape(bs, ts, n_heads, vd))

    return (xp, from_time_major(xs, ts),
            jnp.stack(conv_p), jnp.stack(conv_s),
            jnp.stack(lruh_p), jnp.stack(lruh_s),
            jnp.stack(lruc_p), jnp.stack(lruc_s),
            jnp.stack(k_p), jnp.stack(v_p), jnp.stack(k_s), jnp.stack(v_s))
```

```python
import functools
import math

import jax
import jax.numpy as jnp
import numpy as np
from jax import lax
from jax.experimental import pallas as pl
from jax.experimental.pallas import tpu as pltpu

F32 = jnp.float32
BF16 = jnp.bfloat16

LN_EPS = 1e-5
SUBLN_EPS = 1e-5
LRU_C = 8.0
ROPE_THETA = 500000.0
N_MIXERS = 3

SUBLANES = 8
LANES = 128
VMEM_LIMIT_BYTES = 56 * 1024 * 1024
PAGE_BUFFERS = 2


def _cparams(semantics):
    return pltpu.CompilerParams(dimension_semantics=semantics, vmem_limit_bytes=VMEM_LIMIT_BYTES)


def _dot(a, b):
    return jnp.dot(a, b, preferred_element_type=F32)


def _dot_nt(a, b):
    return lax.dot_general(a, b, (((1,), (1,)), ((), ())), preferred_element_type=F32)


def _silu(z):
    return z * jax.nn.sigmoid(z)


def _layer_norm(y, g, b):
    mu = jnp.mean(y, axis=-1, keepdims=True)
    yc = y - mu
    var = jnp.mean(yc * yc, axis=-1, keepdims=True)
    return yc * lax.rsqrt(var + LN_EPS) * g + b


def _round_up(n, m):
    return (n + m - 1) // m * m


def _ada_body(c_ref, w_ref, b_ref, op_ref, os_ref, *, bp, bs, ts):
    c = c_ref[...]
    a = _silu(c).astype(BF16)
    m = _dot(a, w_ref[...].astype(BF16)) + b_ref[...]
    for r in range(bp):
        op_ref[r] = m[bs + r:bs + r + 1, :]
    for t in range(ts):
        os_ref[t * bs:(t + 1) * bs, :] = m[0:bs, :]


def _ada_call(c_all, w_ada, b_ada, *, bp, bs, ts):
    depth, d, d3 = w_ada.shape
    rows = c_all.shape[0]
    nt = d3 // d
    return pl.pallas_call(
        functools.partial(_ada_body, bp=bp, bs=bs, ts=ts),
        grid=(depth, nt),
        in_specs=[
            pl.BlockSpec((rows, d), lambda i, n: (0, 0)),
            pl.BlockSpec((None, d, d), lambda i, n: (i, 0, n)),
            pl.BlockSpec((None, 1, d), lambda i, n: (i, 0, n)),
        ],
        out_specs=[pl.BlockSpec((None, None, bp, 1, d), lambda i, n: (i, n, 0, 0, 0)),
                   pl.BlockSpec((None, None, None, ts * bs, d), lambda i, n: (i, n, 0, 0, 0))],
        out_shape=[jax.ShapeDtypeStruct((depth, nt, bp, 1, d), F32),
                   jax.ShapeDtypeStruct((depth, nt, 1, ts * bs, d), F32)],
        compiler_params=_cparams(("arbitrary", "arbitrary")),
        name="adaln",
    )(c_all, w_ada, b_ada.reshape(depth, 1, d3))


def _mod_spec(mod, layer, k):
    _, _, _, r, d = mod.shape
    return pl.BlockSpec((None, None, None, r, d), lambda b, t, *_: (layer, k, b, 0, 0))


def _load_history(t, buf, st0_ref, pad, ks, tm):
    @pl.when(t == 0)
    def _():
        buf[pad - ks:pad, :] = st0_ref[...]

    @pl.when(t > 0)
    def _():
        buf[pad - ks:pad, :] = buf[pad + tm - ks:pad + tm, :]


def _conv_taps(buf, cw_ref, cols, width, s, pad, tm):
    y = None
    if s % SUBLANES == 0:
        for k in range(width):
            r0 = pad - (width - 1 - k) * s
            term = cw_ref[k:k + 1, cols] * buf[r0:r0 + tm, cols]
            y = term if y is None else y + term
        return y
    full = buf[0:pad + tm, cols]
    for k in range(width):
        back = (width - 1 - k) * s
        src = full if back == 0 else pltpu.roll(full, back, axis=0)
        term = cw_ref[k:k + 1, cols] * src[pad:pad + tm, :]
        y = term if y is None else y + term
    return y


def _conv_body(x_ref, sh_ref, sc_ref, gt_ref, st0_ref, win_ref, cw_ref, wout_ref, g_ref, b_ref,
               y_ref, st_ref, abuf, acc, *, s, tm, cw, width, pad, alpha):
    d = x_ref.shape[-1]
    ks = (width - 1) * s
    t = pl.program_id(1)
    _load_history(t, abuf, st0_ref, pad, ks, tm)
    x = x_ref[...]
    u = (x * (1.0 + sc_ref[...]) + sh_ref[...]).astype(BF16)
    for c in range(d // cw):
        cols = slice(c * cw, (c + 1) * cw)
        h = _dot(u, win_ref[:, c * cw:(c + 1) * cw])
        cg = _dot(u, win_ref[:, 2 * d + c * cw:2 * d + (c + 1) * cw])
        abuf[pad:pad + tm, cols] = cg * h
        y = _conv_taps(abuf, cw_ref, cols, width, s, pad, tm)
        bg = _dot(u, win_ref[:, d + c * cw:d + (c + 1) * cw])
        z = _dot(u, win_ref[:, 3 * d + c * cw:3 * d + (c + 1) * cw])
        gated = (_silu(z) * bg * y).astype(BF16)
        part = _dot(gated, wout_ref[cols, :])
        if c == 0:
            acc[...] = part
        else:
            acc[...] += part
    st_ref[...] = abuf[pad + tm - ks:pad + tm, :]
    res = alpha * x + gt_ref[...] * acc[...]
    y_ref[...] = _layer_norm(res, g_ref[...], b_ref[...])


def _conv_layer(x, mod, layer, st0, w_in, conv_w, w_out, ln_g, ln_b, *, s, tm, alpha):
    bsz, t_rows, d = x.shape
    width = conv_w.shape[0]
    ks = (width - 1) * s
    pad = _round_up(ks, SUBLANES)
    cw = 256
    body = functools.partial(_conv_body, s=s, tm=tm, cw=cw, width=width, pad=pad, alpha=alpha)
    row_spec = pl.BlockSpec((None, tm, d), lambda b, t, *_: (b, t, 0))
    st_spec = pl.BlockSpec((None, ks, d), lambda b, t, *_: (b, 0, 0))
    const = lambda shape: pl.BlockSpec(shape, lambda b, t, *_: (0,) * len(shape))
    return _call(
        body, (bsz, t_rows // tm),
        [row_spec, _mod_spec(mod, layer, 0), _mod_spec(mod, layer, 1),
         _mod_spec(mod, layer, 2), st_spec,
         const(w_in.shape), const(conv_w.shape), const(w_out.shape),
         const((1, d)), const((1, d))],
        [row_spec, st_spec],
        [jax.ShapeDtypeStruct((bsz, t_rows, d), F32), jax.ShapeDtypeStruct((bsz, ks, d), F32)],
        [pltpu.VMEM((pad + tm, d), F32), pltpu.VMEM((tm, d), F32)],
        (x, mod, mod, mod, st0, w_in, conv_w, w_out, ln_g.reshape(1, d), ln_b.reshape(1, d)),
        "conv_layer")


def _lru_body(x_ref, sh_ref, sc_ref, gt_ref, st0_ref, h0_ref, win_ref, cw_ref, cb_ref,
              wga_ref, bga_ref, wgx_ref, bgx_ref, prm_ref, wout_ref, g_ref, b_ref,
              y_ref, st_ref, hl_ref, xbuf, a_s, b_s, h_s, hc, *, s, tm, width, pad, alpha):
    d = x_ref.shape[-1]
    nblk, blk, _ = wga_ref.shape
    ks = (width - 1) * s
    t = pl.program_id(1)
    _load_history(t, xbuf, st0_ref, pad, ks, tm)

    @pl.when(t == 0)
    def _():
        hc[...] = h0_ref[...]

    x = x_ref[...]
    u = (x * (1.0 + sc_ref[...]) + sh_ref[...]).astype(BF16)
    xbuf[pad:pad + tm, :] = _dot(u, win_ref[:, 0:d])
    st_ref[...] = xbuf[pad + tm - ks:pad + tm, :]
    for n in range(nblk):
        cols = slice(n * blk, (n + 1) * blk)
        xc = _conv_taps(xbuf, cw_ref, cols, width, s, pad, tm) + cb_ref[:, cols]
        xcb = xc.astype(BF16)
        r = jax.nn.sigmoid(_dot(xcb, wga_ref[n]) + bga_ref[:, cols])
        gi = jax.nn.sigmoid(_dot(xcb, wgx_ref[n]) + bgx_ref[:, cols])
        log_a = LRU_C * r * jax.nn.log_sigmoid(prm_ref[:, cols])
        a = jnp.exp(log_a)
        a_s[:, cols] = a
        b_s[:, cols] = jnp.sqrt(-jnp.tanh(log_a) * (a * a + 1.0)) * (gi * xc)

    steps = tm // s

    def step(i, h):
        r0 = pl.multiple_of(i * s, s)
        h = a_s[pl.ds(r0, s), :] * h + b_s[pl.ds(r0, s), :]
        h_s[pl.ds(r0, s), :] = h
        return h

    h_last = lax.fori_loop(0, steps, step, hc[...], unroll=True)
    hc[...] = h_last
    hl_ref[...] = h_last

    z = _dot(u, win_ref[:, d:2 * d])
    yy = (h_s[...] * _silu(z)).astype(BF16)
    out = _dot(yy, wout_ref[...])
    res = alpha * x + gt_ref[...] * out
    y_ref[...] = _layer_norm(res, g_ref[...], b_ref[...])


def _lru_layer(x, mod, layer, st0, h0, w_in, conv_w, conv_b, w_ga, b_ga, w_gx, b_gx, prm, w_out,
               ln_g, ln_b, *, s, tm, alpha):
    bsz, t_rows, d = x.shape
    width = conv_w.shape[0]
    ks = (width - 1) * s
    pad = _round_up(ks, SUBLANES)
    body = functools.partial(_lru_body, s=s, tm=tm, width=width, pad=pad, alpha=alpha)
    row_spec = pl.BlockSpec((None, tm, d), lambda b, t, *_: (b, t, 0))
    st_spec = pl.BlockSpec((None, ks, d), lambda b, t, *_: (b, 0, 0))
    h_spec = pl.BlockSpec((None, s, d), lambda b, t, *_: (b, 0, 0))
    const = lambda shape: pl.BlockSpec(shape, lambda b, t, *_: (0,) * len(shape))
    vec = const((1, d))
    return _call(
        body, (bsz, t_rows // tm),
        [row_spec, _mod_spec(mod, layer, 0), _mod_spec(mod, layer, 1),
         _mod_spec(mod, layer, 2), st_spec, h_spec,
         const(w_in.shape), const(conv_w.shape), vec,
         const(w_ga.shape), vec, const(w_gx.shape), vec, vec,
         const(w_out.shape), vec, vec],
        [row_spec, st_spec, h_spec],
        [jax.ShapeDtypeStruct((bsz, t_rows, d), F32), jax.ShapeDtypeStruct((bsz, ks, d), F32),
         jax.ShapeDtypeStruct((bsz, s, d), F32)],
        [pltpu.VMEM((pad + tm, d), F32), pltpu.VMEM((tm, d), F32), pltpu.VMEM((tm, d), F32),
         pltpu.VMEM((tm, d), F32), pltpu.VMEM((s, d), F32)],
        (x, mod, mod, mod, st0, h0, w_in, conv_w, conv_b.reshape(1, d), w_ga,
         b_ga.reshape(1, d), w_gx, b_gx.reshape(1, d), prm.reshape(1, d), w_out,
         ln_g.reshape(1, d), ln_b.reshape(1, d)),
        "lru_layer")


def _rope_block(xb, cos, sin_lo, sin_hi, half):
    return (xb * cos + pltpu.roll(xb, LANES - half, axis=1) * sin_lo
            + pltpu.roll(xb, half, axis=1) * sin_hi)


def _qkv_body(x_ref, sh_ref, sc_ref, cos_ref, slo_ref, shi_ref, win_ref,
              k_ref, v_ref, z_ref, qb_ref, *attn_refs, half, qscale):
    d = x_ref.shape[-1]
    x = x_ref[...]
    u = (x * (1.0 + sc_ref[...]) + sh_ref[...]).astype(BF16)
    cos, slo, shi = cos_ref[...], slo_ref[...], shi_ref[...]
    q = _dot(u, win_ref[:, 0:d])
    for j in range(d // LANES):
        cols = slice(j * LANES, (j + 1) * LANES)
        qb_ref[:, cols] = (_rope_block(q[:, cols], cos, slo, shi, half) * qscale).astype(BF16)
    k = _dot(u, win_ref[:, d:2 * d])
    for j in range(d // LANES):
        cols = slice(j * LANES, (j + 1) * LANES)
        kr = _rope_block(k[:, cols], cos, slo, shi, half)
        k_ref[:, cols] = kr
        if attn_refs:
            attn_refs[0][:, cols] = kr.astype(BF16)
    v = _dot(u, win_ref[:, 2 * d:3 * d])
    v_ref[...] = v
    if attn_refs:
        vt_ref = attn_refs[1]
        vd = vt_ref.shape[1]
        for hv in range(vt_ref.shape[0]):
            vt_ref[hv] = v[:, hv * vd:(hv + 1) * vd].T.astype(BF16)
    z_ref[...] = _dot(u, win_ref[:, 3 * d:4 * d]).astype(z_ref.dtype)


def _qkv_call(x, mod, layer, cos_t, slo_t, shi_t, w_in, *, tm, half, qscale, attn_vd=None):
    bsz, t_rows, d = x.shape
    body = functools.partial(_qkv_body, half=half, qscale=qscale)
    row_spec = pl.BlockSpec((None, tm, d), lambda b, t: (b, t, 0))
    tab_spec = pl.BlockSpec((tm, LANES), lambda b, t: (t, 0))
    f32_out = jax.ShapeDtypeStruct((bsz, t_rows, d), F32)
    bf_out = jax.ShapeDtypeStruct((bsz, t_rows, d), BF16)
    out_specs = [row_spec] * 4
    out_shape = [f32_out, f32_out, bf_out, bf_out]
    if attn_vd is not None:
        n_heads = d // attn_vd
        out_specs += [row_spec, pl.BlockSpec((None, n_heads, None, attn_vd, tm),
                                             lambda b, t: (b, 0, t, 0, 0))]
        out_shape += [bf_out,
                      jax.ShapeDtypeStruct((bsz, n_heads, t_rows // tm, attn_vd, tm), BF16)]
    return pl.pallas_call(
        body,
        grid=(bsz, t_rows // tm),
        in_specs=[row_spec, _mod_spec(mod, layer, 0), _mod_spec(mod, layer, 1),
                  tab_spec, tab_spec, tab_spec,
                  pl.BlockSpec(w_in.shape, lambda b, t: (0, 0))],
        out_specs=out_specs,
        out_shape=out_shape,
        compiler_params=_cparams(("arbitrary", "arbitrary")),
        name="attn_qkv",
    )(x, mod, mod, cos_t, slo_t, shi_t, w_in)


def _rope_tables(pos, head_dim, rot_dim):
    half = rot_dim // 2
    inv_freq = np.exp(np.arange(half, dtype=np.float64) * (-2.0 * math.log(ROPE_THETA) / rot_dim))
    ang = np.asarray(pos, dtype=np.float64)[:, None] * inv_freq[None, :]
    cos, sin = np.cos(ang), np.sin(ang)
    n = ang.shape[0]
    ones = np.ones((n, head_dim - rot_dim))
    zeros = np.zeros((n, head_dim - rot_dim))
    zh = np.zeros((n, half))
    cos_h = np.concatenate([cos, cos, ones], axis=1)
    slo_h = np.concatenate([-sin, zh, zeros], axis=1)
    shi_h = np.concatenate([zh, sin, zeros], axis=1)
    rep = LANES // head_dim
    return tuple(jnp.asarray(np.tile(t, (1, rep)), dtype=F32) for t in (cos_h, slo_h, shi_h))


def _diff_lambda(lq1_ref, lk1_ref, lq2_ref, lk2_ref, lam_init):
    s1 = jnp.sum(lq1_ref[...] * lk1_ref[...], axis=-1, keepdims=True)
    s2 = jnp.sum(lq2_ref[...] * lk2_ref[...], axis=-1, keepdims=True)
    return jnp.exp(s1) - jnp.exp(s2) + lam_init


def _flash_body(q_ref, k_ref, vt_ref, lq1_ref, lk1_ref, lq2_ref, lk2_ref, o_ref,
                qq_s, sa_s, sb_s, m_s, acc_s, *, tq, head_dim, nh, lam_init):
    qi = pl.program_id(2)
    vd, tk = vt_ref.shape[2], vt_ref.shape[3]
    nd = tq // tk
    lane = lax.broadcasted_iota(jnp.int32, (tq, LANES), 1)
    for h in range(nh):
        q = q_ref[:, h * LANES:(h + 1) * LANES]
        zero = jnp.zeros_like(q)
        qq_s[h, 0:tq, :] = jnp.where(lane < head_dim, q, zero)
        qq_s[h, tq:2 * tq, :] = jnp.where(lane >= head_dim, q, zero)
    m_s[...] = jnp.full(m_s.shape, -jnp.inf, F32)
    acc_s[...] = jnp.zeros(acc_s.shape, F32)
    ones = jnp.ones((2 * SUBLANES, tk), BF16)

    def scores(j, s_ref, q0=0):
        r0 = pl.multiple_of(j * tk, tk)
        for h in range(nh):
            k = k_ref[pl.ds(r0, tk), h * LANES:(h + 1) * LANES]
            if q0 == 0:
                s_ref[h] = _dot_nt(k, qq_s[h])
            else:
                qq = jnp.concatenate([qq_s[h, q0:tq, :], qq_s[h, tq + q0:2 * tq, :]], axis=0)
                s_ref[h, :, 0:2 * (tq - q0)] = _dot_nt(k, qq)

    def consume(j, s_ref, diag=False, q0=0):
        w = tq - q0
        for h in range(nh):
            v1 = jnp.concatenate([vt_ref[h, j], ones], axis=0)
            for sub in range(2):
                for cc in range(w // tk):
                    src = slice(sub * w + cc * tk, sub * w + (cc + 1) * tk)
                    dst = slice(sub * tq + q0 + cc * tk, sub * tq + q0 + (cc + 1) * tk)
                    st = s_ref[h, :, src]
                    if diag and cc == 0:
                        key = lax.broadcasted_iota(jnp.int32, st.shape, 0)
                        row = lax.broadcasted_iota(jnp.int32, st.shape, 1)
                        st = jnp.where(key <= row, st, -jnp.inf)
                    m_prev = m_s[h, :, dst]
                    m_new = jnp.maximum(m_prev, jnp.max(st, axis=0, keepdims=True))
                    alpha = jnp.exp2(m_prev - m_new)
                    pt = jnp.exp2(st - m_new).astype(BF16)
                    acc_s[h, :, dst] = alpha * acc_s[h, :, dst] + _dot(v1, pt)
                    m_s[h, :, dst] = m_new

    scores(0, sa_s)

    def pair(i, carry):
        j = 2 * i
        scores(j + 1, sb_s)
        consume(j, sa_s)
        scores(j + 2, sa_s)
        consume(j + 1, sb_s)
        return carry

    lax.fori_loop(0, qi * (nd // 2), pair, 0)

    bufs = (sa_s, sb_s)
    for g in range(nd):
        j = nd * qi + g
        if g + 1 < nd:
            scores(j + 1, bufs[(g + 1) % 2], q0=(g + 1) * tk)
        consume(j, bufs[g % 2], diag=True, q0=g * tk)

    lam = _diff_lambda(lq1_ref, lk1_ref, lq2_ref, lk2_ref, lam_init)
    for h in range(nh):
        acc = acc_s[h]
        ot = acc[0:vd, :] / acc[vd:vd + 1, :]
        o_ref[:, h * LANES:(h + 1) * LANES] = (
            (ot[:, 0:tq] - lam * ot[:, tq:2 * tq]).T.astype(o_ref.dtype))


def _flash_call(qb, kb, vt, lq1, lk1, lq2, lk2, *, tq, head_dim, nh, lam_init):
    bsz, t_rows, d = qb.shape
    _, n_heads, n_blk, vd, tk = vt.shape
    assert tq % (2 * tk) == 0 and n_blk * tk == t_rows and vd == LANES
    body = functools.partial(_flash_body, tq=tq, head_dim=head_dim, nh=nh, lam_init=lam_init)
    q_spec = pl.BlockSpec((None, tq, nh * LANES), lambda b, h, i, *_: (b, i, h))
    k_spec = pl.BlockSpec((None, t_rows, nh * LANES), lambda b, h, i, *_: (b, 0, h))
    vt_spec = pl.BlockSpec((None, nh, n_blk, vd, tk), lambda b, h, i, *_: (b, h, 0, 0, 0))
    l_spec = pl.BlockSpec((1, head_dim), lambda b, h, i, *_: (0, 0))
    return _call(
        body, (bsz, n_heads // nh, t_rows // tq),
        [q_spec, k_spec, vt_spec, l_spec, l_spec, l_spec, l_spec],
        q_spec,
        jax.ShapeDtypeStruct((bsz, t_rows, d), BF16),
        [pltpu.VMEM((nh, 2 * tq, LANES), BF16), pltpu.VMEM((nh, tk, 2 * tq), F32),
         pltpu.VMEM((nh, tk, 2 * tq), F32), pltpu.VMEM((nh, 1, 2 * tq), F32),
         pltpu.VMEM((nh, vd + 2 * SUBLANES, 2 * tq), F32)],
        (qb, kb, vt, lq1, lk1, lq2, lk2),
        "attn_flash")


class _DecodeConfig:
    def __init__(self, page_table, qs, cache_kt, cache_vr, layer, k_new, v_new, lvec, *, first,
                 gp, n_heads, lam_init):
        self.pt = page_table.reshape(-1)
        self.n_pages = page_table.shape[1]
        self.qs, self.kt, self.vr, self.layer = qs, cache_kt, cache_vr, layer
        self.k_new, self.v_new, self.lvec = k_new, v_new, list(lvec)
        self.first, self.gp, self.n_heads, self.lam_init = first, gp, n_heads, lam_init
        self.kd, self.page = cache_kt.shape[2], cache_kt.shape[3]
        self.vd = cache_vr.shape[-1]
        self.t_new = k_new.shape[1]
        self.head_dim = lvec[0].shape[-1]
        self.n_sub = self.kd // self.head_dim
        self.rows = self.n_sub * self.t_new
        self.spb = self.n_pages // gp

    def count(self, grid):
        assert len(grid) == 2 and grid[1] == self.spb
        return grid[0]

    def in_specs(self, grid):
        first = self.first
        self.count(grid)

        def per_seq(shape):
            return pl.BlockSpec((None,) + shape, lambda b, p, pt: (first + b, 0, 0))

        l_spec = pl.BlockSpec((1, self.head_dim), lambda *a: (0, 0))
        in_hbm = pl.BlockSpec(memory_space=pl.ANY)
        return [per_seq((self.t_new, self.kd)), in_hbm, in_hbm,
                per_seq((self.t_new, self.kd)), per_seq((self.t_new * self.n_heads, self.vd)),
                l_spec, l_spec, l_spec, l_spec]

    def args(self):
        return [self.qs, self.kt, self.vr, self.k_new, self.v_new] + self.lvec

    def out_spec(self, grid):
        return pl.BlockSpec((None, self.t_new, self.n_heads * self.vd),
                            lambda b, p, pt: (b, 0, 0))

    def out_shape(self, grid):
        return jax.ShapeDtypeStruct((self.count(grid), self.t_new, self.n_heads * self.vd), F32)

    def scratch_shapes(self):
        return [pltpu.VMEM((self.rows, self.kd), F32), pltpu.VMEM((self.rows, self.kd), BF16),
                pltpu.VMEM((self.rows, 1), F32), pltpu.VMEM((self.rows, 1), F32),
                pltpu.VMEM((self.rows, self.vd), F32),
                pltpu.VMEM((self.page, self.kd), F32),
                pltpu.VMEM((self.page * self.n_heads, self.vd), F32),
                pltpu.VMEM((PAGE_BUFFERS, self.gp, self.kd, self.page), F32),
                pltpu.VMEM((PAGE_BUFFERS, self.gp, self.page * self.n_heads, self.vd), F32),
                pltpu.SemaphoreType.DMA((PAGE_BUFFERS, 2))]


class _DecodeStep:
    def __init__(self, cfg, grid, pt_ref, in_refs, out_ref, scratch_refs):
        self.c = cfg
        self.pt_ref = pt_ref
        (self.q_ref, self.kt_hbm, self.vr_hbm, self.kn_ref, self.vn_ref) = in_refs[:5]
        self.l_refs = in_refs[5:]
        self.o_ref = out_ref
        (self.qf_s, self.qx_s, self.m_s, self.l_s, self.acc_s, self.kpad, self.vpad,
         self.kbuf, self.vbuf, self.sem) = scratch_refs
        self.b = pl.program_id(0)
        self.p = pl.program_id(1)
        self.n_steps = grid[0] * grid[1]
        self.step = self.b * grid[1] + self.p
        slot = lax.rem(self.step, PAGE_BUFFERS)
        self.k_refs = [self.kbuf.at[slot, g] for g in range(cfg.gp)]
        self.v_refs = [self.vbuf.at[slot, g] for g in range(cfg.gp)]

    def _page_copies(self, step):
        c = self.c
        slot = step % PAGE_BUFFERS if isinstance(step, int) else lax.rem(step, PAGE_BUFFERS)
        copies = []
        for g in range(c.gp):
            page = self.pt_ref[c.first * c.n_pages + step * c.gp + g]
            copies.append(pltpu.make_async_copy(self.kt_hbm.at[c.layer, page],
                                                self.kbuf.at[slot, g], self.sem.at[slot, 0]))
            copies.append(pltpu.make_async_copy(self.vr_hbm.at[c.layer, page],
                                                self.vbuf.at[slot, g], self.sem.at[slot, 1]))
        return copies

    @staticmethod
    def _start(copies):
        for i, cp in enumerate(copies):
            cp.start(priority=i % 2)

    def preamble(self):
        c = self.c
        ahead = PAGE_BUFFERS - 1

        @pl.when(self.step == 0)
        def _():
            for s in range(min(ahead, self.n_steps)):
                self._start(self._page_copies(s))

        @pl.when(self.step + ahead < self.n_steps)
        def _():
            self._start(self._page_copies(self.step + ahead))

        for cp in self._page_copies(self.step):
            cp.wait()

        @pl.when((self.b == 0) & (self.p == 0))
        def _():
            self.kpad[...] = jnp.zeros(self.kpad.shape, F32)
            self.vpad[...] = jnp.zeros(self.vpad.shape, F32)

        @pl.when(self.p == 0)
        def _():
            self.m_s[...] = jnp.full(self.m_s.shape, -jnp.inf, F32)
            self.l_s[...] = jnp.zeros(self.l_s.shape, F32)
            self.acc_s[...] = jnp.zeros(self.acc_s.shape, F32)
            q = self.q_ref[...].astype(F32)
            lane = lax.broadcasted_iota(jnp.int32, q.shape, 1)
            for h in range(c.n_sub):
                own = (lane >= h * c.head_dim) & (lane < (h + 1) * c.head_dim)
                self.qf_s[h * c.t_new:(h + 1) * c.t_new, :] = jnp.where(own, q, 0.0)
            self.qx_s[...] = self.qf_s[...].astype(BF16)

    def _update(self, sc, vs):
        c = self.c
        m_prev = self.m_s[...]
        m_new = jnp.maximum(m_prev, jnp.max(sc, axis=1, keepdims=True))
        alpha = jnp.exp(m_prev - m_new)
        pe = jnp.exp(sc - m_new)
        self.l_s[...] = alpha * self.l_s[...] + jnp.sum(pe, axis=1, keepdims=True)
        self.m_s[...] = m_new
        pb = pe.astype(BF16)
        grp = 2 * c.t_new
        for hv in range(c.n_heads):
            rs = slice(hv * grp, (hv + 1) * grp)
            vh = jnp.concatenate(
                [v[pl.ds(hv, c.page, stride=c.n_heads), :].astype(BF16) for v in vs], axis=0)
            self.acc_s[rs, :] = alpha[rs, :] * self.acc_s[rs, :] + _dot(pb[rs, :], vh)

    def main(self):
        kt = jnp.concatenate([k[...].astype(BF16) for k in self.k_refs], axis=1)
        self._update(_dot(self.qx_s[...], kt), self.v_refs)

    def finalize(self):
        c = self.c

        @pl.when(self.p == c.spb - 1)
        def _():
            self.kpad[0:c.t_new, :] = self.kn_ref[...]
            self.vpad[0:c.t_new * c.n_heads, :] = self.vn_ref[...]
            sc = _dot_nt(self.qx_s[...], self.kpad[...].astype(BF16))
            row = lax.broadcasted_iota(jnp.int32, sc.shape, 0)
            col = lax.broadcasted_iota(jnp.int32, sc.shape, 1)
            self._update(jnp.where(col <= row % c.t_new, sc, -jnp.inf), [self.vpad])
            o = self.acc_s[...] / self.l_s[...]
            lam = _diff_lambda(*self.l_refs, c.lam_init)
            grp = 2 * c.t_new
            for hv in range(c.n_heads):
                o1 = o[hv * grp:hv * grp + c.t_new, :]
                o2 = o[hv * grp + c.t_new:(hv + 1) * grp, :]
                self.o_ref[:, hv * c.vd:(hv + 1) * c.vd] = o1 - lam * o2


def _decode_call(cfg, n_seq):
    grid = (n_seq, cfg.spb)
    in_specs = cfg.in_specs(grid)
    n_in = len(in_specs)

    def body(pt_ref, *refs):
        step = _DecodeStep(cfg, grid, pt_ref, refs[:n_in], refs[n_in], refs[n_in + 1:])
        step.preamble()
        step.main()
        step.finalize()

    grid_spec = pltpu.PrefetchScalarGridSpec(
        num_scalar_prefetch=1, grid=grid, in_specs=in_specs, out_specs=cfg.out_spec(grid),
        scratch_shapes=cfg.scratch_shapes())
    return pl.pallas_call(body, grid_spec=grid_spec, out_shape=cfg.out_shape(grid),
                          compiler_params=_cparams(("arbitrary", "arbitrary")),
                          name="attn_decode")(cfg.pt, *cfg.args())


def _call(body, grid, in_specs, out_specs, out_shape, scratch_shapes, args, name):
    return pl.pallas_call(body, grid=grid, in_specs=in_specs, out_specs=out_specs,
                          out_shape=out_shape, scratch_shapes=scratch_shapes,
                          compiler_params=_cparams(("arbitrary",) * len(grid)), name=name)(*args)


def _attn_out_body(x_ref, gt_ref, o_ref, z_ref, sg_ref, wout_ref, g_ref, b_ref, y_ref, gbuf, *,
                   alpha, out_scale):
    d = x_ref.shape[-1]
    vd = sg_ref.shape[-1]
    for hv in range(d // vd):
        cols = slice(hv * vd, (hv + 1) * vd)
        of = o_ref[:, cols].astype(F32)
        of = of * lax.rsqrt(jnp.mean(of * of, axis=-1, keepdims=True) + SUBLN_EPS)
        of = of * sg_ref[...] * out_scale
        gbuf[:, cols] = (of * _silu(z_ref[:, cols].astype(F32))).astype(BF16)
    out = _dot(gbuf[...], wout_ref[...])
    res = alpha * x_ref[...] + gt_ref[...] * out
    y_ref[...] = _layer_norm(res, g_ref[...], b_ref[...])


def _attn_out_call(x, mod, layer, o, z, subln_g, w_out, ln_g, ln_b, *, tm, alpha, out_scale):
    bsz, t_rows, d = x.shape
    vd = subln_g.shape[-1]
    body = functools.partial(_attn_out_body, alpha=alpha, out_scale=out_scale)
    row_spec = pl.BlockSpec((None, tm, d), lambda b, t: (b, t, 0))
    const = lambda shape: pl.BlockSpec(shape, lambda b, t: (0,) * len(shape))
    return pl.pallas_call(
        body,
        grid=(bsz, t_rows // tm),
        in_specs=[row_spec, _mod_spec(mod, layer, 2), row_spec, row_spec, const((1, vd)),
                  const(w_out.shape),
                  const((1, d)), const((1, d))],
        out_specs=row_spec,
        out_shape=jax.ShapeDtypeStruct((bsz, t_rows, d), F32),
        scratch_shapes=[pltpu.VMEM((tm, d), BF16)],
        compiler_params=_cparams(("arbitrary", "arbitrary")),
        name="attn_out",
    )(x, mod, o, z, subln_g.reshape(1, vd), w_out, ln_g.reshape(1, d), ln_b.reshape(1, d))


def kernel(x_prompt, x_sample, state_conv_a, state_lru_h, state_lru_conv, cache_k, cache_v, page_table, c_prompt, c_sample, w_ada, b_ada, ln_g, ln_b, a_w_in, a_conv_w, a_w_out, r_w_in, r_conv_w, r_conv_b, r_w_ga, r_b_ga, r_w_gx, r_b_gx, r_lru_param, r_w_out, d_w_in, d_lq1, d_lk1, d_lq2, d_lk2, d_subln_g, d_w_out):
    bp, tp, d = x_prompt.shape
    bs, ts, _ = x_sample.shape
    depth = w_ada.shape[0]
    n_pages = page_table.shape[1]
    page = cache_k.shape[2]
    n_sub, head_dim = cache_k.shape[3], cache_k.shape[4]
    n_heads, vd = cache_v.shape[3], cache_v.shape[4]
    past_len = n_pages * page
    rot_dim = head_dim // 4
    alpha = (2 * depth) ** 0.25
    rows_s = bs * ts

    n_c = _round_up(bp + bs, SUBLANES)
    c_all = jnp.concatenate([c_sample, c_prompt, jnp.zeros((n_c - bp - bs, d), F32)], axis=0)
    mod_p, mod_s = _ada_call(c_all, w_ada, b_ada, bp=bp, bs=bs, ts=ts)

    def to_time_major(a):
        return jnp.swapaxes(a, 0, 1).reshape((1, a.shape[1] * bs) + a.shape[2:])

    def from_time_major(a, n):
        return jnp.swapaxes(a.reshape(n, bs, a.shape[-1]), 0, 1)

    bf = lambda w: w.astype(BF16)
    half = rot_dim // 2
    qscale = head_dim ** -0.5

    conv_p, conv_s, lruh_p, lruh_s, lruc_p, lruc_s = [], [], [], [], [], []
    k_p, v_p, k_s, v_s = [], [], [], []

    def layer_weights(i):
        kind, j = i % N_MIXERS, i // N_MIXERS
        if kind == 0:
            return (bf(a_w_in[j]), a_conv_w[j], bf(a_w_out[j]), ln_g[i], ln_b[i])
        if kind == 1:
            return (bf(r_w_in[j]), r_conv_w[j], r_conv_b[j], bf(r_w_ga[j]), r_b_ga[j],
                    bf(r_w_gx[j]), r_b_gx[j], r_lru_param[j], bf(r_w_out[j]), ln_g[i], ln_b[i])
        return (bf(d_w_in[j]), bf(d_w_out[j]),
                [v[j].reshape(1, head_dim) for v in (d_lq1, d_lk1, d_lq2, d_lk2)],
                0.8 - 0.6 * math.exp(-0.3 * i))

    xp = x_prompt
    xs = to_time_major(x_sample)
    for i in range(depth):
        kind, j = i % N_MIXERS, i // N_MIXERS
        w = layer_weights(i)
        if kind == 0:
            width = a_conv_w.shape[1]
            xp, nbp = _conv_layer(xp, mod_p, i, jnp.zeros((bp, width - 1, d), F32), *w,
                                  s=1, tm=512, alpha=alpha)
            xs, nbs = _conv_layer(xs, mod_s, i, to_time_major(state_conv_a[j]), *w,
                                  s=bs, tm=rows_s, alpha=alpha)
            conv_p.append(nbp)
            conv_s.append(from_time_major(nbs, width - 1))
        elif kind == 1:
            width = r_conv_w.shape[1]
            xp, nbp, hp = _lru_layer(xp, mod_p, i, jnp.zeros((bp, width - 1, d), F32),
                                     jnp.zeros((bp, 1, d), F32), *w, s=1, tm=256, alpha=alpha)
            xs, nbs, hs = _lru_layer(xs, mod_s, i, to_time_major(state_lru_conv[j]),
                                     state_lru_h[j].reshape(1, bs, d), *w, s=bs, tm=rows_s,
                                     alpha=alpha)
            lruh_p.append(hp.reshape(bp, d))
            lruh_s.append(hs.reshape(bs, d))
            lruc_p.append(nbp)
            lruc_s.append(from_time_major(nbs, width - 1))
        else:
            w_in, w_out, lvec, lam_init = w
            tabs_p = _rope_tables(np.arange(tp), head_dim, rot_dim)
            kp, vp, zp, qb, kb, vt = _qkv_call(xp, mod_p, i, *tabs_p, w_in, tm=256, half=half,
                                               qscale=qscale * math.log2(math.e), attn_vd=vd)
            op = _flash_call(qb, kb, vt, *lvec, tq=1024, head_dim=head_dim, nh=4,
                             lam_init=lam_init)
            xp = _attn_out_call(xp, mod_p, i, op, zp, d_subln_g[j], w_out, ln_g[i], ln_b[i],
                                tm=512, alpha=alpha, out_scale=1.0 - lam_init)
            k_p.append(kp.reshape(bp, tp, n_sub, head_dim))
            v_p.append(vp.reshape(bp, tp, n_heads, vd))
            pos_s = past_len + np.repeat(np.arange(ts), bs)
            tabs_s = _rope_tables(pos_s, head_dim, rot_dim)
            ks_, vs_, zs, qsb = _qkv_call(xs, mod_s, i, *tabs_s, w_in, tm=rows_s, half=half,
                                          qscale=qscale)
            ksn = from_time_major(ks_, ts)
            vsn = from_time_major(vs_, ts)
            n_layers, n_phys = cache_k.shape[0], cache_k.shape[1]
            cache_kt = jnp.transpose(cache_k, (0, 1, 3, 4, 2)).reshape(
                n_layers, n_phys, n_sub * head_dim, page)
            cache_vr = cache_v.reshape(n_layers, n_phys, page * n_heads, vd)
            cfg = _DecodeConfig(page_table, from_time_major(qsb, ts), cache_kt, cache_vr, j, ksn,
                                vsn.reshape(bs, ts * n_heads, vd), lvec, first=0, gp=16,
                                n_heads=n_heads, lam_init=lam_init)
            os_ = _decode_call(cfg, bs)
            xs = _attn_out_call(xs, mod_s, i, to_time_major(os_), zs, d_subln_g[j], w_out,
                                ln_g[i], ln_b[i], tm=rows_s, alpha=alpha,
                                out_scale=1.0 - lam_init)
            k_s.append(ksn.reshape(bs, ts, n_sub, head_dim))
            v_s.append(vsn.reshape(bs, ts, n_heads, vd))

    return (xp, from_time_major(xs, ts),
            jnp.stack(conv_p), jnp.stack(conv_s),
            jnp.stack(lruh_p), jnp.stack(lruh_s),
            jnp.stack(lruc_p), jnp.stack(lruc_s),
            jnp.stack(k_p), jnp.stack(v_p), jnp.stack(k_s), jnp.stack(v_s))
```

```python
import functools
import math

import jax
import jax.numpy as jnp
import numpy as np
from jax import lax
from jax.experimental import pallas as pl
from jax.experimental.pallas import tpu as pltpu

F32 = jnp.float32
BF16 = jnp.bfloat16

LN_EPS = 1e-5
SUBLN_EPS = 1e-5
LRU_C = 8.0
ROPE_THETA = 500000.0
N_MIXERS = 3

SUBLANES = 8
LANES = 128
VMEM_LIMIT_BYTES = 56 * 1024 * 1024
PAGE_BUFFERS = 2


def _cparams(semantics):
    return pltpu.CompilerParams(dimension_semantics=semantics, vmem_limit_bytes=VMEM_LIMIT_BYTES)


def _dot(a, b):
    return jnp.dot(a, b, preferred_element_type=F32)


def _dot_nt(a, b):
    return lax.dot_general(a, b, (((1,), (1,)), ((), ())), preferred_element_type=F32)


def _silu(z):
    return z * jax.nn.sigmoid(z)


def _layer_norm(y, g, b):
    mu = jnp.mean(y, axis=-1, keepdims=True)
    yc = y - mu
    var = jnp.mean(yc * yc, axis=-1, keepdims=True)
    return yc * lax.rsqrt(var + LN_EPS) * g + b


def _round_up(n, m):
    return (n + m - 1) // m * m


def _ada_body(c_ref, w_ref, b_ref, op_ref, os_ref, *, bp, bs, ts):
    c = c_ref[...]
    a = _silu(c).astype(BF16)
    m = _dot(a, w_ref[...].astype(BF16)) + b_ref[...]
    for r in range(bp):
        op_ref[r] = m[bs + r:bs + r + 1, :]
    for t in range(ts):
        os_ref[t * bs:(t + 1) * bs, :] = m[0:bs, :]


def _ada_call(c_all, w_ada, b_ada, *, bp, bs, ts):
    depth, d, d3 = w_ada.shape
    rows = c_all.shape[0]
    nt = d3 // d
    return pl.pallas_call(
        functools.partial(_ada_body, bp=bp, bs=bs, ts=ts),
        grid=(depth, nt),
        in_specs=[
            pl.BlockSpec((rows, d), lambda i, n: (0, 0)),
            pl.BlockSpec((None, d, d), lambda i, n: (i, 0, n)),
            pl.BlockSpec((None, 1, d), lambda i, n: (i, 0, n)),
        ],
        out_specs=[pl.BlockSpec((None, None, bp, 1, d), lambda i, n: (i, n, 0, 0, 0)),
                   pl.BlockSpec((None, None, None, ts * bs, d), lambda i, n: (i, n, 0, 0, 0))],
        out_shape=[jax.ShapeDtypeStruct((depth, nt, bp, 1, d), F32),
                   jax.ShapeDtypeStruct((depth, nt, 1, ts * bs, d), F32)],
        compiler_params=_cparams(("arbitrary", "arbitrary")),
        name="adaln",
    )(c_all, w_ada, b_ada.reshape(depth, 1, d3))


def _mod_spec(mod, layer, k):
    _, _, _, r, d = mod.shape
    return pl.BlockSpec((None, None, None, r, d), lambda b, t, *_: (layer, k, b, 0, 0))


def _load_history(t, buf, st0_ref, pad, ks, tm):
    @pl.when(t == 0)
    def _():
        buf[pad - ks:pad, :] = st0_ref[...]

    @pl.when(t > 0)
    def _():
        buf[pad - ks:pad, :] = buf[pad + tm - ks:pad + tm, :]


def _conv_taps(buf, cw_ref, cols, width, s, pad, tm):
    y = None
    if s % SUBLANES == 0:
        for k in range(width):
            r0 = pad - (width - 1 - k) * s
            term = cw_ref[k:k + 1, cols] * buf[r0:r0 + tm, cols]
            y = term if y is None else y + term
        return y
    full = buf[0:pad + tm, cols]
    for k in range(width):
        back = (width - 1 - k) * s
        src = full if back == 0 else pltpu.roll(full, back, axis=0)
        term = cw_ref[k:k + 1, cols] * src[pad:pad + tm, :]
        y = term if y is None else y + term
    return y


def _conv_body(x_ref, sh_ref, sc_ref, gt_ref, st0_ref, win_ref, cw_ref, wout_ref, g_ref, b_ref,
               y_ref, st_ref, abuf, acc, *, s, tm, cw, width, pad, alpha):
    d = x_ref.shape[-1]
    ks = (width - 1) * s
    t = pl.program_id(1)
    _load_history(t, abuf, st0_ref, pad, ks, tm)
    x = x_ref[...]
    u = (x * (1.0 + sc_ref[...]) + sh_ref[...]).astype(BF16)
    for c in range(d // cw):
        cols = slice(c * cw, (c + 1) * cw)
        h = _dot(u, win_ref[:, c * cw:(c + 1) * cw])
        cg = _dot(u, win_ref[:, 2 * d + c * cw:2 * d + (c + 1) * cw])
        abuf[pad:pad + tm, cols] = cg * h
        y = _conv_taps(abuf, cw_ref, cols, width, s, pad, tm)
        bg = _dot(u, win_ref[:, d + c * cw:d + (c + 1) * cw])
        z = _dot(u, win_ref[:, 3 * d + c * cw:3 * d + (c + 1) * cw])
        gated = (_silu(z) * bg * y).astype(BF16)
        part = _dot(gated, wout_ref[cols, :])
        if c == 0:
            acc[...] = part
        else:
            acc[...] += part
    st_ref[...] = abuf[pad + tm - ks:pad + tm, :]
    res = alpha * x + gt_ref[...] * acc[...]
    y_ref[...] = _layer_norm(res, g_ref[...], b_ref[...])


def _conv_layer(x, mod, layer, st0, w_in, conv_w, w_out, ln_g, ln_b, *, s, tm, alpha):
    bsz, t_rows, d = x.shape
    width = conv_w.shape[0]
    ks = (width - 1) * s
    pad = _round_up(ks, SUBLANES)
    cw = 256
    body = functools.partial(_conv_body, s=s, tm=tm, cw=cw, width=width, pad=pad, alpha=alpha)
    row_spec = pl.BlockSpec((None, tm, d), lambda b, t, *_: (b, t, 0))
    st_spec = pl.BlockSpec((None, ks, d), lambda b, t, *_: (b, 0, 0))
    const = lambda shape: pl.BlockSpec(shape, lambda b, t, *_: (0,) * len(shape))
    return _call(
        body, (bsz, t_rows // tm),
        [row_spec, _mod_spec(mod, layer, 0), _mod_spec(mod, layer, 1),
         _mod_spec(mod, layer, 2), st_spec,
         const(w_in.shape), const(conv_w.shape), const(w_out.shape),
         const((1, d)), const((1, d))],
        [row_spec, st_spec],
        [jax.ShapeDtypeStruct((bsz, t_rows, d), F32), jax.ShapeDtypeStruct((bsz, ks, d), F32)],
        [pltpu.VMEM((pad + tm, d), F32), pltpu.VMEM((tm, d), F32)],
        (x, mod, mod, mod, st0, w_in, conv_w, w_out, ln_g.reshape(1, d), ln_b.reshape(1, d)),
        "conv_layer")


def _lru_body(x_ref, sh_ref, sc_ref, gt_ref, st0_ref, h0_ref, win_ref, cw_ref, cb_ref,
              wga_ref, bga_ref, wgx_ref, bgx_ref, prm_ref, wout_ref, g_ref, b_ref,
              y_ref, st_ref, hl_ref, xbuf, a_s, b_s, h_s, hc, *, s, tm, width, pad, alpha):
    d = x_ref.shape[-1]
    nblk, blk, _ = wga_ref.shape
    ks = (width - 1) * s
    t = pl.program_id(1)
    _load_history(t, xbuf, st0_ref, pad, ks, tm)

    @pl.when(t == 0)
    def _():
        hc[...] = h0_ref[...]

    x = x_ref[...]
    u = (x * (1.0 + sc_ref[...]) + sh_ref[...]).astype(BF16)
    xbuf[pad:pad + tm, :] = _dot(u, win_ref[:, 0:d])
    st_ref[...] = xbuf[pad + tm - ks:pad + tm, :]
    for n in range(nblk):
        cols = slice(n * blk, (n + 1) * blk)
        xc = _conv_taps(xbuf, cw_ref, cols, width, s, pad, tm) + cb_ref[:, cols]
        xcb = xc.astype(BF16)
        r = jax.nn.sigmoid(_dot(xcb, wga_ref[n]) + bga_ref[:, cols])
        gi = jax.nn.sigmoid(_dot(xcb, wgx_ref[n]) + bgx_ref[:, cols])
        log_a = r * (LRU_C * jax.nn.log_sigmoid(prm_ref[:, cols]))
        a = jnp.exp(log_a)
        a_s[:, cols] = a
        b_s[:, cols] = jnp.sqrt(-jnp.tanh(log_a) * (a * a + 1.0)) * (gi * xc)

    steps = tm // s

    def step(i, h):
        r0 = pl.multiple_of(i * s, s)
        h = a_s[pl.ds(r0, s), :] * h + b_s[pl.ds(r0, s), :]
        h_s[pl.ds(r0, s), :] = h
        return h

    h_last = lax.fori_loop(0, steps, step, hc[...], unroll=True)
    hc[...] = h_last
    hl_ref[...] = h_last

    z = _dot(u, win_ref[:, d:2 * d])
    yy = (h_s[...] * _silu(z)).astype(BF16)
    out = _dot(yy, wout_ref[...])
    res = alpha * x + gt_ref[...] * out
    y_ref[...] = _layer_norm(res, g_ref[...], b_ref[...])


def _lru_layer(x, mod, layer, st0, h0, w_in, conv_w, conv_b, w_ga, b_ga, w_gx, b_gx, prm, w_out,
               ln_g, ln_b, *, s, tm, alpha):
    bsz, t_rows, d = x.shape
    width = conv_w.shape[0]
    ks = (width - 1) * s
    pad = _round_up(ks, SUBLANES)
    body = functools.partial(_lru_body, s=s, tm=tm, width=width, pad=pad, alpha=alpha)
    row_spec = pl.BlockSpec((None, tm, d), lambda b, t, *_: (b, t, 0))
    st_spec = pl.BlockSpec((None, ks, d), lambda b, t, *_: (b, 0, 0))
    h_spec = pl.BlockSpec((None, s, d), lambda b, t, *_: (b, 0, 0))
    const = lambda shape: pl.BlockSpec(shape, lambda b, t, *_: (0,) * len(shape))
    vec = const((1, d))
    return _call(
        body, (bsz, t_rows // tm),
        [row_spec, _mod_spec(mod, layer, 0), _mod_spec(mod, layer, 1),
         _mod_spec(mod, layer, 2), st_spec, h_spec,
         const(w_in.shape), const(conv_w.shape), vec,
         const(w_ga.shape), vec, const(w_gx.shape), vec, vec,
         const(w_out.shape), vec, vec],
        [row_spec, st_spec, h_spec],
        [jax.ShapeDtypeStruct((bsz, t_rows, d), F32), jax.ShapeDtypeStruct((bsz, ks, d), F32),
         jax.ShapeDtypeStruct((bsz, s, d), F32)],
        [pltpu.VMEM((pad + tm, d), F32), pltpu.VMEM((tm, d), F32), pltpu.VMEM((tm, d), F32),
         pltpu.VMEM((tm, d), F32), pltpu.VMEM((s, d), F32)],
        (x, mod, mod, mod, st0, h0, w_in, conv_w, conv_b.reshape(1, d), w_ga,
         b_ga.reshape(1, d), w_gx, b_gx.reshape(1, d), prm.reshape(1, d), w_out,
         ln_g.reshape(1, d), ln_b.reshape(1, d)),
        "lru_layer")


def _rope_block(xb, cos, sin_lo, sin_hi, half):
    return (xb * cos + pltpu.roll(xb, LANES - half, axis=1) * sin_lo
            + pltpu.roll(xb, half, axis=1) * sin_hi)


def _qkv_body(x_ref, sh_ref, sc_ref, cos_ref, slo_ref, shi_ref, win_ref,
              k_ref, v_ref, z_ref, qb_ref, *attn_refs, half, qscale):
    d = x_ref.shape[-1]
    x = x_ref[...]
    u = (x * (1.0 + sc_ref[...]) + sh_ref[...]).astype(BF16)
    cos, slo, shi = cos_ref[...], slo_ref[...], shi_ref[...]
    q = _dot(u, win_ref[:, 0:d])
    for j in range(d // LANES):
        cols = slice(j * LANES, (j + 1) * LANES)
        qb_ref[:, cols] = (_rope_block(q[:, cols], cos, slo, shi, half) * qscale).astype(BF16)
    k = _dot(u, win_ref[:, d:2 * d])
    for j in range(d // LANES):
        cols = slice(j * LANES, (j + 1) * LANES)
        kr = _rope_block(k[:, cols], cos, slo, shi, half)
        k_ref[:, cols] = kr
        if attn_refs:
            attn_refs[0][:, cols] = kr.astype(BF16)
    v = _dot(u, win_ref[:, 2 * d:3 * d])
    v_ref[...] = v
    if attn_refs:
        vt_ref = attn_refs[1]
        vd = vt_ref.shape[1]
        for hv in range(vt_ref.shape[0]):
            vt_ref[hv] = v[:, hv * vd:(hv + 1) * vd].T.astype(BF16)
    z_ref[...] = _dot(u, win_ref[:, 3 * d:4 * d]).astype(z_ref.dtype)


def _qkv_call(x, mod, layer, cos_t, slo_t, shi_t, w_in, *, tm, half, qscale, attn_vd=None):
    bsz, t_rows, d = x.shape
    body = functools.partial(_qkv_body, half=half, qscale=qscale)
    row_spec = pl.BlockSpec((None, tm, d), lambda b, t: (b, t, 0))
    tab_spec = pl.BlockSpec((tm, LANES), lambda b, t: (t, 0))
    f32_out = jax.ShapeDtypeStruct((bsz, t_rows, d), F32)
    bf_out = jax.ShapeDtypeStruct((bsz, t_rows, d), BF16)
    out_specs = [row_spec] * 4
    out_shape = [f32_out, f32_out, bf_out, bf_out]
    if attn_vd is not None:
        n_heads = d // attn_vd
        out_specs += [row_spec, pl.BlockSpec((None, n_heads, None, attn_vd, tm),
                                             lambda b, t: (b, 0, t, 0, 0))]
        out_shape += [bf_out,
                      jax.ShapeDtypeStruct((bsz, n_heads, t_rows // tm, attn_vd, tm), BF16)]
    return pl.pallas_call(
        body,
        grid=(bsz, t_rows // tm),
        in_specs=[row_spec, _mod_spec(mod, layer, 0), _mod_spec(mod, layer, 1),
                  tab_spec, tab_spec, tab_spec,
                  pl.BlockSpec(w_in.shape, lambda b, t: (0, 0))],
        out_specs=out_specs,
        out_shape=out_shape,
        compiler_params=_cparams(("arbitrary", "arbitrary")),
        name="attn_qkv",
    )(x, mod, mod, cos_t, slo_t, shi_t, w_in)


def _rope_tables(pos, head_dim, rot_dim):
    half = rot_dim // 2
    inv_freq = np.exp(np.arange(half, dtype=np.float64) * (-2.0 * math.log(ROPE_THETA) / rot_dim))
    ang = np.asarray(pos, dtype=np.float64)[:, None] * inv_freq[None, :]
    cos, sin = np.cos(ang), np.sin(ang)
    n = ang.shape[0]
    ones = np.ones((n, head_dim - rot_dim))
    zeros = np.zeros((n, head_dim - rot_dim))
    zh = np.zeros((n, half))
    cos_h = np.concatenate([cos, cos, ones], axis=1)
    slo_h = np.concatenate([-sin, zh, zeros], axis=1)
    shi_h = np.concatenate([zh, sin, zeros], axis=1)
    rep = LANES // head_dim
    return tuple(jnp.asarray(np.tile(t, (1, rep)), dtype=F32) for t in (cos_h, slo_h, shi_h))


def _diff_lambda(lq1_ref, lk1_ref, lq2_ref, lk2_ref, lam_init):
    s1 = jnp.sum(lq1_ref[...] * lk1_ref[...], axis=-1, keepdims=True)
    s2 = jnp.sum(lq2_ref[...] * lk2_ref[...], axis=-1, keepdims=True)
    return jnp.exp(s1) - jnp.exp(s2) + lam_init


def _flash_body(q_ref, k_ref, vt_ref, lq1_ref, lk1_ref, lq2_ref, lk2_ref, o_ref,
                qq_s, sa_s, sb_s, m_s, acc_s, *, tq, head_dim, nh, lam_init):
    qi = pl.program_id(2)
    vd, tk = vt_ref.shape[2], vt_ref.shape[3]
    nd = tq // tk
    lane = lax.broadcasted_iota(jnp.int32, (tq, LANES), 1)
    for h in range(nh):
        q = q_ref[:, h * LANES:(h + 1) * LANES]
        zero = jnp.zeros_like(q)
        qq_s[h, 0:tq, :] = jnp.where(lane < head_dim, q, zero)
        qq_s[h, tq:2 * tq, :] = jnp.where(lane >= head_dim, q, zero)
    m_s[...] = jnp.full(m_s.shape, -jnp.inf, F32)
    acc_s[...] = jnp.zeros(acc_s.shape, F32)
    ones = jnp.ones((2 * SUBLANES, tk), BF16)

    def scores(j, s_ref, q0=0):
        r0 = pl.multiple_of(j * tk, tk)
        for h in range(nh):
            k = k_ref[pl.ds(r0, tk), h * LANES:(h + 1) * LANES]
            if q0 == 0:
                s_ref[h] = _dot_nt(k, qq_s[h])
            else:
                qq = jnp.concatenate([qq_s[h, q0:tq, :], qq_s[h, tq + q0:2 * tq, :]], axis=0)
                s_ref[h, :, 0:2 * (tq - q0)] = _dot_nt(k, qq)

    def consume(j, s_ref, diag=False, q0=0):
        w = tq - q0
        for h in range(nh):
            v1 = jnp.concatenate([vt_ref[h, j], ones], axis=0)
            for sub in range(2):
                for cc in range(w // tk):
                    src = slice(sub * w + cc * tk, sub * w + (cc + 1) * tk)
                    dst = slice(sub * tq + q0 + cc * tk, sub * tq + q0 + (cc + 1) * tk)
                    st = s_ref[h, :, src]
                    if diag and cc == 0:
                        key = lax.broadcasted_iota(jnp.int32, st.shape, 0)
                        row = lax.broadcasted_iota(jnp.int32, st.shape, 1)
                        st = jnp.where(key <= row, st, -jnp.inf)
                    m_prev = m_s[h, :, dst]
                    m_new = jnp.maximum(m_prev, jnp.max(st, axis=0, keepdims=True))
                    alpha = jnp.exp2(m_prev - m_new)
                    pt = jnp.exp2(st - m_new).astype(BF16)
                    acc_s[h, :, dst] = alpha * acc_s[h, :, dst] + _dot(v1, pt)
                    m_s[h, :, dst] = m_new

    scores(0, sa_s)

    def pair(i, carry):
        j = 2 * i
        scores(j + 1, sb_s)
        consume(j, sa_s)
        scores(j + 2, sa_s)
        consume(j + 1, sb_s)
        return carry

    lax.fori_loop(0, qi * (nd // 2), pair, 0)

    bufs = (sa_s, sb_s)
    for g in range(nd):
        j = nd * qi + g
        if g + 1 < nd:
            scores(j + 1, bufs[(g + 1) % 2], q0=(g + 1) * tk)
        consume(j, bufs[g % 2], diag=True, q0=g * tk)

    lam = _diff_lambda(lq1_ref, lk1_ref, lq2_ref, lk2_ref, lam_init)
    for h in range(nh):
        acc = acc_s[h]
        ot = acc[0:vd, :] / acc[vd:vd + 1, :]
        o_ref[:, h * LANES:(h + 1) * LANES] = (
            (ot[:, 0:tq] - lam * ot[:, tq:2 * tq]).T.astype(o_ref.dtype))


def _flash_call(qb, kb, vt, lq1, lk1, lq2, lk2, *, tq, head_dim, nh, lam_init):
    bsz, t_rows, d = qb.shape
    _, n_heads, n_blk, vd, tk = vt.shape
    assert tq % (2 * tk) == 0 and n_blk * tk == t_rows and vd == LANES
    body = functools.partial(_flash_body, tq=tq, head_dim=head_dim, nh=nh, lam_init=lam_init)
    q_spec = pl.BlockSpec((None, tq, nh * LANES), lambda b, h, i, *_: (b, i, h))
    k_spec = pl.BlockSpec((None, t_rows, nh * LANES), lambda b, h, i, *_: (b, 0, h))
    vt_spec = pl.BlockSpec((None, nh, n_blk, vd, tk), lambda b, h, i, *_: (b, h, 0, 0, 0))
    l_spec = pl.BlockSpec((1, head_dim), lambda b, h, i, *_: (0, 0))
    return _call(
        body, (bsz, n_heads // nh, t_rows // tq),
        [q_spec, k_spec, vt_spec, l_spec, l_spec, l_spec, l_spec],
        q_spec,
        jax.ShapeDtypeStruct((bsz, t_rows, d), BF16),
        [pltpu.VMEM((nh, 2 * tq, LANES), BF16), pltpu.VMEM((nh, tk, 2 * tq), F32),
         pltpu.VMEM((nh, tk, 2 * tq), F32), pltpu.VMEM((nh, 1, 2 * tq), F32),
         pltpu.VMEM((nh, vd + 2 * SUBLANES, 2 * tq), F32)],
        (qb, kb, vt, lq1, lk1, lq2, lk2),
        "attn_flash")


class _DecodeConfig:
    def __init__(self, page_table, qs, cache_kt, cache_vr, layer, k_new, v_new, lvec, *, first,
                 gp, n_heads, lam_init):
        self.pt = page_table.reshape(-1)
        self.n_pages = page_table.shape[1]
        self.qs, self.kt, self.vr, self.layer = qs, cache_kt, cache_vr, layer
        self.k_new, self.v_new, self.lvec = k_new, v_new, list(lvec)
        self.first, self.gp, self.n_heads, self.lam_init = first, gp, n_heads, lam_init
        self.kd, self.page = cache_kt.shape[2], cache_kt.shape[3]
        self.vd = cache_vr.shape[-1]
        self.t_new = k_new.shape[1]
        self.head_dim = lvec[0].shape[-1]
        self.n_sub = self.kd // self.head_dim
        self.rows = self.n_sub * self.t_new
        self.spb = self.n_pages // gp

    def count(self, grid):
        assert len(grid) == 2 and grid[1] == self.spb
        return grid[0]

    def in_specs(self, grid):
        first = self.first
        self.count(grid)

        def per_seq(shape):
            return pl.BlockSpec((None,) + shape, lambda b, p, pt: (first + b, 0, 0))

        l_spec = pl.BlockSpec((1, self.head_dim), lambda *a: (0, 0))
        in_hbm = pl.BlockSpec(memory_space=pl.ANY)
        return [per_seq((self.t_new, self.kd)), in_hbm, in_hbm,
                per_seq((self.t_new, self.kd)), per_seq((self.t_new * self.n_heads, self.vd)),
                l_spec, l_spec, l_spec, l_spec]

    def args(self):
        return [self.qs, self.kt, self.vr, self.k_new, self.v_new] + self.lvec

    def out_spec(self, grid):
        return pl.BlockSpec((None, self.t_new, self.n_heads * self.vd),
                            lambda b, p, pt: (b, 0, 0))

    def out_shape(self, grid):
        return jax.ShapeDtypeStruct((self.count(grid), self.t_new, self.n_heads * self.vd), F32)

    def scratch_shapes(self):
        return [pltpu.VMEM((self.rows, self.kd), F32), pltpu.VMEM((self.rows, self.kd), BF16),
                pltpu.VMEM((self.rows, 1), F32), pltpu.VMEM((self.rows, 1), F32),
                pltpu.VMEM((self.rows, self.vd), F32),
                pltpu.VMEM((self.page, self.kd), F32),
                pltpu.VMEM((self.page * self.n_heads, self.vd), F32),
                pltpu.VMEM((PAGE_BUFFERS, self.gp, self.kd, self.page), F32),
                pltpu.VMEM((PAGE_BUFFERS, self.gp, self.page * self.n_heads, self.vd), F32),
                pltpu.SemaphoreType.DMA((PAGE_BUFFERS, 2))]


class _DecodeStep:
    def __init__(self, cfg, grid, pt_ref, in_refs, out_ref, scratch_refs):
        self.c = cfg
        self.pt_ref = pt_ref
        (self.q_ref, self.kt_hbm, self.vr_hbm, self.kn_ref, self.vn_ref) = in_refs[:5]
        self.l_refs = in_refs[5:]
        self.o_ref = out_ref
        (self.qf_s, self.qx_s, self.m_s, self.l_s, self.acc_s, self.kpad, self.vpad,
         self.kbuf, self.vbuf, self.sem) = scratch_refs
        self.b = pl.program_id(0)
        self.p = pl.program_id(1)
        self.n_steps = grid[0] * grid[1]
        self.step = self.b * grid[1] + self.p
        slot = lax.rem(self.step, PAGE_BUFFERS)
        self.k_refs = [self.kbuf.at[slot, g] for g in range(cfg.gp)]
        self.v_refs = [self.vbuf.at[slot, g] for g in range(cfg.gp)]

    def _page_copies(self, step):
        c = self.c
        slot = step % PAGE_BUFFERS if isinstance(step, int) else lax.rem(step, PAGE_BUFFERS)
        copies = []
        for g in range(c.gp):
            page = self.pt_ref[c.first * c.n_pages + step * c.gp + g]
            copies.append(pltpu.make_async_copy(self.kt_hbm.at[c.layer, page],
                                                self.kbuf.at[slot, g], self.sem.at[slot, 0]))
            copies.append(pltpu.make_async_copy(self.vr_hbm.at[c.layer, page],
                                                self.vbuf.at[slot, g], self.sem.at[slot, 1]))
        return copies

    def preamble(self):
        c = self.c
        ahead = PAGE_BUFFERS - 1

        @pl.when(self.step == 0)
        def _():
            for s in range(min(ahead, self.n_steps)):
                for cp in self._page_copies(s):
                    cp.start()

        @pl.when(self.step + ahead < self.n_steps)
        def _():
            for cp in self._page_copies(self.step + ahead):
                cp.start()

        for cp in self._page_copies(self.step):
            cp.wait()

        @pl.when((self.b == 0) & (self.p == 0))
        def _():
            self.kpad[...] = jnp.zeros(self.kpad.shape, F32)
            self.vpad[...] = jnp.zeros(self.vpad.shape, F32)

        @pl.when(self.p == 0)
        def _():
            self.m_s[...] = jnp.full(self.m_s.shape, -jnp.inf, F32)
            self.l_s[...] = jnp.zeros(self.l_s.shape, F32)
            self.acc_s[...] = jnp.zeros(self.acc_s.shape, F32)
            q = self.q_ref[...].astype(F32)
            lane = lax.broadcasted_iota(jnp.int32, q.shape, 1)
            for h in range(c.n_sub):
                own = (lane >= h * c.head_dim) & (lane < (h + 1) * c.head_dim)
                self.qf_s[h * c.t_new:(h + 1) * c.t_new, :] = jnp.where(own, q, 0.0)
            self.qx_s[...] = self.qf_s[...].astype(BF16)

    def _update(self, sc, vs):
        c = self.c
        m_prev = self.m_s[...]
        m_new = jnp.maximum(m_prev, jnp.max(sc, axis=1, keepdims=True))
        alpha = jnp.exp(m_prev - m_new)
        pe = jnp.exp(sc - m_new)
        self.l_s[...] = alpha * self.l_s[...] + jnp.sum(pe, axis=1, keepdims=True)
        self.m_s[...] = m_new
        pb = pe.astype(BF16)
        grp = 2 * c.t_new
        for hv in range(c.n_heads):
            rs = slice(hv * grp, (hv + 1) * grp)
            vh = jnp.concatenate(
                [v[pl.ds(hv, c.page, stride=c.n_heads), :].astype(BF16) for v in vs], axis=0)
            self.acc_s[rs, :] = alpha[rs, :] * self.acc_s[rs, :] + _dot(pb[rs, :], vh)

    def main(self):
        kt = jnp.concatenate([k[...].astype(BF16) for k in self.k_refs], axis=1)
        self._update(_dot(self.qx_s[...], kt), self.v_refs)

    def finalize(self):
        c = self.c

        @pl.when(self.p == c.spb - 1)
        def _():
            self.kpad[0:c.t_new, :] = self.kn_ref[...]
            self.vpad[0:c.t_new * c.n_heads, :] = self.vn_ref[...]
            sc = _dot_nt(self.qx_s[...], self.kpad[...].astype(BF16))
            row = lax.broadcasted_iota(jnp.int32, sc.shape, 0)
            col = lax.broadcasted_iota(jnp.int32, sc.shape, 1)
            self._update(jnp.where(col <= row % c.t_new, sc, -jnp.inf), [self.vpad])
            o = self.acc_s[...] / self.l_s[...]
            lam = _diff_lambda(*self.l_refs, c.lam_init)
            grp = 2 * c.t_new
            for hv in range(c.n_heads):
                o1 = o[hv * grp:hv * grp + c.t_new, :]
                o2 = o[hv * grp + c.t_new:(hv + 1) * grp, :]
                self.o_ref[:, hv * c.vd:(hv + 1) * c.vd] = o1 - lam * o2


def _decode_call(cfg, n_seq):
    grid = (n_seq, cfg.spb)
    in_specs = cfg.in_specs(grid)
    n_in = len(in_specs)

    def body(pt_ref, *refs):
        step = _DecodeStep(cfg, grid, pt_ref, refs[:n_in], refs[n_in], refs[n_in + 1:])
        step.preamble()
        step.main()
        step.finalize()

    grid_spec = pltpu.PrefetchScalarGridSpec(
        num_scalar_prefetch=1, grid=grid, in_specs=in_specs, out_specs=cfg.out_spec(grid),
        scratch_shapes=cfg.scratch_shapes())
    return pl.pallas_call(body, grid_spec=grid_spec, out_shape=cfg.out_shape(grid),
                          compiler_params=_cparams(("arbitrary", "arbitrary")),
                          name="attn_decode")(cfg.pt, *cfg.args())


def _call(body, grid, in_specs, out_specs, out_shape, scratch_shapes, args, name):
    return pl.pallas_call(body, grid=grid, in_specs=in_specs, out_specs=out_specs,
                          out_shape=out_shape, scratch_shapes=scratch_shapes,
                          compiler_params=_cparams(("arbitrary",) * len(grid)), name=name)(*args)


def _attn_out_body(x_ref, gt_ref, o_ref, z_ref, sg_ref, wout_ref, g_ref, b_ref, y_ref, gbuf, *,
                   alpha, out_scale):
    d = x_ref.shape[-1]
    vd = sg_ref.shape[-1]
    for hv in range(d // vd):
        cols = slice(hv * vd, (hv + 1) * vd)
        of = o_ref[:, cols].astype(F32)
        of = of * lax.rsqrt(jnp.mean(of * of, axis=-1, keepdims=True) + SUBLN_EPS)
        of = of * sg_ref[...] * out_scale
        gbuf[:, cols] = (of * _silu(z_ref[:, cols].astype(F32))).astype(BF16)
    out = _dot(gbuf[...], wout_ref[...])
    res = alpha * x_ref[...] + gt_ref[...] * out
    y_ref[...] = _layer_norm(res, g_ref[...], b_ref[...])


def _attn_out_call(x, mod, layer, o, z, subln_g, w_out, ln_g, ln_b, *, tm, alpha, out_scale):
    bsz, t_rows, d = x.shape
    vd = subln_g.shape[-1]
    body = functools.partial(_attn_out_body, alpha=alpha, out_scale=out_scale)
    row_spec = pl.BlockSpec((None, tm, d), lambda b, t: (b, t, 0))
    const = lambda shape: pl.BlockSpec(shape, lambda b, t: (0,) * len(shape))
    return pl.pallas_call(
        body,
        grid=(bsz, t_rows // tm),
        in_specs=[row_spec, _mod_spec(mod, layer, 2), row_spec, row_spec, const((1, vd)),
                  const(w_out.shape),
                  const((1, d)), const((1, d))],
        out_specs=row_spec,
        out_shape=jax.ShapeDtypeStruct((bsz, t_rows, d), F32),
        scratch_shapes=[pltpu.VMEM((tm, d), BF16)],
        compiler_params=_cparams(("arbitrary", "arbitrary")),
        name="attn_out",
    )(x, mod, o, z, subln_g.reshape(1, vd), w_out, ln_g.reshape(1, d), ln_b.reshape(1, d))


def kernel(x_prompt, x_sample, state_conv_a, state_lru_h, state_lru_conv, cache_k, cache_v, page_table, c_prompt, c_sample, w_ada, b_ada, ln_g, ln_b, a_w_in, a_conv_w, a_w_out, r_w_in, r_conv_w, r_conv_b, r_w_ga, r_b_ga, r_w_gx, r_b_gx, r_lru_param, r_w_out, d_w_in, d_lq1, d_lk1, d_lq2, d_lk2, d_subln_g, d_w_out):
    bp, tp, d = x_prompt.shape
    bs, ts, _ = x_sample.shape
    depth = w_ada.shape[0]
    n_pages = page_table.shape[1]
    page = cache_k.shape[2]
    n_sub, head_dim = cache_k.shape[3], cache_k.shape[4]
    n_heads, vd = cache_v.shape[3], cache_v.shape[4]
    past_len = n_pages * page
    rot_dim = head_dim // 4
    alpha = (2 * depth) ** 0.25
    rows_s = bs * ts

    n_c = _round_up(bp + bs, SUBLANES)
    c_all = jnp.concatenate([c_sample, c_prompt, jnp.zeros((n_c - bp - bs, d), F32)], axis=0)
    mod_p, mod_s = _ada_call(c_all, w_ada, b_ada, bp=bp, bs=bs, ts=ts)

    def to_time_major(a):
        return jnp.swapaxes(a, 0, 1).reshape((1, a.shape[1] * bs) + a.shape[2:])

    def from_time_major(a, n):
        return jnp.swapaxes(a.reshape(n, bs, a.shape[-1]), 0, 1)

    bf = lambda w: w.astype(BF16)
    half = rot_dim // 2
    qscale = head_dim ** -0.5

    conv_p, conv_s, lruh_p, lruh_s, lruc_p, lruc_s = [], [], [], [], [], []
    k_p, v_p, k_s, v_s = [], [], [], []

    def layer_weights(i):
        kind, j = i % N_MIXERS, i // N_MIXERS
        if kind == 0:
            return (bf(a_w_in[j]), a_conv_w[j], bf(a_w_out[j]), ln_g[i], ln_b[i])
        if kind == 1:
            return (bf(r_w_in[j]), r_conv_w[j], r_conv_b[j], bf(r_w_ga[j]), r_b_ga[j],
                    bf(r_w_gx[j]), r_b_gx[j], r_lru_param[j], bf(r_w_out[j]), ln_g[i], ln_b[i])
        return (bf(d_w_in[j]), bf(d_w_out[j]),
                [v[j].reshape(1, head_dim) for v in (d_lq1, d_lk1, d_lq2, d_lk2)],
                0.8 - 0.6 * math.exp(-0.3 * i))

    xp = x_prompt
    xs = to_time_major(x_sample)
    for i in range(depth):
        kind, j = i % N_MIXERS, i // N_MIXERS
        w = layer_weights(i)
        if kind == 0:
            width = a_conv_w.shape[1]
            xp, nbp = _conv_layer(xp, mod_p, i, jnp.zeros((bp, width - 1, d), F32), *w,
                                  s=1, tm=512, alpha=alpha)
            xs, nbs = _conv_layer(xs, mod_s, i, to_time_major(state_conv_a[j]), *w,
                                  s=bs, tm=rows_s, alpha=alpha)
            conv_p.append(nbp)
            conv_s.append(from_time_major(nbs, width - 1))
        elif kind == 1:
            width = r_conv_w.shape[1]
            xp, nbp, hp = _lru_layer(xp, mod_p, i, jnp.zeros((bp, width - 1, d), F32),
                                     jnp.zeros((bp, 1, d), F32), *w, s=1, tm=512, alpha=alpha)
            xs, nbs, hs = _lru_layer(xs, mod_s, i, to_time_major(state_lru_conv[j]),
                                     state_lru_h[j].reshape(1, bs, d), *w, s=bs, tm=rows_s,
                                     alpha=alpha)
            lruh_p.append(hp.reshape(bp, d))
            lruh_s.append(hs.reshape(bs, d))
            lruc_p.append(nbp)
            lruc_s.append(from_time_major(nbs, width - 1))
        else:
            w_in, w_out, lvec, lam_init = w
            tabs_p = _rope_tables(np.arange(tp), head_dim, rot_dim)
            kp, vp, zp, qb, kb, vt = _qkv_call(xp, mod_p, i, *tabs_p, w_in, tm=256, half=half,
                                               qscale=qscale * math.log2(math.e), attn_vd=vd)
            op = _flash_call(qb, kb, vt, *lvec, tq=1024, head_dim=head_dim, nh=4,
                             lam_init=lam_init)
            xp = _attn_out_call(xp, mod_p, i, op, zp, d_subln_g[j], w_out, ln_g[i], ln_b[i],
                                tm=512, alpha=alpha, out_scale=1.0 - lam_init)
            k_p.append(kp.reshape(bp, tp, n_sub, head_dim))
            v_p.append(vp.reshape(bp, tp, n_heads, vd))
            pos_s = past_len + np.repeat(np.arange(ts), bs)
            tabs_s = _rope_tables(pos_s, head_dim, rot_dim)
            ks_, vs_, zs, qsb = _qkv_call(xs, mod_s, i, *tabs_s, w_in, tm=rows_s, half=half,
                                          qscale=qscale)
            ksn = from_time_major(ks_, ts)
            vsn = from_time_major(vs_, ts)
            n_layers, n_phys = cache_k.shape[0], cache_k.shape[1]
            cache_kt = jnp.transpose(cache_k, (0, 1, 3, 4, 2)).reshape(
                n_layers, n_phys, n_sub * head_dim, page)
            cache_vr = cache_v.reshape(n_layers, n_phys, page * n_heads, vd)
            cfg = _DecodeConfig(page_table, from_time_major(qsb, ts), cache_kt, cache_vr, j, ksn,
                                vsn.reshape(bs, ts * n_heads, vd), lvec, first=0, gp=16,
                                n_heads=n_heads, lam_init=lam_init)
            os_ = _decode_call(cfg, bs)
            xs = _attn_out_call(xs, mod_s, i, to_time_major(os_), zs, d_subln_g[j], w_out,
                                ln_g[i], ln_b[i], tm=rows_s, alpha=alpha,
                                out_scale=1.0 - lam_init)
            k_s.append(ksn.reshape(bs, ts, n_sub, head_dim))
            v_s.append(vsn.reshape(bs, ts, n_heads, vd))

    return (xp, from_time_major(xs, ts),
            jnp.stack(conv_p), jnp.stack(conv_s),
            jnp.stack(lruh_p), jnp.stack(lruh_s),
            jnp.stack(lruc_p), jnp.stack(lruc_s),
            jnp.stack(k_p), jnp.stack(v_p), jnp.stack(k_s), jnp.stack(v_s))
```

```python
import functools
import math

import jax
import jax.numpy as jnp
import numpy as np
from jax import lax
from jax.experimental import pallas as pl
from jax.experimental.pallas import tpu as pltpu

F32 = jnp.float32
BF16 = jnp.bfloat16

LN_EPS = 1e-5
SUBLN_EPS = 1e-5
LRU_C = 8.0
ROPE_THETA = 500000.0
N_MIXERS = 3

SUBLANES = 8
LANES = 128
VMEM_LIMIT_BYTES = 56 * 1024 * 1024
PAGE_BUFFERS = 2


def _cparams(semantics):
    return pltpu.CompilerParams(dimension_semantics=semantics, vmem_limit_bytes=VMEM_LIMIT_BYTES)


def _dot(a, b):
    return jnp.dot(a, b, preferred_element_type=F32)


def _dot_nt(a, b):
    return lax.dot_general(a, b, (((1,), (1,)), ((), ())), preferred_element_type=F32)


def _silu(z):
    return z * jax.nn.sigmoid(z)


def _layer_norm(y, g, b):
    mu = jnp.mean(y, axis=-1, keepdims=True)
    yc = y - mu
    var = jnp.mean(yc * yc, axis=-1, keepdims=True)
    return yc * lax.rsqrt(var + LN_EPS) * g + b


def _round_up(n, m):
    return (n + m - 1) // m * m


def _ada_body(c_ref, w_ref, b_ref, op_ref, os_ref, *, bp, bs, ts):
    c = c_ref[...]
    a = _silu(c).astype(BF16)
    m = _dot(a, w_ref[...].astype(BF16)) + b_ref[...]
    for r in range(bp):
        op_ref[r] = m[bs + r:bs + r + 1, :]
    for t in range(ts):
        os_ref[t * bs:(t + 1) * bs, :] = m[0:bs, :]


def _ada_call(c_all, w_ada, b_ada, *, bp, bs, ts):
    depth, d, d3 = w_ada.shape
    rows = c_all.shape[0]
    nt = d3 // d
    return pl.pallas_call(
        functools.partial(_ada_body, bp=bp, bs=bs, ts=ts),
        grid=(depth, nt),
        in_specs=[
            pl.BlockSpec((rows, d), lambda i, n: (0, 0)),
            pl.BlockSpec((None, d, d), lambda i, n: (i, 0, n)),
            pl.BlockSpec((None, 1, d), lambda i, n: (i, 0, n)),
        ],
        out_specs=[pl.BlockSpec((None, None, bp, 1, d), lambda i, n: (i, n, 0, 0, 0)),
                   pl.BlockSpec((None, None, None, ts * bs, d), lambda i, n: (i, n, 0, 0, 0))],
        out_shape=[jax.ShapeDtypeStruct((depth, nt, bp, 1, d), F32),
                   jax.ShapeDtypeStruct((depth, nt, 1, ts * bs, d), F32)],
        compiler_params=_cparams(("arbitrary", "arbitrary")),
        name="adaln",
    )(c_all, w_ada, b_ada.reshape(depth, 1, d3))


def _mod_spec(mod, layer, k):
    _, _, _, r, d = mod.shape
    return pl.BlockSpec((None, None, None, r, d), lambda b, t, *_: (layer, k, b, 0, 0))


def _load_history(t, buf, st0_ref, pad, ks, tm):
    @pl.when(t == 0)
    def _():
        buf[pad - ks:pad, :] = st0_ref[...]

    @pl.when(t > 0)
    def _():
        buf[pad - ks:pad, :] = buf[pad + tm - ks:pad + tm, :]


def _conv_taps(buf, cw_ref, cols, width, s, pad, tm):
    y = None
    if s % SUBLANES == 0:
        for k in range(width):
            r0 = pad - (width - 1 - k) * s
            term = cw_ref[k:k + 1, cols] * buf[r0:r0 + tm, cols]
            y = term if y is None else y + term
        return y
    full = buf[0:pad + tm, cols]
    for k in range(width):
        back = (width - 1 - k) * s
        src = full if back == 0 else pltpu.roll(full, back, axis=0)
        term = cw_ref[k:k + 1, cols] * src[pad:pad + tm, :]
        y = term if y is None else y + term
    return y


def _conv_body(x_ref, sh_ref, sc_ref, gt_ref, st0_ref, win_ref, cw_ref, wout_ref, g_ref, b_ref,
               y_ref, st_ref, abuf, acc, *, s, tm, cw, width, pad, alpha):
    d = x_ref.shape[-1]
    ks = (width - 1) * s
    t = pl.program_id(1)
    _load_history(t, abuf, st0_ref, pad, ks, tm)
    x = x_ref[...]
    u = (x * (1.0 + sc_ref[...]) + sh_ref[...]).astype(BF16)
    for c in range(d // cw):
        cols = slice(c * cw, (c + 1) * cw)
        h = _dot(u, win_ref[:, c * cw:(c + 1) * cw])
        cg = _dot(u, win_ref[:, 2 * d + c * cw:2 * d + (c + 1) * cw])
        abuf[pad:pad + tm, cols] = cg * h
        y = _conv_taps(abuf, cw_ref, cols, width, s, pad, tm)
        bg = _dot(u, win_ref[:, d + c * cw:d + (c + 1) * cw])
        z = _dot(u, win_ref[:, 3 * d + c * cw:3 * d + (c + 1) * cw])
        gated = (_silu(z) * bg * y).astype(BF16)
        part = _dot(gated, wout_ref[cols, :])
        if c == 0:
            acc[...] = part
        else:
            acc[...] += part
    st_ref[...] = abuf[pad + tm - ks:pad + tm, :]
    res = alpha * x + gt_ref[...] * acc[...]
    y_ref[...] = _layer_norm(res, g_ref[...], b_ref[...])


def _conv_layer(x, mod, layer, st0, w_in, conv_w, w_out, ln_g, ln_b, *, s, tm, alpha):
    bsz, t_rows, d = x.shape
    width = conv_w.shape[0]
    ks = (width - 1) * s
    pad = _round_up(ks, SUBLANES)
    cw = 256
    body = functools.partial(_conv_body, s=s, tm=tm, cw=cw, width=width, pad=pad, alpha=alpha)
    row_spec = pl.BlockSpec((None, tm, d), lambda b, t, *_: (b, t, 0))
    st_spec = pl.BlockSpec((None, ks, d), lambda b, t, *_: (b, 0, 0))
    const = lambda shape: pl.BlockSpec(shape, lambda b, t, *_: (0,) * len(shape))
    return _call(
        body, (bsz, t_rows // tm),
        [row_spec, _mod_spec(mod, layer, 0), _mod_spec(mod, layer, 1),
         _mod_spec(mod, layer, 2), st_spec,
         const(w_in.shape), const(conv_w.shape), const(w_out.shape),
         const((1, d)), const((1, d))],
        [row_spec, st_spec],
        [jax.ShapeDtypeStruct((bsz, t_rows, d), F32), jax.ShapeDtypeStruct((bsz, ks, d), F32)],
        [pltpu.VMEM((pad + tm, d), F32), pltpu.VMEM((tm, d), F32)],
        (x, mod, mod, mod, st0, w_in, conv_w, w_out, ln_g.reshape(1, d), ln_b.reshape(1, d)),
        "conv_layer")


def _lru_body(x_ref, sh_ref, sc_ref, gt_ref, st0_ref, h0_ref, win_ref, cw_ref, cb_ref,
              wga_ref, bga_ref, wgx_ref, bgx_ref, prm_ref, wout_ref, g_ref, b_ref,
              y_ref, st_ref, hl_ref, xbuf, a_s, b_s, h_s, hc, *, s, tm, width, pad, alpha):
    d = x_ref.shape[-1]
    nblk, blk, _ = wga_ref.shape
    ks = (width - 1) * s
    t = pl.program_id(1)
    _load_history(t, xbuf, st0_ref, pad, ks, tm)

    @pl.when(t == 0)
    def _():
        hc[...] = h0_ref[...]

    x = x_ref[...]
    u = (x * (1.0 + sc_ref[...]) + sh_ref[...]).astype(BF16)
    xbuf[pad:pad + tm, :] = _dot(u, win_ref[:, 0:d])
    st_ref[...] = xbuf[pad + tm - ks:pad + tm, :]
    for n in range(nblk):
        cols = slice(n * blk, (n + 1) * blk)
        xc = _conv_taps(xbuf, cw_ref, cols, width, s, pad, tm) + cb_ref[:, cols]
        xcb = xc.astype(BF16)
        r = jax.nn.sigmoid(_dot(xcb, wga_ref[n]) + bga_ref[:, cols])
        gi = jax.nn.sigmoid(_dot(xcb, wgx_ref[n]) + bgx_ref[:, cols])
        log_a = r * (LRU_C * jax.nn.log_sigmoid(prm_ref[:, cols]))
        a = jnp.exp(log_a)
        a_s[:, cols] = a
        b_s[:, cols] = jnp.sqrt(-jnp.tanh(log_a) * (a * a + 1.0)) * (gi * xc)

    steps = tm // s

    def step(i, h):
        r0 = pl.multiple_of(i * s, s)
        h = a_s[pl.ds(r0, s), :] * h + b_s[pl.ds(r0, s), :]
        h_s[pl.ds(r0, s), :] = h
        return h

    h_last = lax.fori_loop(0, steps, step, hc[...], unroll=True)
    hc[...] = h_last
    hl_ref[...] = h_last

    z = _dot(u, win_ref[:, d:2 * d])
    yy = (h_s[...] * _silu(z)).astype(BF16)
    out = _dot(yy, wout_ref[...])
    res = alpha * x + gt_ref[...] * out
    y_ref[...] = _layer_norm(res, g_ref[...], b_ref[...])


def _lru_layer(x, mod, layer, st0, h0, w_in, conv_w, conv_b, w_ga, b_ga, w_gx, b_gx, prm, w_out,
               ln_g, ln_b, *, s, tm, alpha):
    bsz, t_rows, d = x.shape
    width = conv_w.shape[0]
    ks = (width - 1) * s
    pad = _round_up(ks, SUBLANES)
    body = functools.partial(_lru_body, s=s, tm=tm, width=width, pad=pad, alpha=alpha)
    row_spec = pl.BlockSpec((None, tm, d), lambda b, t, *_: (b, t, 0))
    st_spec = pl.BlockSpec((None, ks, d), lambda b, t, *_: (b, 0, 0))
    h_spec = pl.BlockSpec((None, s, d), lambda b, t, *_: (b, 0, 0))
    const = lambda shape: pl.BlockSpec(shape, lambda b, t, *_: (0,) * len(shape))
    vec = const((1, d))
    return _call(
        body, (bsz, t_rows // tm),
        [row_spec, _mod_spec(mod, layer, 0), _mod_spec(mod, layer, 1),
         _mod_spec(mod, layer, 2), st_spec, h_spec,
         const(w_in.shape), const(conv_w.shape), vec,
         const(w_ga.shape), vec, const(w_gx.shape), vec, vec,
         const(w_out.shape), vec, vec],
        [row_spec, st_spec, h_spec],
        [jax.ShapeDtypeStruct((bsz, t_rows, d), F32), jax.ShapeDtypeStruct((bsz, ks, d), F32),
         jax.ShapeDtypeStruct((bsz, s, d), F32)],
        [pltpu.VMEM((pad + tm, d), F32), pltpu.VMEM((tm, d), F32), pltpu.VMEM((tm, d), F32),
         pltpu.VMEM((tm, d), F32), pltpu.VMEM((s, d), F32)],
        (x, mod, mod, mod, st0, h0, w_in, conv_w, conv_b.reshape(1, d), w_ga,
         b_ga.reshape(1, d), w_gx, b_gx.reshape(1, d), prm.reshape(1, d), w_out,
         ln_g.reshape(1, d), ln_b.reshape(1, d)),
        "lru_layer")


def _rope_block(xb, cos, sin_lo, sin_hi, half):
    return (xb * cos + pltpu.roll(xb, LANES - half, axis=1) * sin_lo
            + pltpu.roll(xb, half, axis=1) * sin_hi)


def _qkv_body(x_ref, sh_ref, sc_ref, cos_ref, slo_ref, shi_ref, win_ref,
              k_ref, v_ref, z_ref, qb_ref, *attn_refs, half, qscale):
    d = x_ref.shape[-1]
    x = x_ref[...]
    u = (x * (1.0 + sc_ref[...]) + sh_ref[...]).astype(BF16)
    cos, slo, shi = cos_ref[...], slo_ref[...], shi_ref[...]
    q = _dot(u, win_ref[:, 0:d])
    for j in range(d // LANES):
        cols = slice(j * LANES, (j + 1) * LANES)
        qb_ref[:, cols] = (_rope_block(q[:, cols], cos, slo, shi, half) * qscale).astype(BF16)
    k = _dot(u, win_ref[:, d:2 * d])
    for j in range(d // LANES):
        cols = slice(j * LANES, (j + 1) * LANES)
        kr = _rope_block(k[:, cols], cos, slo, shi, half)
        k_ref[:, cols] = kr
        if attn_refs:
            attn_refs[0][:, cols] = kr.astype(BF16)
    v = _dot(u, win_ref[:, 2 * d:3 * d])
    v_ref[...] = v
    if attn_refs:
        vt_ref = attn_refs[1]
        _, n_kb, vd, tk = vt_ref.shape
        for hv in range(vt_ref.shape[0]):
            for c in range(n_kb):
                vt_ref[hv, c] = v[c * tk:(c + 1) * tk, hv * vd:(hv + 1) * vd].T.astype(BF16)
    z_ref[...] = _dot(u, win_ref[:, 3 * d:4 * d]).astype(z_ref.dtype)


def _qkv_call(x, mod, layer, cos_t, slo_t, shi_t, w_in, *, tm, half, qscale, attn_vd=None,
              attn_tk=None):
    bsz, t_rows, d = x.shape
    body = functools.partial(_qkv_body, half=half, qscale=qscale)
    row_spec = pl.BlockSpec((None, tm, d), lambda b, t: (b, t, 0))
    tab_spec = pl.BlockSpec((tm, LANES), lambda b, t: (t, 0))
    f32_out = jax.ShapeDtypeStruct((bsz, t_rows, d), F32)
    bf_out = jax.ShapeDtypeStruct((bsz, t_rows, d), BF16)
    out_specs = [row_spec] * 4
    out_shape = [f32_out, f32_out, bf_out, bf_out]
    if attn_vd is not None:
        n_heads = d // attn_vd
        out_specs += [row_spec, pl.BlockSpec((None, n_heads, tm // attn_tk, attn_vd, attn_tk),
                                             lambda b, t: (b, 0, t, 0, 0))]
        out_shape += [bf_out, jax.ShapeDtypeStruct(
            (bsz, n_heads, t_rows // attn_tk, attn_vd, attn_tk), BF16)]
    return pl.pallas_call(
        body,
        grid=(bsz, t_rows // tm),
        in_specs=[row_spec, _mod_spec(mod, layer, 0), _mod_spec(mod, layer, 1),
                  tab_spec, tab_spec, tab_spec,
                  pl.BlockSpec(w_in.shape, lambda b, t: (0, 0))],
        out_specs=out_specs,
        out_shape=out_shape,
        compiler_params=_cparams(("arbitrary", "arbitrary")),
        name="attn_qkv",
    )(x, mod, mod, cos_t, slo_t, shi_t, w_in)


def _rope_tables(pos, head_dim, rot_dim):
    half = rot_dim // 2
    inv_freq = np.exp(np.arange(half, dtype=np.float64) * (-2.0 * math.log(ROPE_THETA) / rot_dim))
    ang = np.asarray(pos, dtype=np.float64)[:, None] * inv_freq[None, :]
    cos, sin = np.cos(ang), np.sin(ang)
    n = ang.shape[0]
    ones = np.ones((n, head_dim - rot_dim))
    zeros = np.zeros((n, head_dim - rot_dim))
    zh = np.zeros((n, half))
    cos_h = np.concatenate([cos, cos, ones], axis=1)
    slo_h = np.concatenate([-sin, zh, zeros], axis=1)
    shi_h = np.concatenate([zh, sin, zeros], axis=1)
    rep = LANES // head_dim
    return tuple(jnp.asarray(np.tile(t, (1, rep)), dtype=F32) for t in (cos_h, slo_h, shi_h))


def _diff_lambda(lq1_ref, lk1_ref, lq2_ref, lk2_ref, lam_init):
    s1 = jnp.sum(lq1_ref[...] * lk1_ref[...], axis=-1, keepdims=True)
    s2 = jnp.sum(lq2_ref[...] * lk2_ref[...], axis=-1, keepdims=True)
    return jnp.exp(s1) - jnp.exp(s2) + lam_init


def _flash_body(q_ref, k_ref, vt_ref, lq1_ref, lk1_ref, lq2_ref, lk2_ref, o_ref,
                qq_s, sa_s, sb_s, m_s, acc_s, *, tq, head_dim, nh, lam_init):
    qi = pl.program_id(2)
    vd, tk = vt_ref.shape[2], vt_ref.shape[3]
    nd = tq // tk
    lane = lax.broadcasted_iota(jnp.int32, (tq, LANES), 1)
    for h in range(nh):
        q = q_ref[:, h * LANES:(h + 1) * LANES]
        zero = jnp.zeros_like(q)
        qq_s[h, 0:tq, :] = jnp.where(lane < head_dim, q, zero)
        qq_s[h, tq:2 * tq, :] = jnp.where(lane >= head_dim, q, zero)
    m_s[...] = jnp.full(m_s.shape, -jnp.inf, F32)
    acc_s[...] = jnp.zeros(acc_s.shape, F32)
    ones = jnp.ones((2 * SUBLANES, tk), BF16)

    def scores(j, s_ref, q0=0):
        r0 = pl.multiple_of(j * tk, tk)
        for h in range(nh):
            k = k_ref[pl.ds(r0, tk), h * LANES:(h + 1) * LANES]
            if q0 == 0:
                s_ref[h] = _dot_nt(k, qq_s[h])
            else:
                qq = jnp.concatenate([qq_s[h, q0:tq, :], qq_s[h, tq + q0:2 * tq, :]], axis=0)
                s_ref[h, :, 0:2 * (tq - q0)] = _dot_nt(k, qq)

    def consume(j, s_ref, diag=False, q0=0):
        w = tq - q0
        for h in range(nh):
            v1 = jnp.concatenate([vt_ref[h, j], ones], axis=0)
            for sub in range(2):
                for cc in range(w // tk):
                    src = slice(sub * w + cc * tk, sub * w + (cc + 1) * tk)
                    dst = slice(sub * tq + q0 + cc * tk, sub * tq + q0 + (cc + 1) * tk)
                    st = s_ref[h, :, src]
                    if diag and cc == 0:
                        key = lax.broadcasted_iota(jnp.int32, st.shape, 0)
                        row = lax.broadcasted_iota(jnp.int32, st.shape, 1)
                        st = jnp.where(key <= row, st, -jnp.inf)
                    m_prev = m_s[h, :, dst]
                    m_new = jnp.maximum(m_prev, jnp.max(st, axis=0, keepdims=True))
                    alpha = jnp.exp2(m_prev - m_new)
                    pt = jnp.exp2(st - m_new).astype(BF16)
                    acc_s[h, :, dst] = alpha * acc_s[h, :, dst] + _dot(v1, pt)
                    m_s[h, :, dst] = m_new

    scores(0, sa_s)

    def pair(i, carry):
        j = 2 * i
        scores(j + 1, sb_s)
        consume(j, sa_s)
        scores(j + 2, sa_s)
        consume(j + 1, sb_s)
        return carry

    lax.fori_loop(0, qi * (nd // 2), pair, 0)

    bufs = (sa_s, sb_s)
    for g in range(nd):
        j = nd * qi + g
        if g + 1 < nd:
            scores(j + 1, bufs[(g + 1) % 2], q0=(g + 1) * tk)
        consume(j, bufs[g % 2], diag=True, q0=g * tk)

    lam = _diff_lambda(lq1_ref, lk1_ref, lq2_ref, lk2_ref, lam_init)
    for h in range(nh):
        acc = acc_s[h]
        ot = acc[0:vd, :] / acc[vd:vd + 1, :]
        o_ref[:, h * LANES:(h + 1) * LANES] = (
            (ot[:, 0:tq] - lam * ot[:, tq:2 * tq]).T.astype(o_ref.dtype))


def _flash_call(qb, kb, vt, lq1, lk1, lq2, lk2, *, tq, head_dim, nh, lam_init):
    bsz, t_rows, d = qb.shape
    _, n_heads, n_blk, vd, tk = vt.shape
    assert tq % (2 * tk) == 0 and n_blk * tk == t_rows and vd == LANES
    body = functools.partial(_flash_body, tq=tq, head_dim=head_dim, nh=nh, lam_init=lam_init)
    q_spec = pl.BlockSpec((None, tq, nh * LANES), lambda b, h, i, *_: (b, i, h))
    k_spec = pl.BlockSpec((None, t_rows, nh * LANES), lambda b, h, i, *_: (b, 0, h))
    vt_spec = pl.BlockSpec((None, nh, n_blk, vd, tk), lambda b, h, i, *_: (b, h, 0, 0, 0))
    l_spec = pl.BlockSpec((1, head_dim), lambda b, h, i, *_: (0, 0))
    return _call(
        body, (bsz, n_heads // nh, t_rows // tq),
        [q_spec, k_spec, vt_spec, l_spec, l_spec, l_spec, l_spec],
        q_spec,
        jax.ShapeDtypeStruct((bsz, t_rows, d), BF16),
        [pltpu.VMEM((nh, 2 * tq, LANES), BF16), pltpu.VMEM((nh, tk, 2 * tq), F32),
         pltpu.VMEM((nh, tk, 2 * tq), F32), pltpu.VMEM((nh, 1, 2 * tq), F32),
         pltpu.VMEM((nh, vd + 2 * SUBLANES, 2 * tq), F32)],
        (qb, kb, vt, lq1, lk1, lq2, lk2),
        "attn_flash")


class _DecodeConfig:
    def __init__(self, page_table, qs, cache_kt, cache_vr, layer, k_new, v_new, lvec, *, first,
                 gp, n_heads, lam_init):
        self.pt = page_table.reshape(-1)
        self.n_pages = page_table.shape[1]
        self.qs, self.kt, self.vr, self.layer = qs, cache_kt, cache_vr, layer
        self.k_new, self.v_new, self.lvec = k_new, v_new, list(lvec)
        self.first, self.gp, self.n_heads, self.lam_init = first, gp, n_heads, lam_init
        self.kd, self.page = cache_kt.shape[2], cache_kt.shape[3]
        self.vd = cache_vr.shape[-1]
        self.t_new = k_new.shape[1]
        self.head_dim = lvec[0].shape[-1]
        self.n_sub = self.kd // self.head_dim
        self.rows = self.n_sub * self.t_new
        self.spb = self.n_pages // gp

    def count(self, grid):
        assert len(grid) == 2 and grid[1] == self.spb
        return grid[0]

    def in_specs(self, grid):
        first = self.first
        self.count(grid)

        def per_seq(shape):
            return pl.BlockSpec((None,) + shape, lambda b, p, pt: (first + b, 0, 0))

        l_spec = pl.BlockSpec((1, self.head_dim), lambda *a: (0, 0))
        in_hbm = pl.BlockSpec(memory_space=pl.ANY)
        return [per_seq((self.t_new, self.kd)), in_hbm, in_hbm,
                per_seq((self.t_new, self.kd)), per_seq((self.t_new * self.n_heads, self.vd)),
                l_spec, l_spec, l_spec, l_spec]

    def args(self):
        return [self.qs, self.kt, self.vr, self.k_new, self.v_new] + self.lvec

    def out_spec(self, grid):
        return pl.BlockSpec((None, self.t_new, self.n_heads * self.vd),
                            lambda b, p, pt: (b, 0, 0))

    def out_shape(self, grid):
        return jax.ShapeDtypeStruct((self.count(grid), self.t_new, self.n_heads * self.vd), F32)

    def scratch_shapes(self):
        return [pltpu.VMEM((self.rows, self.kd), F32), pltpu.VMEM((self.rows, self.kd), BF16),
                pltpu.VMEM((self.rows, 1), F32), pltpu.VMEM((self.rows, 1), F32),
                pltpu.VMEM((self.rows, self.vd), F32),
                pltpu.VMEM((self.page, self.kd), F32),
                pltpu.VMEM((self.page * self.n_heads, self.vd), F32),
                pltpu.VMEM((PAGE_BUFFERS, self.gp, self.kd, self.page), F32),
                pltpu.VMEM((PAGE_BUFFERS, self.gp, self.page * self.n_heads, self.vd), F32),
                pltpu.SemaphoreType.DMA((PAGE_BUFFERS, 2))]


class _DecodeStep:
    def __init__(self, cfg, grid, pt_ref, in_refs, out_ref, scratch_refs):
        self.c = cfg
        self.pt_ref = pt_ref
        (self.q_ref, self.kt_hbm, self.vr_hbm, self.kn_ref, self.vn_ref) = in_refs[:5]
        self.l_refs = in_refs[5:]
        self.o_ref = out_ref
        (self.qf_s, self.qx_s, self.m_s, self.l_s, self.acc_s, self.kpad, self.vpad,
         self.kbuf, self.vbuf, self.sem) = scratch_refs
        self.b = pl.program_id(0)
        self.p = pl.program_id(1)
        self.n_steps = grid[0] * grid[1]
        self.step = self.b * grid[1] + self.p
        slot = lax.rem(self.step, PAGE_BUFFERS)
        self.k_refs = [self.kbuf.at[slot, g] for g in range(cfg.gp)]
        self.v_refs = [self.vbuf.at[slot, g] for g in range(cfg.gp)]

    def _page_copies(self, step):
        c = self.c
        slot = step % PAGE_BUFFERS if isinstance(step, int) else lax.rem(step, PAGE_BUFFERS)
        copies = []
        for g in range(c.gp):
            page = self.pt_ref[c.first * c.n_pages + step * c.gp + g]
            copies.append(pltpu.make_async_copy(self.kt_hbm.at[c.layer, page],
                                                self.kbuf.at[slot, g], self.sem.at[slot, 0]))
            copies.append(pltpu.make_async_copy(self.vr_hbm.at[c.layer, page],
                                                self.vbuf.at[slot, g], self.sem.at[slot, 1]))
        return copies

    def preamble(self):
        c = self.c
        ahead = PAGE_BUFFERS - 1

        @pl.when(self.step == 0)
        def _():
            for s in range(min(ahead, self.n_steps)):
                for cp in self._page_copies(s):
                    cp.start()

        @pl.when(self.step + ahead < self.n_steps)
        def _():
            for cp in self._page_copies(self.step + ahead):
                cp.start()

        for cp in self._page_copies(self.step):
            cp.wait()

        @pl.when((self.b == 0) & (self.p == 0))
        def _():
            self.kpad[...] = jnp.zeros(self.kpad.shape, F32)
            self.vpad[...] = jnp.zeros(self.vpad.shape, F32)

        @pl.when(self.p == 0)
        def _():
            self.m_s[...] = jnp.full(self.m_s.shape, -jnp.inf, F32)
            self.l_s[...] = jnp.zeros(self.l_s.shape, F32)
            self.acc_s[...] = jnp.zeros(self.acc_s.shape, F32)
            q = self.q_ref[...].astype(F32)
            lane = lax.broadcasted_iota(jnp.int32, q.shape, 1)
            for h in range(c.n_sub):
                own = (lane >= h * c.head_dim) & (lane < (h + 1) * c.head_dim)
                self.qf_s[h * c.t_new:(h + 1) * c.t_new, :] = jnp.where(own, q, 0.0)
            self.qx_s[...] = self.qf_s[...].astype(BF16)

    def _update(self, sc, vs):
        c = self.c
        m_prev = self.m_s[...]
        m_new = jnp.maximum(m_prev, jnp.max(sc, axis=1, keepdims=True))
        alpha = jnp.exp(m_prev - m_new)
        pe = jnp.exp(sc - m_new)
        self.l_s[...] = alpha * self.l_s[...] + jnp.sum(pe, axis=1, keepdims=True)
        self.m_s[...] = m_new
        pb = pe.astype(BF16)
        grp = 2 * c.t_new
        for hv in range(c.n_heads):
            rs = slice(hv * grp, (hv + 1) * grp)
            vh = jnp.concatenate(
                [v[pl.ds(hv, c.page, stride=c.n_heads), :].astype(BF16) for v in vs], axis=0)
            self.acc_s[rs, :] = alpha[rs, :] * self.acc_s[rs, :] + _dot(pb[rs, :], vh)

    def main(self):
        kt = jnp.concatenate([k[...].astype(BF16) for k in self.k_refs], axis=1)
        self._update(_dot(self.qx_s[...], kt), self.v_refs)

    def finalize(self):
        c = self.c

        @pl.when(self.p == c.spb - 1)
        def _():
            self.kpad[0:c.t_new, :] = self.kn_ref[...]
            self.vpad[0:c.t_new * c.n_heads, :] = self.vn_ref[...]
            sc = _dot_nt(self.qx_s[...], self.kpad[...].astype(BF16))
            row = lax.broadcasted_iota(jnp.int32, sc.shape, 0)
            col = lax.broadcasted_iota(jnp.int32, sc.shape, 1)
            self._update(jnp.where(col <= row % c.t_new, sc, -jnp.inf), [self.vpad])
            o = self.acc_s[...] / self.l_s[...]
            lam = _diff_lambda(*self.l_refs, c.lam_init)
            grp = 2 * c.t_new
            for hv in range(c.n_heads):
                o1 = o[hv * grp:hv * grp + c.t_new, :]
                o2 = o[hv * grp + c.t_new:(hv + 1) * grp, :]
                self.o_ref[:, hv * c.vd:(hv + 1) * c.vd] = o1 - lam * o2


def _decode_call(cfg, n_seq):
    grid = (n_seq, cfg.spb)
    in_specs = cfg.in_specs(grid)
    n_in = len(in_specs)

    def body(pt_ref, *refs):
        step = _DecodeStep(cfg, grid, pt_ref, refs[:n_in], refs[n_in], refs[n_in + 1:])
        step.preamble()
        step.main()
        step.finalize()

    grid_spec = pltpu.PrefetchScalarGridSpec(
        num_scalar_prefetch=1, grid=grid, in_specs=in_specs, out_specs=cfg.out_spec(grid),
        scratch_shapes=cfg.scratch_shapes())
    return pl.pallas_call(body, grid_spec=grid_spec, out_shape=cfg.out_shape(grid),
                          compiler_params=_cparams(("arbitrary", "arbitrary")),
                          name="attn_decode")(cfg.pt, *cfg.args())


def _call(body, grid, in_specs, out_specs, out_shape, scratch_shapes, args, name):
    return pl.pallas_call(body, grid=grid, in_specs=in_specs, out_specs=out_specs,
                          out_shape=out_shape, scratch_shapes=scratch_shapes,
                          compiler_params=_cparams(("arbitrary",) * len(grid)), name=name)(*args)


def _attn_out_body(x_ref, gt_ref, o_ref, z_ref, sg_ref, wout_ref, g_ref, b_ref, y_ref, gbuf, *,
                   alpha, out_scale):
    d = x_ref.shape[-1]
    vd = sg_ref.shape[-1]
    for hv in range(d // vd):
        cols = slice(hv * vd, (hv + 1) * vd)
        of = o_ref[:, cols].astype(F32)
        of = of * lax.rsqrt(jnp.mean(of * of, axis=-1, keepdims=True) + SUBLN_EPS)
        of = of * sg_ref[...] * out_scale
        gbuf[:, cols] = (of * _silu(z_ref[:, cols].astype(F32))).astype(BF16)
    out = _dot(gbuf[...], wout_ref[...])
    res = alpha * x_ref[...] + gt_ref[...] * out
    y_ref[...] = _layer_norm(res, g_ref[...], b_ref[...])


def _attn_out_call(x, mod, layer, o, z, subln_g, w_out, ln_g, ln_b, *, tm, alpha, out_scale):
    bsz, t_rows, d = x.shape
    vd = subln_g.shape[-1]
    body = functools.partial(_attn_out_body, alpha=alpha, out_scale=out_scale)
    row_spec = pl.BlockSpec((None, tm, d), lambda b, t: (b, t, 0))
    const = lambda shape: pl.BlockSpec(shape, lambda b, t: (0,) * len(shape))
    return pl.pallas_call(
        body,
        grid=(bsz, t_rows // tm),
        in_specs=[row_spec, _mod_spec(mod, layer, 2), row_spec, row_spec, const((1, vd)),
                  const(w_out.shape),
                  const((1, d)), const((1, d))],
        out_specs=row_spec,
        out_shape=jax.ShapeDtypeStruct((bsz, t_rows, d), F32),
        scratch_shapes=[pltpu.VMEM((tm, d), BF16)],
        compiler_params=_cparams(("arbitrary", "arbitrary")),
        name="attn_out",
    )(x, mod, o, z, subln_g.reshape(1, vd), w_out, ln_g.reshape(1, d), ln_b.reshape(1, d))


def kernel(x_prompt, x_sample, state_conv_a, state_lru_h, state_lru_conv, cache_k, cache_v, page_table, c_prompt, c_sample, w_ada, b_ada, ln_g, ln_b, a_w_in, a_conv_w, a_w_out, r_w_in, r_conv_w, r_conv_b, r_w_ga, r_b_ga, r_w_gx, r_b_gx, r_lru_param, r_w_out, d_w_in, d_lq1, d_lk1, d_lq2, d_lk2, d_subln_g, d_w_out):
    bp, tp, d = x_prompt.shape
    bs, ts, _ = x_sample.shape
    depth = w_ada.shape[0]
    n_pages = page_table.shape[1]
    page = cache_k.shape[2]
    n_sub, head_dim = cache_k.shape[3], cache_k.shape[4]
    n_heads, vd = cache_v.shape[3], cache_v.shape[4]
    past_len = n_pages * page
    rot_dim = head_dim // 4
    alpha = (2 * depth) ** 0.25
    rows_s = bs * ts

    n_c = _round_up(bp + bs, SUBLANES)
    c_all = jnp.concatenate([c_sample, c_prompt, jnp.zeros((n_c - bp - bs, d), F32)], axis=0)
    mod_p, mod_s = _ada_call(c_all, w_ada, b_ada, bp=bp, bs=bs, ts=ts)

    def to_time_major(a):
        return jnp.swapaxes(a, 0, 1).reshape((1, a.shape[1] * bs) + a.shape[2:])

    def from_time_major(a, n):
        return jnp.swapaxes(a.reshape(n, bs, a.shape[-1]), 0, 1)

    bf = lambda w: w.astype(BF16)
    half = rot_dim // 2
    qscale = head_dim ** -0.5

    conv_p, conv_s, lruh_p, lruh_s, lruc_p, lruc_s = [], [], [], [], [], []
    k_p, v_p, k_s, v_s = [], [], [], []

    def layer_weights(i):
        kind, j = i % N_MIXERS, i // N_MIXERS
        if kind == 0:
            return (bf(a_w_in[j]), a_conv_w[j], bf(a_w_out[j]), ln_g[i], ln_b[i])
        if kind == 1:
            return (bf(r_w_in[j]), r_conv_w[j], r_conv_b[j], bf(r_w_ga[j]), r_b_ga[j],
                    bf(r_w_gx[j]), r_b_gx[j], r_lru_param[j], bf(r_w_out[j]), ln_g[i], ln_b[i])
        return (bf(d_w_in[j]), bf(d_w_out[j]),
                [v[j].reshape(1, head_dim) for v in (d_lq1, d_lk1, d_lq2, d_lk2)],
                0.8 - 0.6 * math.exp(-0.3 * i))

    xp = x_prompt
    xs = to_time_major(x_sample)
    for i in range(depth):
        kind, j = i % N_MIXERS, i // N_MIXERS
        w = layer_weights(i)
        if kind == 0:
            width = a_conv_w.shape[1]
            xp, nbp = _conv_layer(xp, mod_p, i, jnp.zeros((bp, width - 1, d), F32), *w,
                                  s=1, tm=512, alpha=alpha)
            xs, nbs = _conv_layer(xs, mod_s, i, to_time_major(state_conv_a[j]), *w,
                                  s=bs, tm=rows_s, alpha=alpha)
            conv_p.append(nbp)
            conv_s.append(from_time_major(nbs, width - 1))
        elif kind == 1:
            width = r_conv_w.shape[1]
            xp, nbp, hp = _lru_layer(xp, mod_p, i, jnp.zeros((bp, width - 1, d), F32),
                                     jnp.zeros((bp, 1, d), F32), *w, s=1, tm=512, alpha=alpha)
            xs, nbs, hs = _lru_layer(xs, mod_s, i, to_time_major(state_lru_conv[j]),
                                     state_lru_h[j].reshape(1, bs, d), *w, s=bs, tm=rows_s,
                                     alpha=alpha)
            lruh_p.append(hp.reshape(bp, d))
            lruh_s.append(hs.reshape(bs, d))
            lruc_p.append(nbp)
            lruc_s.append(from_time_major(nbs, width - 1))
        else:
            w_in, w_out, lvec, lam_init = w
            tabs_p = _rope_tables(np.arange(tp), head_dim, rot_dim)
            kp, vp, zp, qb, kb, vt = _qkv_call(xp, mod_p, i, *tabs_p, w_in, tm=512, half=half,
                                               qscale=qscale * math.log2(math.e), attn_vd=vd,
                                               attn_tk=256)
            op = _flash_call(qb, kb, vt, *lvec, tq=1024, head_dim=head_dim, nh=4,
                             lam_init=lam_init)
            xp = _attn_out_call(xp, mod_p, i, op, zp, d_subln_g[j], w_out, ln_g[i], ln_b[i],
                                tm=1024, alpha=alpha, out_scale=1.0 - lam_init)
            k_p.append(kp.reshape(bp, tp, n_sub, head_dim))
            v_p.append(vp.reshape(bp, tp, n_heads, vd))
            pos_s = past_len + np.repeat(np.arange(ts), bs)
            tabs_s = _rope_tables(pos_s, head_dim, rot_dim)
            ks_, vs_, zs, qsb = _qkv_call(xs, mod_s, i, *tabs_s, w_in, tm=rows_s, half=half,
                                          qscale=qscale)
            ksn = from_time_major(ks_, ts)
            vsn = from_time_major(vs_, ts)
            n_layers, n_phys = cache_k.shape[0], cache_k.shape[1]
            cache_kt = jnp.transpose(cache_k, (0, 1, 3, 4, 2)).reshape(
                n_layers, n_phys, n_sub * head_dim, page)
            cache_vr = cache_v.reshape(n_layers, n_phys, page * n_heads, vd)
            cfg = _DecodeConfig(page_table, from_time_major(qsb, ts), cache_kt, cache_vr, j, ksn,
                                vsn.reshape(bs, ts * n_heads, vd), lvec, first=0, gp=16,
                                n_heads=n_heads, lam_init=lam_init)
            os_ = _decode_call(cfg, bs)
            xs = _attn_out_call(xs, mod_s, i, to_time_major(os_), zs, d_subln_g[j], w_out,
                                ln_g[i], ln_b[i], tm=rows_s, alpha=alpha,
                                out_scale=1.0 - lam_init)
            k_s.append(ksn.reshape(bs, ts, n_sub, head_dim))
            v_s.append(vsn.reshape(bs, ts, n_heads, vd))

    return (xp, from_time_major(xs, ts),
            jnp.stack(conv_p), jnp.stack(conv_s),
            jnp.stack(lruh_p), jnp.stack(lruh_s),
            jnp.stack(lruc_p), jnp.stack(lruc_s),
            jnp.stack(k_p), jnp.stack(v_p), jnp.stack(k_s), jnp.stack(v_s))
```

```python
import functools
import math

import jax
import jax.numpy as jnp
import numpy as np
from jax import lax
from jax.experimental import pallas as pl
from jax.experimental.pallas import tpu as pltpu

F32 = jnp.float32
BF16 = jnp.bfloat16

LN_EPS = 1e-5
SUBLN_EPS = 1e-5
LRU_C = 8.0
ROPE_THETA = 500000.0
N_MIXERS = 3

SUBLANES = 8
LANES = 128
VMEM_LIMIT_BYTES = 56 * 1024 * 1024
PAGE_BUFFERS = 2


def _cparams(semantics):
    return pltpu.CompilerParams(dimension_semantics=semantics, vmem_limit_bytes=VMEM_LIMIT_BYTES)


def _dot(a, b):
    return jnp.dot(a, b, preferred_element_type=F32)


def _dot_nt(a, b):
    return lax.dot_general(a, b, (((1,), (1,)), ((), ())), preferred_element_type=F32)


def _silu(z):
    return z * jax.nn.sigmoid(z)


def _layer_norm(y, g, b):
    mu = jnp.mean(y, axis=-1, keepdims=True)
    yc = y - mu
    var = jnp.mean(yc * yc, axis=-1, keepdims=True)
    return yc * lax.rsqrt(var + LN_EPS) * g + b


def _round_up(n, m):
    return (n + m - 1) // m * m


def _ada_body(c_ref, w_ref, b_ref, op_ref, os_ref, *, bp, bs, ts):
    c = c_ref[...]
    a = _silu(c).astype(BF16)
    m = _dot(a, w_ref[...].astype(BF16)) + b_ref[...]
    for r in range(bp):
        op_ref[r] = m[bs + r:bs + r + 1, :]
    for t in range(ts):
        os_ref[t * bs:(t + 1) * bs, :] = m[0:bs, :]


def _ada_call(c_all, w_ada, b_ada, *, bp, bs, ts):
    depth, d, d3 = w_ada.shape
    rows = c_all.shape[0]
    nt = d3 // d
    return pl.pallas_call(
        functools.partial(_ada_body, bp=bp, bs=bs, ts=ts),
        grid=(depth, nt),
        in_specs=[
            pl.BlockSpec((rows, d), lambda i, n: (0, 0)),
            pl.BlockSpec((None, d, d), lambda i, n: (i, 0, n)),
            pl.BlockSpec((None, 1, d), lambda i, n: (i, 0, n)),
        ],
        out_specs=[pl.BlockSpec((None, None, bp, 1, d), lambda i, n: (i, n, 0, 0, 0)),
                   pl.BlockSpec((None, None, None, ts * bs, d), lambda i, n: (i, n, 0, 0, 0))],
        out_shape=[jax.ShapeDtypeStruct((depth, nt, bp, 1, d), F32),
                   jax.ShapeDtypeStruct((depth, nt, 1, ts * bs, d), F32)],
        compiler_params=_cparams(("arbitrary", "arbitrary")),
        name="adaln",
    )(c_all, w_ada, b_ada.reshape(depth, 1, d3))


def _mod_spec(mod, layer, k):
    _, _, _, r, d = mod.shape
    return pl.BlockSpec((None, None, None, r, d), lambda b, t, *_: (layer, k, b, 0, 0))


def _load_history(t, buf, st0_ref, pad, ks, tm):
    @pl.when(t == 0)
    def _():
        buf[pad - ks:pad, :] = st0_ref[...]

    @pl.when(t > 0)
    def _():
        buf[pad - ks:pad, :] = buf[pad + tm - ks:pad + tm, :]


def _conv_taps(buf, cw_ref, cols, width, s, pad, tm):
    y = None
    if s % SUBLANES == 0:
        for k in range(width):
            r0 = pad - (width - 1 - k) * s
            term = cw_ref[k:k + 1, cols] * buf[r0:r0 + tm, cols]
            y = term if y is None else y + term
        return y
    full = buf[0:pad + tm, cols]
    for k in range(width):
        back = (width - 1 - k) * s
        src = full if back == 0 else pltpu.roll(full, back, axis=0)
        term = cw_ref[k:k + 1, cols] * src[pad:pad + tm, :]
        y = term if y is None else y + term
    return y


def _conv_body(x_ref, sh_ref, sc_ref, gt_ref, st0_ref, win_ref, cw_ref, wout_ref, g_ref, b_ref,
               y_ref, st_ref, abuf, acc, *, s, tm, cw, width, pad, alpha):
    d = x_ref.shape[-1]
    ks = (width - 1) * s
    t = pl.program_id(1)
    _load_history(t, abuf, st0_ref, pad, ks, tm)
    x = x_ref[...]
    u = (x * (1.0 + sc_ref[...]) + sh_ref[...]).astype(BF16)
    for c in range(d // cw):
        cols = slice(c * cw, (c + 1) * cw)
        h = _dot(u, win_ref[:, c * cw:(c + 1) * cw])
        cg = _dot(u, win_ref[:, 2 * d + c * cw:2 * d + (c + 1) * cw])
        abuf[pad:pad + tm, cols] = cg * h
        y = _conv_taps(abuf, cw_ref, cols, width, s, pad, tm)
        bg = _dot(u, win_ref[:, d + c * cw:d + (c + 1) * cw])
        z = _dot(u, win_ref[:, 3 * d + c * cw:3 * d + (c + 1) * cw])
        gated = (_silu(z) * bg * y).astype(BF16)
        part = _dot(gated, wout_ref[cols, :])
        if c == 0:
            acc[...] = part
        else:
            acc[...] += part
    st_ref[...] = abuf[pad + tm - ks:pad + tm, :]
    res = alpha * x + gt_ref[...] * acc[...]
    y_ref[...] = _layer_norm(res, g_ref[...], b_ref[...])


def _conv_layer(x, mod, layer, st0, w_in, conv_w, w_out, ln_g, ln_b, *, s, tm, alpha):
    bsz, t_rows, d = x.shape
    width = conv_w.shape[0]
    ks = (width - 1) * s
    pad = _round_up(ks, SUBLANES)
    cw = 256
    body = functools.partial(_conv_body, s=s, tm=tm, cw=cw, width=width, pad=pad, alpha=alpha)
    row_spec = pl.BlockSpec((None, tm, d), lambda b, t, *_: (b, t, 0))
    st_spec = pl.BlockSpec((None, ks, d), lambda b, t, *_: (b, 0, 0))
    const = lambda shape: pl.BlockSpec(shape, lambda b, t, *_: (0,) * len(shape))
    return _call(
        body, (bsz, t_rows // tm),
        [row_spec, _mod_spec(mod, layer, 0), _mod_spec(mod, layer, 1),
         _mod_spec(mod, layer, 2), st_spec,
         const(w_in.shape), const(conv_w.shape), const(w_out.shape),
         const((1, d)), const((1, d))],
        [row_spec, st_spec],
        [jax.ShapeDtypeStruct((bsz, t_rows, d), F32), jax.ShapeDtypeStruct((bsz, ks, d), F32)],
        [pltpu.VMEM((pad + tm, d), F32), pltpu.VMEM((tm, d), F32)],
        (x, mod, mod, mod, st0, w_in, conv_w, w_out, ln_g.reshape(1, d), ln_b.reshape(1, d)),
        "conv_layer")


def _lru_body(x_ref, sh_ref, sc_ref, gt_ref, st0_ref, h0_ref, win_ref, cw_ref, cb_ref,
              wga_ref, bga_ref, wgx_ref, bgx_ref, prm_ref, wout_ref, g_ref, b_ref,
              y_ref, st_ref, hl_ref, xbuf, a_s, b_s, h_s, hc, *, s, tm, width, pad, alpha):
    d = x_ref.shape[-1]
    nblk, blk, _ = wga_ref.shape
    ks = (width - 1) * s
    t = pl.program_id(1)
    _load_history(t, xbuf, st0_ref, pad, ks, tm)

    @pl.when(t == 0)
    def _():
        hc[...] = h0_ref[...]

    x = x_ref[...]
    u = (x * (1.0 + sc_ref[...]) + sh_ref[...]).astype(BF16)
    xbuf[pad:pad + tm, :] = _dot(u, win_ref[:, 0:d])
    st_ref[...] = xbuf[pad + tm - ks:pad + tm, :]
    for n in range(nblk):
        cols = slice(n * blk, (n + 1) * blk)
        xc = _conv_taps(xbuf, cw_ref, cols, width, s, pad, tm) + cb_ref[:, cols]
        xcb = xc.astype(BF16)
        r = jax.nn.sigmoid(_dot(xcb, wga_ref[n]) + bga_ref[:, cols])
        gi = jax.nn.sigmoid(_dot(xcb, wgx_ref[n]) + bgx_ref[:, cols])
        log_a = r * (LRU_C * jax.nn.log_sigmoid(prm_ref[:, cols]))
        a = jnp.exp(log_a)
        a_s[:, cols] = a
        b_s[:, cols] = jnp.sqrt(-jnp.tanh(log_a) * (a * a + 1.0)) * (gi * xc)

    steps = tm // s

    def step(i, h):
        r0 = pl.multiple_of(i * s, s)
        h = a_s[pl.ds(r0, s), :] * h + b_s[pl.ds(r0, s), :]
        h_s[pl.ds(r0, s), :] = h
        return h

    h_last = lax.fori_loop(0, steps, step, hc[...], unroll=True)
    hc[...] = h_last
    hl_ref[...] = h_last

    z = _dot(u, win_ref[:, d:2 * d])
    yy = (h_s[...] * _silu(z)).astype(BF16)
    out = _dot(yy, wout_ref[...])
    res = alpha * x + gt_ref[...] * out
    y_ref[...] = _layer_norm(res, g_ref[...], b_ref[...])


def _lru_layer(x, mod, layer, st0, h0, w_in, conv_w, conv_b, w_ga, b_ga, w_gx, b_gx, prm, w_out,
               ln_g, ln_b, *, s, tm, alpha):
    bsz, t_rows, d = x.shape
    width = conv_w.shape[0]
    ks = (width - 1) * s
    pad = _round_up(ks, SUBLANES)
    body = functools.partial(_lru_body, s=s, tm=tm, width=width, pad=pad, alpha=alpha)
    row_spec = pl.BlockSpec((None, tm, d), lambda b, t, *_: (b, t, 0))
    st_spec = pl.BlockSpec((None, ks, d), lambda b, t, *_: (b, 0, 0))
    h_spec = pl.BlockSpec((None, s, d), lambda b, t, *_: (b, 0, 0))
    const = lambda shape: pl.BlockSpec(shape, lambda b, t, *_: (0,) * len(shape))
    vec = const((1, d))
    return _call(
        body, (bsz, t_rows // tm),
        [row_spec, _mod_spec(mod, layer, 0), _mod_spec(mod, layer, 1),
         _mod_spec(mod, layer, 2), st_spec, h_spec,
         const(w_in.shape), const(conv_w.shape), vec,
         const(w_ga.shape), vec, const(w_gx.shape), vec, vec,
         const(w_out.shape), vec, vec],
        [row_spec, st_spec, h_spec],
        [jax.ShapeDtypeStruct((bsz, t_rows, d), F32), jax.ShapeDtypeStruct((bsz, ks, d), F32),
         jax.ShapeDtypeStruct((bsz, s, d), F32)],
        [pltpu.VMEM((pad + tm, d), F32), pltpu.VMEM((tm, d), F32), pltpu.VMEM((tm, d), F32),
         pltpu.VMEM((tm, d), F32), pltpu.VMEM((s, d), F32)],
        (x, mod, mod, mod, st0, h0, w_in, conv_w, conv_b.reshape(1, d), w_ga,
         b_ga.reshape(1, d), w_gx, b_gx.reshape(1, d), prm.reshape(1, d), w_out,
         ln_g.reshape(1, d), ln_b.reshape(1, d)),
        "lru_layer")


def _rope_block(xb, cos, sin_lo, sin_hi, half):
    return (xb * cos + pltpu.roll(xb, LANES - half, axis=1) * sin_lo
            + pltpu.roll(xb, half, axis=1) * sin_hi)


def _qkv_body(x_ref, sh_ref, sc_ref, cos_ref, slo_ref, shi_ref, win_ref,
              k_ref, v_ref, z_ref, qb_ref, *attn_refs, half, qscale):
    d = x_ref.shape[-1]
    x = x_ref[...]
    u = (x * (1.0 + sc_ref[...]) + sh_ref[...]).astype(BF16)
    cos, slo, shi = cos_ref[...], slo_ref[...], shi_ref[...]
    q = _dot(u, win_ref[:, 0:d])
    for j in range(d // LANES):
        cols = slice(j * LANES, (j + 1) * LANES)
        qb_ref[:, cols] = (_rope_block(q[:, cols], cos, slo, shi, half) * qscale).astype(BF16)
    k = _dot(u, win_ref[:, d:2 * d])
    for j in range(d // LANES):
        cols = slice(j * LANES, (j + 1) * LANES)
        kr = _rope_block(k[:, cols], cos, slo, shi, half)
        k_ref[:, cols] = kr
        if attn_refs:
            attn_refs[0][:, cols] = kr.astype(BF16)
    v = _dot(u, win_ref[:, 2 * d:3 * d])
    v_ref[...] = v
    if attn_refs:
        vt_ref = attn_refs[1]
        _, n_kb, vd, tk = vt_ref.shape
        for hv in range(vt_ref.shape[0]):
            for c in range(n_kb):
                vt_ref[hv, c] = v[c * tk:(c + 1) * tk, hv * vd:(hv + 1) * vd].T.astype(BF16)
    z_ref[...] = _dot(u, win_ref[:, 3 * d:4 * d]).astype(z_ref.dtype)


def _qkv_call(x, mod, layer, cos_t, slo_t, shi_t, w_in, *, tm, half, qscale, attn_vd=None,
              attn_tk=None):
    bsz, t_rows, d = x.shape
    body = functools.partial(_qkv_body, half=half, qscale=qscale)
    row_spec = pl.BlockSpec((None, tm, d), lambda b, t: (b, t, 0))
    tab_spec = pl.BlockSpec((tm, LANES), lambda b, t: (t, 0))
    f32_out = jax.ShapeDtypeStruct((bsz, t_rows, d), F32)
    bf_out = jax.ShapeDtypeStruct((bsz, t_rows, d), BF16)
    out_specs = [row_spec] * 4
    out_shape = [f32_out, f32_out, bf_out, bf_out]
    if attn_vd is not None:
        n_heads = d // attn_vd
        out_specs += [row_spec, pl.BlockSpec((None, n_heads, tm // attn_tk, attn_vd, attn_tk),
                                             lambda b, t: (b, 0, t, 0, 0))]
        out_shape += [bf_out, jax.ShapeDtypeStruct(
            (bsz, n_heads, t_rows // attn_tk, attn_vd, attn_tk), BF16)]
    return pl.pallas_call(
        body,
        grid=(bsz, t_rows // tm),
        in_specs=[row_spec, _mod_spec(mod, layer, 0), _mod_spec(mod, layer, 1),
                  tab_spec, tab_spec, tab_spec,
                  pl.BlockSpec(w_in.shape, lambda b, t: (0, 0))],
        out_specs=out_specs,
        out_shape=out_shape,
        compiler_params=_cparams(("arbitrary", "arbitrary")),
        name="attn_qkv",
    )(x, mod, mod, cos_t, slo_t, shi_t, w_in)


def _rope_tables(pos, head_dim, rot_dim):
    half = rot_dim // 2
    inv_freq = np.exp(np.arange(half, dtype=np.float64) * (-2.0 * math.log(ROPE_THETA) / rot_dim))
    ang = np.asarray(pos, dtype=np.float64)[:, None] * inv_freq[None, :]
    cos, sin = np.cos(ang), np.sin(ang)
    n = ang.shape[0]
    ones = np.ones((n, head_dim - rot_dim))
    zeros = np.zeros((n, head_dim - rot_dim))
    zh = np.zeros((n, half))
    cos_h = np.concatenate([cos, cos, ones], axis=1)
    slo_h = np.concatenate([-sin, zh, zeros], axis=1)
    shi_h = np.concatenate([zh, sin, zeros], axis=1)
    rep = LANES // head_dim
    return tuple(jnp.asarray(np.tile(t, (1, rep)), dtype=F32) for t in (cos_h, slo_h, shi_h))


def _diff_lambda(lq1_ref, lk1_ref, lq2_ref, lk2_ref, lam_init):
    s1 = jnp.sum(lq1_ref[...] * lk1_ref[...], axis=-1, keepdims=True)
    s2 = jnp.sum(lq2_ref[...] * lk2_ref[...], axis=-1, keepdims=True)
    return jnp.exp(s1) - jnp.exp(s2) + lam_init


def _flash_body(q_ref, k_ref, vt_ref, lq1_ref, lk1_ref, lq2_ref, lk2_ref, o_ref,
                qq_s, sa_s, sb_s, m_s, acc_s, *, tq, head_dim, nh, lam_init):
    qi = pl.program_id(2)
    vd, tk = vt_ref.shape[2], vt_ref.shape[3]
    nd = tq // tk
    lane = lax.broadcasted_iota(jnp.int32, (tq, LANES), 1)
    for h in range(nh):
        q = q_ref[:, h * LANES:(h + 1) * LANES]
        zero = jnp.zeros_like(q)
        qq_s[h, 0:tq, :] = jnp.where(lane < head_dim, q, zero)
        qq_s[h, tq:2 * tq, :] = jnp.where(lane >= head_dim, q, zero)
    m_s[...] = jnp.full(m_s.shape, -jnp.inf, F32)
    acc_s[...] = jnp.zeros(acc_s.shape, F32)
    ones = jnp.ones((2 * SUBLANES, tk), BF16)

    def scores(j, s_ref, q0=0):
        r0 = pl.multiple_of(j * tk, tk)
        for h in range(nh):
            k = k_ref[pl.ds(r0, tk), h * LANES:(h + 1) * LANES]
            if q0 == 0:
                s_ref[h] = _dot_nt(k, qq_s[h])
            else:
                qq = jnp.concatenate([qq_s[h, q0:tq, :], qq_s[h, tq + q0:2 * tq, :]], axis=0)
                s_ref[h, :, 0:2 * (tq - q0)] = _dot_nt(k, qq)

    def consume(j, s_ref, diag=False, q0=0):
        w = tq - q0
        for h in range(nh):
            v1 = jnp.concatenate([vt_ref[h, j], ones], axis=0)
            for sub in range(2):
                for cc in range(w // tk):
                    src = slice(sub * w + cc * tk, sub * w + (cc + 1) * tk)
                    dst = slice(sub * tq + q0 + cc * tk, sub * tq + q0 + (cc + 1) * tk)
                    st = s_ref[h, :, src]
                    if diag and cc == 0:
                        key = lax.broadcasted_iota(jnp.int32, st.shape, 0)
                        row = lax.broadcasted_iota(jnp.int32, st.shape, 1)
                        st = jnp.where(key <= row, st, -jnp.inf)
                    m_prev = m_s[h, :, dst]
                    m_new = jnp.maximum(m_prev, jnp.max(st, axis=0, keepdims=True))
                    alpha = jnp.exp2(m_prev - m_new)
                    pt = jnp.exp2(st - m_new).astype(BF16)
                    acc_s[h, :, dst] = alpha * acc_s[h, :, dst] + _dot(v1, pt)
                    m_s[h, :, dst] = m_new

    scores(0, sa_s)

    def pair(i, carry):
        j = 2 * i
        scores(j + 1, sb_s)
        consume(j, sa_s)
        scores(j + 2, sa_s)
        consume(j + 1, sb_s)
        return carry

    lax.fori_loop(0, qi * (nd // 2), pair, 0)

    bufs = (sa_s, sb_s)
    for g in range(nd):
        j = nd * qi + g
        if g + 1 < nd:
            scores(j + 1, bufs[(g + 1) % 2], q0=(g + 1) * tk)
        consume(j, bufs[g % 2], diag=True, q0=g * tk)

    lam = _diff_lambda(lq1_ref, lk1_ref, lq2_ref, lk2_ref, lam_init)
    for h in range(nh):
        acc = acc_s[h]
        ot = acc[0:vd, :] / acc[vd:vd + 1, :]
        o_ref[:, h * LANES:(h + 1) * LANES] = (
            (ot[:, 0:tq] - lam * ot[:, tq:2 * tq]).T.astype(o_ref.dtype))


def _flash_call(qb, kb, vt, lq1, lk1, lq2, lk2, *, tq, head_dim, nh, lam_init):
    bsz, t_rows, d = qb.shape
    _, n_heads, n_blk, vd, tk = vt.shape
    assert tq % (2 * tk) == 0 and n_blk * tk == t_rows and vd == LANES
    body = functools.partial(_flash_body, tq=tq, head_dim=head_dim, nh=nh, lam_init=lam_init)
    q_spec = pl.BlockSpec((None, tq, nh * LANES), lambda b, h, i, *_: (b, i, h))
    k_spec = pl.BlockSpec((None, t_rows, nh * LANES), lambda b, h, i, *_: (b, 0, h))
    vt_spec = pl.BlockSpec((None, nh, n_blk, vd, tk), lambda b, h, i, *_: (b, h, 0, 0, 0))
    l_spec = pl.BlockSpec((1, head_dim), lambda b, h, i, *_: (0, 0))
    return _call(
        body, (bsz, n_heads // nh, t_rows // tq),
        [q_spec, k_spec, vt_spec, l_spec, l_spec, l_spec, l_spec],
        q_spec,
        jax.ShapeDtypeStruct((bsz, t_rows, d), BF16),
        [pltpu.VMEM((nh, 2 * tq, LANES), BF16), pltpu.VMEM((nh, tk, 2 * tq), F32),
         pltpu.VMEM((nh, tk, 2 * tq), F32), pltpu.VMEM((nh, 1, 2 * tq), F32),
         pltpu.VMEM((nh, vd + 2 * SUBLANES, 2 * tq), F32)],
        (qb, kb, vt, lq1, lk1, lq2, lk2),
        "attn_flash")


class _DecodeConfig:
    def __init__(self, page_table, qs, cache_kt, cache_vr, layer, k_new, v_new, lvec, *, first,
                 gp, n_heads, lam_init):
        self.pt = page_table.reshape(-1)
        self.n_pages = page_table.shape[1]
        self.qs, self.kt, self.vr, self.layer = qs, cache_kt, cache_vr, layer
        self.k_new, self.v_new, self.lvec = k_new, v_new, list(lvec)
        self.first, self.gp, self.n_heads, self.lam_init = first, gp, n_heads, lam_init
        self.kd, self.page = cache_kt.shape[2], cache_kt.shape[3]
        self.vd = cache_vr.shape[-1]
        self.t_new = k_new.shape[1]
        self.head_dim = lvec[0].shape[-1]
        self.n_sub = self.kd // self.head_dim
        self.rows = self.n_sub * self.t_new
        self.spb = self.n_pages // gp

    def count(self, grid):
        assert len(grid) == 2 and grid[1] == self.spb
        return grid[0]

    def in_specs(self, grid):
        first = self.first
        self.count(grid)

        def per_seq(shape):
            return pl.BlockSpec((None,) + shape, lambda b, p, pt: (first + b, 0, 0))

        l_spec = pl.BlockSpec((1, self.head_dim), lambda *a: (0, 0))
        in_hbm = pl.BlockSpec(memory_space=pl.ANY)
        return [per_seq((self.t_new, self.kd)), in_hbm, in_hbm,
                per_seq((self.t_new, self.kd)), per_seq((self.t_new * self.n_heads, self.vd)),
                l_spec, l_spec, l_spec, l_spec]

    def args(self):
        return [self.qs, self.kt, self.vr, self.k_new, self.v_new] + self.lvec

    def out_spec(self, grid):
        return pl.BlockSpec((None, self.t_new, self.n_heads * self.vd),
                            lambda b, p, pt: (b, 0, 0))

    def out_shape(self, grid):
        return jax.ShapeDtypeStruct((self.count(grid), self.t_new, self.n_heads * self.vd), F32)

    def scratch_shapes(self):
        return [pltpu.VMEM((self.rows, self.kd), F32), pltpu.VMEM((self.rows, self.kd), BF16),
                pltpu.VMEM((self.rows, 1), F32), pltpu.VMEM((self.rows, 1), F32),
                pltpu.VMEM((self.rows, self.vd), F32),
                pltpu.VMEM((self.page, self.kd), F32),
                pltpu.VMEM((self.page * self.n_heads, self.vd), F32),
                pltpu.VMEM((PAGE_BUFFERS, self.gp, self.kd, self.page), F32),
                pltpu.VMEM((PAGE_BUFFERS, self.gp, self.page * self.n_heads, self.vd), F32),
                pltpu.SemaphoreType.DMA((PAGE_BUFFERS, 2))]


class _DecodeStep:
    def __init__(self, cfg, grid, pt_ref, in_refs, out_ref, scratch_refs):
        self.c = cfg
        self.pt_ref = pt_ref
        (self.q_ref, self.kt_hbm, self.vr_hbm, self.kn_ref, self.vn_ref) = in_refs[:5]
        self.l_refs = in_refs[5:]
        self.o_ref = out_ref
        (self.qf_s, self.qx_s, self.m_s, self.l_s, self.acc_s, self.kpad, self.vpad,
         self.kbuf, self.vbuf, self.sem) = scratch_refs
        self.b = pl.program_id(0)
        self.p = pl.program_id(1)
        self.n_steps = grid[0] * grid[1]
        self.step = self.b * grid[1] + self.p
        slot = lax.rem(self.step, PAGE_BUFFERS)
        self.k_refs = [self.kbuf.at[slot, g] for g in range(cfg.gp)]
        self.v_refs = [self.vbuf.at[slot, g] for g in range(cfg.gp)]

    def _page_copies(self, step):
        c = self.c
        slot = step % PAGE_BUFFERS if isinstance(step, int) else lax.rem(step, PAGE_BUFFERS)
        copies = []
        for g in range(c.gp):
            page = self.pt_ref[c.first * c.n_pages + step * c.gp + g]
            copies.append(pltpu.make_async_copy(self.kt_hbm.at[c.layer, page],
                                                self.kbuf.at[slot, g], self.sem.at[slot, 0]))
            copies.append(pltpu.make_async_copy(self.vr_hbm.at[c.layer, page],
                                                self.vbuf.at[slot, g], self.sem.at[slot, 1]))
        return copies

    def preamble(self):
        c = self.c
        ahead = PAGE_BUFFERS - 1

        @pl.when(self.step == 0)
        def _():
            for s in range(min(ahead, self.n_steps)):
                for cp in self._page_copies(s):
                    cp.start()

        @pl.when(self.step + ahead < self.n_steps)
        def _():
            for cp in self._page_copies(self.step + ahead):
                cp.start()

        for cp in self._page_copies(self.step):
            cp.wait()

        @pl.when((self.b == 0) & (self.p == 0))
        def _():
            self.kpad[...] = jnp.zeros(self.kpad.shape, F32)
            self.vpad[...] = jnp.zeros(self.vpad.shape, F32)

        @pl.when(self.p == 0)
        def _():
            self.m_s[...] = jnp.full(self.m_s.shape, -jnp.inf, F32)
            self.l_s[...] = jnp.zeros(self.l_s.shape, F32)
            self.acc_s[...] = jnp.zeros(self.acc_s.shape, F32)
            q = self.q_ref[...].astype(F32)
            lane = lax.broadcasted_iota(jnp.int32, q.shape, 1)
            for h in range(c.n_sub):
                own = (lane >= h * c.head_dim) & (lane < (h + 1) * c.head_dim)
                self.qf_s[h * c.t_new:(h + 1) * c.t_new, :] = jnp.where(own, q, 0.0)
            self.qx_s[...] = self.qf_s[...].astype(BF16)

    def _update(self, sc, vs):
        c = self.c
        m_prev = self.m_s[...]
        m_new = jnp.maximum(m_prev, jnp.max(sc, axis=1, keepdims=True))
        alpha = jnp.exp(m_prev - m_new)
        pe = jnp.exp(sc - m_new)
        self.l_s[...] = alpha * self.l_s[...] + jnp.sum(pe, axis=1, keepdims=True)
        self.m_s[...] = m_new
        pb = pe.astype(BF16)
        grp = 2 * c.t_new
        for hv in range(c.n_heads):
            rs = slice(hv * grp, (hv + 1) * grp)
            vh = jnp.concatenate(
                [v[pl.ds(hv, c.page, stride=c.n_heads), :].astype(BF16) for v in vs], axis=0)
            self.acc_s[rs, :] = alpha[rs, :] * self.acc_s[rs, :] + _dot(pb[rs, :], vh)

    def main(self):
        kt = jnp.concatenate([k[...].astype(BF16) for k in self.k_refs], axis=1)
        self._update(_dot(self.qx_s[...], kt), self.v_refs)

    def finalize(self):
        c = self.c

        @pl.when(self.p == c.spb - 1)
        def _():
            self.kpad[0:c.t_new, :] = self.kn_ref[...]
            self.vpad[0:c.t_new * c.n_heads, :] = self.vn_ref[...]
            sc = _dot_nt(self.qx_s[...], self.kpad[...].astype(BF16))
            row = lax.broadcasted_iota(jnp.int32, sc.shape, 0)
            col = lax.broadcasted_iota(jnp.int32, sc.shape, 1)
            self._update(jnp.where(col <= row % c.t_new, sc, -jnp.inf), [self.vpad])
            o = self.acc_s[...] / self.l_s[...]
            lam = _diff_lambda(*self.l_refs, c.lam_init)
            grp = 2 * c.t_new
            for hv in range(c.n_heads):
                o1 = o[hv * grp:hv * grp + c.t_new, :]
                o2 = o[hv * grp + c.t_new:(hv + 1) * grp, :]
                self.o_ref[:, hv * c.vd:(hv + 1) * c.vd] = o1 - lam * o2


def _decode_call(cfg, n_seq):
    grid = (n_seq, cfg.spb)
    in_specs = cfg.in_specs(grid)
    n_in = len(in_specs)

    def body(pt_ref, *refs):
        step = _DecodeStep(cfg, grid, pt_ref, refs[:n_in], refs[n_in], refs[n_in + 1:])
        step.preamble()
        step.main()
        step.finalize()

    grid_spec = pltpu.PrefetchScalarGridSpec(
        num_scalar_prefetch=1, grid=grid, in_specs=in_specs, out_specs=cfg.out_spec(grid),
        scratch_shapes=cfg.scratch_shapes())
    return pl.pallas_call(body, grid_spec=grid_spec, out_shape=cfg.out_shape(grid),
                          compiler_params=_cparams(("arbitrary", "arbitrary")),
                          name="attn_decode")(cfg.pt, *cfg.args())


def _call(body, grid, in_specs, out_specs, out_shape, scratch_shapes, args, name):
    return pl.pallas_call(body, grid=grid, in_specs=in_specs, out_specs=out_specs,
                          out_shape=out_shape, scratch_shapes=scratch_shapes,
                          compiler_params=_cparams(("arbitrary",) * len(grid)), name=name)(*args)


def _attn_out_body(x_ref, gt_ref, o_ref, z_ref, sg_ref, wout_ref, g_ref, b_ref, y_ref, gbuf, *,
                   alpha, out_scale):
    d = x_ref.shape[-1]
    vd = sg_ref.shape[-1]
    for hv in range(d // vd):
        cols = slice(hv * vd, (hv + 1) * vd)
        of = o_ref[:, cols].astype(F32)
        of = of * lax.rsqrt(jnp.mean(of * of, axis=-1, keepdims=True) + SUBLN_EPS)
        of = of * sg_ref[...] * out_scale
        gbuf[:, cols] = (of * _silu(z_ref[:, cols].astype(F32))).astype(BF16)
    out = _dot(gbuf[...], wout_ref[...])
    res = alpha * x_ref[...] + gt_ref[...] * out
    y_ref[...] = _layer_norm(res, g_ref[...], b_ref[...])


def _attn_out_call(x, mod, layer, o, z, subln_g, w_out, ln_g, ln_b, *, tm, alpha, out_scale):
    bsz, t_rows, d = x.shape
    vd = subln_g.shape[-1]
    body = functools.partial(_attn_out_body, alpha=alpha, out_scale=out_scale)
    row_spec = pl.BlockSpec((None, tm, d), lambda b, t: (b, t, 0))
    const = lambda shape: pl.BlockSpec(shape, lambda b, t: (0,) * len(shape))
    return pl.pallas_call(
        body,
        grid=(bsz, t_rows // tm),
        in_specs=[row_spec, _mod_spec(mod, layer, 2), row_spec, row_spec, const((1, vd)),
                  const(w_out.shape),
                  const((1, d)), const((1, d))],
        out_specs=row_spec,
        out_shape=jax.ShapeDtypeStruct((bsz, t_rows, d), F32),
        scratch_shapes=[pltpu.VMEM((tm, d), BF16)],
        compiler_params=_cparams(("arbitrary", "arbitrary")),
        name="attn_out",
    )(x, mod, o, z, subln_g.reshape(1, vd), w_out, ln_g.reshape(1, d), ln_b.reshape(1, d))


def kernel(x_prompt, x_sample, state_conv_a, state_lru_h, state_lru_conv, cache_k, cache_v, page_table, c_prompt, c_sample, w_ada, b_ada, ln_g, ln_b, a_w_in, a_conv_w, a_w_out, r_w_in, r_conv_w, r_conv_b, r_w_ga, r_b_ga, r_w_gx, r_b_gx, r_lru_param, r_w_out, d_w_in, d_lq1, d_lk1, d_lq2, d_lk2, d_subln_g, d_w_out):
    bp, tp, d = x_prompt.shape
    bs, ts, _ = x_sample.shape
    depth = w_ada.shape[0]
    n_pages = page_table.shape[1]
    page = cache_k.shape[2]
    n_sub, head_dim = cache_k.shape[3], cache_k.shape[4]
    n_heads, vd = cache_v.shape[3], cache_v.shape[4]
    past_len = n_pages * page
    rot_dim = head_dim // 4
    alpha = (2 * depth) ** 0.25
    rows_s = bs * ts

    n_c = _round_up(bp + bs, SUBLANES)
    c_all = jnp.concatenate([c_sample, c_prompt, jnp.zeros((n_c - bp - bs, d), F32)], axis=0)
    mod_p, mod_s = _ada_call(c_all, w_ada, b_ada, bp=bp, bs=bs, ts=ts)

    def to_time_major(a):
        return jnp.swapaxes(a, 0, 1).reshape((1, a.shape[1] * bs) + a.shape[2:])

    def from_time_major(a, n):
        return jnp.swapaxes(a.reshape(n, bs, a.shape[-1]), 0, 1)

    bf = lambda w: w.astype(BF16)
    half = rot_dim // 2
    qscale = head_dim ** -0.5

    conv_p, conv_s, lruh_p, lruh_s, lruc_p, lruc_s = [], [], [], [], [], []
    k_p, v_p, k_s, v_s = [], [], [], []

    def layer_weights(i):
        kind, j = i % N_MIXERS, i // N_MIXERS
        if kind == 0:
            return (bf(a_w_in[j]), a_conv_w[j], bf(a_w_out[j]), ln_g[i], ln_b[i])
        if kind == 1:
            return (bf(r_w_in[j]), r_conv_w[j], r_conv_b[j], bf(r_w_ga[j]), r_b_ga[j],
                    bf(r_w_gx[j]), r_b_gx[j], r_lru_param[j], bf(r_w_out[j]), ln_g[i], ln_b[i])
        return (bf(d_w_in[j]), bf(d_w_out[j]),
                [v[j].reshape(1, head_dim) for v in (d_lq1, d_lk1, d_lq2, d_lk2)],
                0.8 - 0.6 * math.exp(-0.3 * i))

    xp = x_prompt
    xs = to_time_major(x_sample)
    for i in range(depth):
        kind, j = i % N_MIXERS, i // N_MIXERS
        w = layer_weights(i)
        if kind == 0:
            width = a_conv_w.shape[1]
            xp, nbp = _conv_layer(xp, mod_p, i, jnp.zeros((bp, width - 1, d), F32), *w,
                                  s=1, tm=512, alpha=alpha)
            xs, nbs = _conv_layer(xs, mod_s, i, to_time_major(state_conv_a[j]), *w,
                                  s=bs, tm=rows_s, alpha=alpha)
            conv_p.append(nbp)
            conv_s.append(from_time_major(nbs, width - 1))
        elif kind == 1:
            width = r_conv_w.shape[1]
            xp, nbp, hp = _lru_layer(xp, mod_p, i, jnp.zeros((bp, width - 1, d), F32),
                                     jnp.zeros((bp, 1, d), F32), *w, s=1, tm=1024, alpha=alpha)
            xs, nbs, hs = _lru_layer(xs, mod_s, i, to_time_major(state_lru_conv[j]),
                                     state_lru_h[j].reshape(1, bs, d), *w, s=bs, tm=rows_s,
                                     alpha=alpha)
            lruh_p.append(hp.reshape(bp, d))
            lruh_s.append(hs.reshape(bs, d))
            lruc_p.append(nbp)
            lruc_s.append(from_time_major(nbs, width - 1))
        else:
            w_in, w_out, lvec, lam_init = w
            tabs_p = _rope_tables(np.arange(tp), head_dim, rot_dim)
            kp, vp, zp, qb, kb, vt = _qkv_call(xp, mod_p, i, *tabs_p, w_in, tm=512, half=half,
                                               qscale=qscale * math.log2(math.e), attn_vd=vd,
                                               attn_tk=256)
            op = _flash_call(qb, kb, vt, *lvec, tq=1024, head_dim=head_dim, nh=4,
                             lam_init=lam_init)
            xp = _attn_out_call(xp, mod_p, i, op, zp, d_subln_g[j], w_out, ln_g[i], ln_b[i],
                                tm=1024, alpha=alpha, out_scale=1.0 - lam_init)
            k_p.append(kp.reshape(bp, tp, n_sub, head_dim))
            v_p.append(vp.reshape(bp, tp, n_heads, vd))
            pos_s = past_len + np.repeat(np.arange(ts), bs)
            tabs_s = _rope_tables(pos_s, head_dim, rot_dim)
            ks_, vs_, zs, qsb = _qkv_call(xs, mod_s, i, *tabs_s, w_in, tm=rows_s, half=half,
                                          qscale=qscale)
            ksn = from_time_major(ks_, ts)
            vsn = from_time_major(vs_, ts)
            n_layers, n_phys = cache_k.shape[0], cache_k.shape[1]
            cache_kt = jnp.transpose(cache_k, (0, 1, 3, 4, 2)).reshape(
                n_layers, n_phys, n_sub * head_dim, page)
            cache_vr = cache_v.reshape(n_layers, n_phys, page * n_heads, vd)
            cfg = _DecodeConfig(page_table, from_time_major(qsb, ts), cache_kt, cache_vr, j, ksn,
                                vsn.reshape(bs, ts * n_heads, vd), lvec, first=0, gp=16,
                                n_heads=n_heads, lam_init=lam_init)
            os_ = _decode_call(cfg, bs)
            xs = _attn_out_call(xs, mod_s, i, to_time_major(os_), zs, d_subln_g[j], w_out,
                                ln_g[i], ln_b[i], tm=rows_s, alpha=alpha,
                                out_scale=1.0 - lam_init)
            k_s.append(ksn.reshape(bs, ts, n_sub, head_dim))
            v_s.append(vsn.reshape(bs, ts, n_heads, vd))

    return (xp, from_time_major(xs, ts),
            jnp.stack(conv_p), jnp.stack(conv_s),
            jnp.stack(lruh_p), jnp.stack(lruh_s),
            jnp.stack(lruc_p), jnp.stack(lruc_s),
            jnp.stack(k_p), jnp.stack(v_p), jnp.stack(k_s), jnp.stack(v_s))
```
